```python
import functools
import jax, jax.numpy as jnp
from jax import lax
import numpy as np

D_MODEL = 1024
BATCH = 8
SEQ = 2048
DEPTH = 4
DEC_BATCH = 32
DEC_SEQ = 4
PAST_LEN = 8192
PAGE_SIZE = 128

H_A = 4
DK_A = 128
DV_A = 128
W_A = H_A * DV_A
CHUNK_A = 128
H_B = 8
D_B = 64
W_B = H_B * D_B
H_I = 8
D_I = 64
TOPK_MAX = 256
Q_BLOCK = 128
H_C = 8
D_C = 64
W_C = H_C * D_C
LORA_W = 64
LORA_A = 64
LORA_G = 128
RWKV_GN_EPS = 64e-5
N_BRANCH = 3
BRANCH_W = 512
D_FF = 2816
N_EXPERTS = 8
TOP_K_E = 2
D_FF_E = 1408
N_DENSE = (DEPTH + 1) // 2
N_MOE = DEPTH // 2
ALPHA = (2.0 * DEPTH) ** 0.25
BETA = (8.0 * DEPTH) ** -0.25
LN_EPS = 1e-5

SIZES_A = (H_A * DK_A, H_A * DK_A, W_A, W_A)
SIZES_B = (W_B, W_B, W_B, H_I * D_I, H_I, D_I)
SIZES_C = (W_C, W_C, W_C, LORA_W, LORA_A, LORA_G)
D_C_IN = sum(SIZES_C)
IN_SIZES = SIZES_A + SIZES_B + (D_C_IN, N_BRANCH * D_MODEL)
D_IN = sum(IN_SIZES)

kernel_name = 'hybrid_retention_dsa_rwkv7_decoder_step'


def _split(a, sizes):
    return jnp.split(a, np.cumsum(sizes)[:-1].tolist(), axis=-1)


def _norm(x, eps=LN_EPS):
    xf = x.astype(jnp.float32)
    mu = jnp.mean(xf, axis=-1, keepdims=True)
    var = jnp.mean(jnp.square(xf - mu), axis=-1, keepdims=True)
    return ((xf - mu) * lax.rsqrt(var + eps)).astype(x.dtype)


def layer_norm(x, g, b):
    return _norm(x) * g + b


def swiglu(h, w1, w3, w2):
    return (jax.nn.silu(h @ w1) * (h @ w3)) @ w2


def moe_swiglu(h, router, router_b, w1, w3, w2):
    logits = (h @ router).astype(jnp.float32) + router_b.astype(jnp.float32)
    top_p, top_i = lax.top_k(jax.nn.softmax(logits, axis=-1), TOP_K_E)
    top_p = top_p / jnp.sum(top_p, axis=-1, keepdims=True)
    gate = jnp.sum(jax.nn.one_hot(top_i, N_EXPERTS, dtype=jnp.float32) * top_p[..., None], axis=-2).astype(h.dtype)
    out = jnp.zeros_like(h)
    for e in range(N_EXPERTS):
        out = out + gate[..., e:e + 1] * swiglu(h, w1[e], w3[e], w2[e])
    return out


def _retention_chunk(q, k, v, s0, log_gamma):
    n = q.shape[1]
    pos = jnp.arange(n, dtype=jnp.float32)
    rel = pos[:, None] - pos[None, :]
    decay = jnp.where(rel >= 0, jnp.exp(log_gamma[:, None, None] * jnp.maximum(rel, 0.0)), 0.0)
    scores = jnp.einsum('bthd,bshd->bhts', q, k) * decay
    inner = jnp.einsum('bhts,bshe->bthe', scores, v)
    xi = jnp.exp(log_gamma[None, :] * (pos[:, None] + 1.0))
    cross = jnp.einsum('bthd,bhde->bthe', q, s0) * xi[:, :, None]
    zeta = jnp.exp(log_gamma[None, :] * (n - 1.0 - pos[:, None]))
    s_new = jnp.exp(log_gamma * n)[:, None, None] * s0 + jnp.einsum('bshd,bshe->bhde', k * zeta[:, :, None], v)
    return inner + cross, s_new


def retention_branch(qa, ka, va, ga, s0, gn_g):
    B, T, _ = qa.shape
    f32 = jnp.float32
    log_gamma = jnp.log1p(-jnp.exp2(-5.0 - jnp.arange(H_A, dtype=f32)))
    q = qa.reshape(B, T, H_A, DK_A).astype(f32) * DK_A ** -0.5
    k = ka.reshape(B, T, H_A, DK_A).astype(f32)
    v = va.reshape(B, T, H_A, DV_A).astype(f32)
    c = CHUNK_A if T % CHUNK_A == 0 else T
    n = T // c

    def chunks(a):
        return a.reshape(B, n, c, H_A, a.shape[-1]).swapaxes(0, 1)

    def step(s, qkv):
        o, s_new = _retention_chunk(qkv[0], qkv[1], qkv[2], s, log_gamma)
        return s_new, o

    s_fin, o = lax.scan(step, s0.astype(f32), (chunks(q), chunks(k), chunks(v)))
    o = o.swapaxes(0, 1).reshape(B, T, H_A, DV_A)
    o = (_norm(o) * gn_g.astype(f32)).reshape(B, T, W_A).astype(qa.dtype)
    return jax.nn.silu(ga) * o, s_fin.astype(qa.dtype)


def _alibi_slopes(h):
    return jnp.exp2(-8.0 * (jnp.arange(h, dtype=jnp.float32) + 1.0) / h)


def _indexer_scores(iq, iw, ik):
    s = jax.nn.relu(jnp.einsum('bthd,bsd->bths', iq, ik).astype(jnp.float32))
    return jnp.einsum('bths,bth->bts', s, iw.astype(jnp.float32))


def _sparse_attend(q, ksel, vsel, qpos, kidx, valid):
    logits = jnp.einsum('bthd,btkhd->bthk', q, ksel).astype(jnp.float32) * D_B ** -0.5
    dist = (qpos[None, :, None] - kidx).astype(jnp.float32)
    logits = logits - _alibi_slopes(H_B)[None, None, :, None] * dist[:, :, None, :]
    logits = jnp.where(valid[:, :, None, :], logits, -jnp.inf)
    p = jax.nn.softmax(logits, axis=-1).astype(q.dtype)
    return jnp.einsum('bthk,btkhd->bthd', p, vsel)


def dsa_prompt(q, k, v, iq, iw, ik):
    B, T = q.shape[:2]
    top = min(TOPK_MAX, T // 4)
    kpos = jnp.arange(T)
    gather = jax.vmap(lambda a, i: a[i])

    def block(i):
        t0 = i * Q_BLOCK
        qb = lax.dynamic_slice_in_dim(q, t0, Q_BLOCK, axis=1)
        iqb = lax.dynamic_slice_in_dim(iq, t0, Q_BLOCK, axis=1)
        iwb = lax.dynamic_slice_in_dim(iw, t0, Q_BLOCK, axis=1)
        qpos = t0 + jnp.arange(Q_BLOCK)
        sc = _indexer_scores(iqb, iwb, ik)
        sc = jnp.where(kpos[None, None, :] <= qpos[None, :, None], sc, -jnp.inf)
        _, idx = lax.top_k(sc, top)
        valid = idx <= qpos[None, :, None]
        return _sparse_attend(qb, gather(k, idx), gather(v, idx), qpos, idx, valid)

    o = lax.map(block, jnp.arange(T // Q_BLOCK))
    return o.swapaxes(0, 1).reshape(B, T, H_B, D_B)


def dsa_sample(q, k, v, iq, iw, ik, cache_k, cache_v, cache_kidx, page_table, layer):
    B, T = q.shape[:2]
    past = page_table.shape[1] * PAGE_SIZE
    top = min(TOPK_MAX, (past + T) // 4)
    ik_past = cache_kidx[layer, page_table].reshape(B, past, D_I).astype(ik.dtype)
    ik_all = jnp.concatenate([ik_past, ik], axis=1)
    qpos = past + jnp.arange(T)
    kpos = jnp.arange(past + T)
    sc = _indexer_scores(iq, iw, ik_all)
    sc = jnp.where(kpos[None, None, :] <= qpos[None, :, None], sc, -jnp.inf)
    _, idx = lax.top_k(sc, top)
    valid = idx <= qpos[None, :, None]
    in_past = idx < past
    pidx = jnp.minimum(idx, past - 1)
    bidx = jnp.arange(B)[:, None, None]
    phys = page_table[bidx, pidx // PAGE_SIZE]
    off = pidx % PAGE_SIZE
    nidx = jnp.clip(idx - past, 0, T - 1)

    def pick(cache, new):
        old_rows = cache[layer, phys, off].astype(new.dtype)
        return jnp.where(in_past[..., None, None], old_rows, new[bidx, nidx])

    return _sparse_attend(q, pick(cache_k, k), pick(cache_v, v), qpos, idx, valid)


def _rwkv_step(s, inp):
    r, w, k, v, a, b = inp
    sa = jnp.einsum('bhvk,bhk->bhv', s, a)
    s = s * w[:, :, None, :] + sa[..., None] * b[:, :, None, :] + v[..., None] * k[:, :, None, :]
    return s, jnp.einsum('bhvk,bhk->bhv', s, r)


def rwkv_branch(pc, shift0, s0, mu, w0, w2, a0, a2, g2, k_k, k_a, r_k, lnx_g, lnx_b):
    B, T, _ = pc.shape
    f32 = jnp.float32
    prev = jnp.concatenate([shift0[:, None, :].astype(pc.dtype), pc[:, :-1]], axis=1)
    pm = pc + (prev - pc) * mu
    r, k, v, lw, la, lg = _split(pm, SIZES_C)
    w = -jax.nn.softplus(-(w0 + jnp.tanh(lw) @ w2)) - 0.5
    a = jax.nn.sigmoid(a0 + la @ a2)
    g = jax.nn.sigmoid(lg) @ g2

    def heads(z):
        return z.reshape(B, T, H_C, D_C).astype(f32)

    kk = heads(k * k_k)
    kk = kk / jnp.maximum(jnp.sqrt(jnp.sum(kk * kk, axis=-1, keepdims=True)), 1e-12)
    k = k * (1.0 + (a - 1.0) * k_a)
    rh, kh, vh, ah = heads(r), heads(k), heads(v), heads(a)
    decay = jnp.exp(-jnp.exp(heads(w)))
    xs = tuple(z.swapaxes(0, 1) for z in (rh, decay, kh, vh, -kk, kk * ah))
    s_fin, y = lax.scan(_rwkv_step, s0.astype(f32), xs)
    y = y.swapaxes(0, 1)
    y = _norm(y, RWKV_GN_EPS) * lnx_g.astype(f32) + lnx_b.astype(f32)
    y = y + jnp.sum(rh * kh * r_k.astype(f32), axis=-1, keepdims=True) * vh
    out = y.reshape(B, T, W_C).astype(pc.dtype) * g
    return out, s_fin.astype(pc.dtype), pc[:, -1]


def trunk_layer(x, c, l, p, attend, ret_s0, wkv_s0, shift0):
    B, T, _ = x.shape
    ada = jax.nn.silu(c) @ p['ada_w'][l] + p['ada_b'][l]
    sh1, sc1, gt1, sh2, sc2, gt2 = [z[:, None, :] for z in jnp.split(ada, 6, axis=-1)]

    h = x * (1.0 + sc1) + sh1
    qa, ka, va, ga, qb, kb, vb, iq, iw, ik, pc, gl = _split(h @ p['w_in'][l], IN_SIZES)
    o_a, ret_s = retention_branch(qa, ka, va, ga, ret_s0, p['ret_gn_g'][l])
    qh = qb.reshape(B, T, H_B, D_B)
    kh = kb.reshape(B, T, H_B, D_B)
    vh = vb.reshape(B, T, H_B, D_B)
    o_b = attend(qh, kh, vh, iq.reshape(B, T, H_I, D_I) * D_I ** -0.5, iw * H_I ** -0.5, ik).reshape(B, T, W_B)
    o_c, wkv_s, shift_s = rwkv_branch(pc, shift0, wkv_s0, p['rw_mu'][l], p['rw_w0'][l], p['rw_w2'][l],
                                      p['rw_a0'][l], p['rw_a2'][l], p['rw_g2'][l], p['rw_kk'][l],
                                      p['rw_ka'][l], p['rw_rk'][l], p['rw_lnx_g'][l], p['rw_lnx_b'][l])
    br = jnp.einsum('btgw,gwd->btgd', jnp.stack([o_a, o_b, o_c], axis=2), p['w_branch'][l])
    mixed = jnp.sum(jax.nn.sigmoid(gl.reshape(B, T, N_BRANCH, D_MODEL)) * br, axis=2)
    x = layer_norm(ALPHA * x + gt1 * (mixed @ p['w_out'][l]), p['ln1_g'][l], p['ln1_b'][l])

    h = x * (1.0 + sc2) + sh2
    i = l // 2
    if l % 2 == 0:
        f = swiglu(h, p['ffn_w1'][i], p['ffn_w3'][i], p['ffn_w2'][i])
    else:
        f = moe_swiglu(h, p['moe_router'][i], p['moe_router_b'][i], p['moe_w1'][i], p['moe_w3'][i], p['moe_w2'][i])
    x = layer_norm(ALPHA * x + gt2 * f, p['ln2_g'][l], p['ln2_b'][l])
    return x, (kh, vh, ik, ret_s, wkv_s, shift_s)


def setup_inputs(seed: int = 0) -> dict:
    key = jax.random.key(seed)
    keys = iter(jax.random.split(key, 64))
    f32 = jnp.float32

    def nrm(shape, scale=1.0):
        return jax.random.normal(next(keys), shape, f32) * scale

    n_pages = PAST_LEN // PAGE_SIZE
    n_pool = (DEC_BATCH * n_pages * 5) // 4
    inp = {}
    inp['x_prompt'] = nrm((BATCH, SEQ, D_MODEL))
    inp['x_sample'] = nrm((DEC_BATCH, DEC_SEQ, D_MODEL))
    inp['c_prompt'] = nrm((BATCH, D_MODEL))
    inp['c_sample'] = nrm((DEC_BATCH, D_MODEL))
    inp['cache_k'] = nrm((DEPTH, n_pool, PAGE_SIZE, H_B, D_B))
    inp['cache_v'] = nrm((DEPTH, n_pool, PAGE_SIZE, H_B, D_B))
    inp['cache_kidx'] = nrm((DEPTH, n_pool, PAGE_SIZE, D_I))
    inp['page_table'] = jax.random.permutation(next(keys), n_pool)[:DEC_BATCH * n_pages].reshape(DEC_BATCH, n_pages).astype(jnp.int32)
    inp['state_ret'] = nrm((DEPTH, DEC_BATCH, H_A, DK_A, DV_A))
    inp['state_wkv'] = nrm((DEPTH, DEC_BATCH, H_C, D_C, D_C), 0.5)
    inp['state_shift'] = nrm((DEPTH, DEC_BATCH, D_C_IN))
    inp['ada_w'] = nrm((DEPTH, D_MODEL, 6 * D_MODEL), 0.5 * D_MODEL ** -0.5)
    inp['ada_b'] = nrm((DEPTH, 6 * D_MODEL), 0.02)
    inp['w_in'] = nrm((DEPTH, D_MODEL, D_IN), D_MODEL ** -0.5)
    inp['ret_gn_g'] = 1.0 + nrm((DEPTH, H_A, DV_A), 0.02)
    inp['rw_mu'] = jax.random.uniform(next(keys), (DEPTH, D_C_IN), f32)
    inp['rw_w0'] = -1.0 + nrm((DEPTH, W_C), 0.5)
    inp['rw_w2'] = nrm((DEPTH, LORA_W, W_C), 0.5 * LORA_W ** -0.5)
    inp['rw_a0'] = nrm((DEPTH, W_C), 0.1)
    inp['rw_a2'] = nrm((DEPTH, LORA_A, W_C), 0.5 * LORA_A ** -0.5)
    inp['rw_g2'] = nrm((DEPTH, LORA_G, W_C), LORA_G ** -0.5)
    inp['rw_kk'] = 0.85 + nrm((DEPTH, W_C), 0.02)
    inp['rw_ka'] = 1.0 + nrm((DEPTH, W_C), 0.02)
    inp['rw_rk'] = nrm((DEPTH, H_C, D_C), 0.1)
    inp['rw_lnx_g'] = 1.0 + nrm((DEPTH, H_C, D_C), 0.02)
    inp['rw_lnx_b'] = nrm((DEPTH, H_C, D_C), 0.02)
    inp['w_branch'] = nrm((DEPTH, N_BRANCH, BRANCH_W, D_MODEL), BRANCH_W ** -0.5)
    inp['w_out'] = nrm((DEPTH, D_MODEL, D_MODEL), BETA * D_MODEL ** -0.5)
    inp['ln1_g'] = 1.0 + nrm((DEPTH, D_MODEL), 0.02)
    inp['ln1_b'] = nrm((DEPTH, D_MODEL), 0.02)
    inp['ln2_g'] = 1.0 + nrm((DEPTH, D_MODEL), 0.02)
    inp['ln2_b'] = nrm((DEPTH, D_MODEL), 0.02)
    inp['ffn_w1'] = nrm((N_DENSE, D_MODEL, D_FF), D_MODEL ** -0.5)
    inp['ffn_w3'] = nrm((N_DENSE, D_MODEL, D_FF), D_MODEL ** -0.5)
    inp['ffn_w2'] = nrm((N_DENSE, D_FF, D_MODEL), BETA * D_FF ** -0.5)
    inp['moe_router'] = nrm((N_MOE, D_MODEL, N_EXPERTS), D_MODEL ** -0.5)
    inp['moe_router_b'] = nrm((N_MOE, N_EXPERTS), 0.01)
    inp['moe_w1'] = nrm((N_MOE, N_EXPERTS, D_MODEL, D_FF_E), D_MODEL ** -0.5)
    inp['moe_w3'] = nrm((N_MOE, N_EXPERTS, D_MODEL, D_FF_E), D_MODEL ** -0.5)
    inp['moe_w2'] = nrm((N_MOE, N_EXPERTS, D_FF_E, D_MODEL), BETA * D_FF_E ** -0.5)
    return inp


def reference(x_prompt, x_sample, c_prompt, c_sample, cache_k, cache_v, cache_kidx, page_table,
              state_ret, state_wkv, state_shift, ada_w, ada_b, w_in, ret_gn_g, rw_mu, rw_w0, rw_w2,
              rw_a0, rw_a2, rw_g2, rw_kk, rw_ka, rw_rk, rw_lnx_g, rw_lnx_b, w_branch, w_out,
              ln1_g, ln1_b, ln2_g, ln2_b, ffn_w1, ffn_w3, ffn_w2, moe_router, moe_router_b,
              moe_w1, moe_w3, moe_w2):
    p = dict(ada_w=ada_w, ada_b=ada_b, w_in=w_in, ret_gn_g=ret_gn_g, rw_mu=rw_mu, rw_w0=rw_w0,
             rw_w2=rw_w2, rw_a0=rw_a0, rw_a2=rw_a2, rw_g2=rw_g2, rw_kk=rw_kk, rw_ka=rw_ka,
             rw_rk=rw_rk, rw_lnx_g=rw_lnx_g, rw_lnx_b=rw_lnx_b, w_branch=w_branch, w_out=w_out,
             ln1_g=ln1_g, ln1_b=ln1_b, ln2_g=ln2_g, ln2_b=ln2_b, ffn_w1=ffn_w1, ffn_w3=ffn_w3,
             ffn_w2=ffn_w2, moe_router=moe_router, moe_router_b=moe_router_b, moe_w1=moe_w1,
             moe_w3=moe_w3, moe_w2=moe_w2)
    xp, xs = x_prompt, x_sample
    bp = xp.shape[0]
    dt = xp.dtype
    st_p, st_s = [], []
    for l in range(DEPTH):
        xp, sp = trunk_layer(xp, c_prompt, l, p, dsa_prompt,
                             jnp.zeros((bp, H_A, DK_A, DV_A), dt),
                             jnp.zeros((bp, H_C, D_C, D_C), dt),
                             jnp.zeros((bp, D_C_IN), dt))
        attend_s = functools.partial(dsa_sample, cache_k=cache_k, cache_v=cache_v,
                                     cache_kidx=cache_kidx, page_table=page_table, layer=l)
        xs, ss = trunk_layer(xs, c_sample, l, p, attend_s, state_ret[l], state_wkv[l], state_shift[l])
        st_p.append(sp)
        st_s.append(ss)
    k_p = jnp.stack([s[0] for s in st_p])
    k_s = jnp.stack([s[0] for s in st_s])
    v_p = jnp.stack([s[1] for s in st_p])
    v_s = jnp.stack([s[1] for s in st_s])
    ki_p = jnp.stack([s[2] for s in st_p])
    ki_s = jnp.stack([s[2] for s in st_s])
    r_p = jnp.stack([s[3] for s in st_p])
    r_s = jnp.stack([s[3] for s in st_s])
    w_p = jnp.stack([s[4] for s in st_p])
    w_s = jnp.stack([s[4] for s in st_s])
    sh_p = jnp.stack([s[5] for s in st_p])
    sh_s = jnp.stack([s[5] for s in st_s])
    return (xp, xs, k_p, k_s, v_p, v_s, ki_p, ki_s, r_p, r_s, w_p, w_s, sh_p, sh_s)
```

```python
import functools
import math

import jax
import jax.numpy as jnp
import numpy as np
from jax import lax
from jax.experimental import pallas as pl
from jax.experimental.pallas import tpu as pltpu

F32 = jnp.float32
BF16 = jnp.bfloat16
I32 = jnp.int32

D_MODEL = 1024
DEPTH = 4
PAGE_SIZE = 128
H_A, DK_A, DV_A = 4, 128, 128
CHUNK_A = 128
H_B, D_B = 8, 64
W_B = H_B * D_B
H_I, D_I = 8, 64
TOPK_MAX = 256
Q_BLOCK = 128
H_C, D_C = 8, 64
W_C = H_C * D_C
LORA_W, LORA_A, LORA_G = 64, 64, 128
RWKV_GN_EPS = 64e-5
N_BRANCH = 3
BRANCH_W = 512
D_FF = 2816
N_EXPERTS = 8
D_FF_E = 1408
ALPHA = (2.0 * DEPTH) ** 0.25
LN_EPS = 1e-5
D_C_IN = 3 * W_C + LORA_W + LORA_A + LORA_G

LANES = 128
SUBLANES = 8
VMEM_LIMIT = 56 * 1024 * 1024
TM_PROJ = 1024
TM_ROW = 512

U_QA, U_KA, U_VA, U_GA = 0, 4, 8, 12
U_QB, U_KB, U_VB, U_IQ = 16, 20, 24, 28
U_MISC = 32
U_PC = 34
U_GL = 48
N_UNITS = 72
PC_BLOCK_W = 2 * LANES
PC_BLOCKS = 7
D_PACK = N_UNITS * LANES
IW_OFF = 64

NEG_INF = float("-inf")


def _cp(sem, vmem=VMEM_LIMIT):
    return pltpu.CompilerParams(dimension_semantics=sem, vmem_limit_bytes=vmem)


def _dot(a, b):
    return jnp.dot(a, b, preferred_element_type=F32)


def _dot_nt(a, b):
    return lax.dot_general(a, b, (((1,), (1,)), ((), ())), preferred_element_type=F32)


def _dot_tn(a, b):
    return lax.dot_general(a, b, (((0,), (0,)), ((), ())), preferred_element_type=F32)


def _sigmoid(x):
    return 1.0 / (1.0 + jnp.exp(-x))


def _silu(x):
    return x * _sigmoid(x)


def _iota(shape, dim):
    return lax.broadcasted_iota(I32, shape, dim)


def _rows_to_tile(rows, width, fill=0.0):
    ri = _iota((SUBLANES, width), 0)
    out = jnp.full((SUBLANES, width), fill, F32)
    for t, r in enumerate(rows):
        out = jnp.where(ri == t, jnp.broadcast_to(r, (SUBLANES, width)), out)
    return out


def _ada_body(c_ref, w_ref, b_ref, o_ref):
    c = c_ref[...]
    o_ref[0] = _dot(_silu(c).astype(BF16), w_ref[0]) + b_ref[0]


def _ada_all(c_all, ada_w_bf, ada_b):
    m = c_all.shape[0]
    n = ada_w_bf.shape[-1]
    tn = 1024
    return pl.pallas_call(
        _ada_body,
        grid=(DEPTH, n // tn),
        in_specs=[pl.BlockSpec((m, D_MODEL), lambda l, j: (0, 0)),
                  pl.BlockSpec((1, D_MODEL, tn), lambda l, j: (l, 0, j)),
                  pl.BlockSpec((1, 1, tn), lambda l, j: (l, 0, j))],
        out_specs=pl.BlockSpec((1, m, tn), lambda l, j: (l, 0, j)),
        out_shape=jax.ShapeDtypeStruct((DEPTH, m, n), F32),
        compiler_params=_cp(("arbitrary", "arbitrary")),
        name="ada",
    )(c_all, ada_w_bf, ada_b.reshape(DEPTH, 1, n))


def _mod_spec(mod, tm, tiles_per_group):
    g, r, d = mod.shape
    if r == 1:
        return pl.BlockSpec((1, 1, d), lambda i, *_: (i // tiles_per_group, 0, 0))
    assert r == tm and g == 1
    return pl.BlockSpec((1, r, d), lambda i, *_: (0, 0, 0))


def _modulate_body(x_ref, sc_ref, sh_ref, h_ref):
    h_ref[...] = (x_ref[...] * (1.0 + sc_ref[0]) + sh_ref[0]).astype(BF16)


def _modulate(x, sc, sh, tm, tiles_per_group):
    n = x.shape[0]
    return pl.pallas_call(
        _modulate_body,
        grid=(n // tm,),
        in_specs=[pl.BlockSpec((tm, D_MODEL), lambda i: (i, 0)),
                  _mod_spec(sc, tm, tiles_per_group), _mod_spec(sh, tm, tiles_per_group)],
        out_specs=pl.BlockSpec((tm, D_MODEL), lambda i: (i, 0)),
        out_shape=jax.ShapeDtypeStruct((n, D_MODEL), BF16),
        compiler_params=_cp(("arbitrary",)),
        name="modulate",
    )(x, sc, sh)


def _mm_body(x_ref, w_ref, o_ref):
    o_ref[...] = _dot(x_ref[...], w_ref[...])


def _in_proj(h, w_pack, tm):
    n = h.shape[0]
    tn = 1024
    return pl.pallas_call(
        _mm_body,
        grid=(n // tm, D_PACK // tn),
        in_specs=[pl.BlockSpec((tm, D_MODEL), lambda i, j: (i, 0)),
                  pl.BlockSpec((D_MODEL, tn), lambda i, j: (0, j))],
        out_specs=pl.BlockSpec((tm, tn), lambda i, j: (i, j)),
        out_shape=jax.ShapeDtypeStruct((n, D_PACK), F32),
        compiler_params=_cp(("arbitrary", "arbitrary")),
        name="in_proj",
    )(h, w_pack)


def _retention_body(q_ref, k_ref, v_ref, g_ref, s0_ref, lg_ref, gn_ref, o_ref, sout_ref,
                    s_scr, pad_scr, *, rb, nchunk):
    n = pl.program_id(1)
    c = CHUNK_A

    @pl.when(n == 0)
    def _():
        s_scr[...] = s0_ref[0]

    def padded(ref, slot):
        if rb == c:
            return ref[0]
        pad_scr[slot] = jnp.zeros((c, H_A * DK_A), F32)
        pad_scr[slot, 0:rb, :] = ref[0]
        return pad_scr[slot]

    q_all = padded(q_ref, 0)
    k_all = padded(k_ref, 1)
    v_all = padded(v_ref, 2)
    g_all = padded(g_ref, 3)

    row = _iota((c, c), 0).astype(F32)
    col = _iota((c, c), 1).astype(F32)
    rel = row - col
    outs = []
    for h in range(H_A):
        sl = slice(h * DK_A, (h + 1) * DK_A)
        lg = lg_ref[0:1, sl]
        q = (q_all[:, sl] * (DK_A ** -0.5)).astype(BF16)
        k = k_all[:, sl]
        v = v_all[:, sl].astype(BF16)
        s0 = s_scr[h]
        decay = jnp.where(rel >= 0.0, jnp.exp(lg * jnp.maximum(rel, 0.0)), 0.0)
        scores = _dot_nt(q, k.astype(BF16)) * decay
        inner = _dot(scores.astype(BF16), v)
        xi = jnp.exp(lg * (row + 1.0))
        cross = _dot(q, s0.astype(BF16)) * xi
        zeta = jnp.where(row < float(rb), jnp.exp(lg * (float(rb) - 1.0 - row)), 0.0)
        s_new = jnp.exp(lg * float(rb)) * s0 + _dot_tn((k * zeta).astype(BF16), v)
        s_scr[h] = s_new
        o = inner + cross
        mu = jnp.mean(o, axis=-1, keepdims=True)
        var = jnp.mean(jnp.square(o - mu), axis=-1, keepdims=True)
        o = (o - mu) * lax.rsqrt(var + LN_EPS) * gn_ref[0:1, sl]
        outs.append(_silu(g_all[:, sl]) * o)
    o_full = jnp.concatenate(outs, axis=-1)
    o_ref[0] = o_full[0:rb].astype(BF16)

    @pl.when(n == nchunk - 1)
    def _():
        sout_ref[0] = s_scr[...]


def _retention(p3, s0, lgam, gn_g, nb, nchunk, rb):
    w = H_A * DK_A
    col = lambda u: (lambda b, n: (b * nchunk + n, 0, u // 4))
    body = functools.partial(_retention_body, rb=rb, nchunk=nchunk)
    return pl.pallas_call(
        body,
        grid=(nb, nchunk),
        in_specs=[pl.BlockSpec((1, rb, w), col(U_QA)), pl.BlockSpec((1, rb, w), col(U_KA)),
                  pl.BlockSpec((1, rb, w), col(U_VA)), pl.BlockSpec((1, rb, w), col(U_GA)),
                  pl.BlockSpec((1, H_A, DK_A, DV_A), lambda b, n: (b, 0, 0, 0)),
                  pl.BlockSpec((1, w), lambda b, n: (0, 0)),
                  pl.BlockSpec((1, w), lambda b, n: (0, 0))],
        out_specs=[pl.BlockSpec((1, rb, w), lambda b, n: (b * nchunk + n, 0, 0)),
                   pl.BlockSpec((1, H_A, DK_A, DV_A), lambda b, n: (b, 0, 0, 0))],
        out_shape=[jax.ShapeDtypeStruct((nb * nchunk, rb, w), BF16),
                   jax.ShapeDtypeStruct((nb, H_A, DK_A, DV_A), F32)],
        scratch_shapes=[pltpu.VMEM((H_A, DK_A, DV_A), F32), pltpu.VMEM((4, CHUNK_A, w), F32)],
        compiler_params=_cp(("arbitrary", "arbitrary")),
        name="retention",
    )(p3, p3, p3, p3, s0, lgam, gn_g)


def _score_key(s):
    bits = pltpu.bitcast(s, I32)
    return jnp.where(bits < 0, bits ^ jnp.int32(0x7FFFFFFF), bits)


def _tri_incl():
    return jnp.where(_iota((LANES, LANES), 0) <= _iota((LANES, LANES), 1), 1.0, 0.0).astype(BF16)


KC = 512


def _dsa_prompt_body(q_ref, iq_ref, mq_ref, k_ref, v_ref, mk_ref, o_ref,
                     kbf, vbf, ikbf, keys, bias, logit, *, t_len, top):
    i = pl.program_id(1)
    nkc = t_len // KC
    qb = Q_BLOCK

    @pl.when(i == 0)
    def _():
        for j in range(nkc):
            kbf[j] = k_ref[j * KC:(j + 1) * KC, :].astype(BF16)
            vbf[j] = v_ref[j * KC:(j + 1) * KC, :].astype(BF16)
            ikbf[j] = mk_ref[j * KC:(j + 1) * KC, :].astype(BF16)

    nsc = (i * qb) // KC + 1
    qpos = i * qb + _iota((qb, KC), 0)
    lane = _iota((qb, KC), 1)

    iq = (iq_ref[...] * (D_I ** -0.5)).astype(BF16)
    mq = mq_ref[...]
    iw = [jnp.broadcast_to(mq[:, IW_OFF + h:IW_OFF + h + 1] * (H_I ** -0.5), (qb, LANES))
          for h in range(H_I)]

    def score_chunk(j, carry):
        ik = ikbf[j][:, 0:D_I]
        acc = jnp.zeros((qb, KC), F32)
        for h in range(H_I):
            s = _dot_nt(iq[:, h * D_I:(h + 1) * D_I], ik)
            wh = jnp.concatenate([iw[h]] * (KC // LANES), axis=1)
            acc = acc + jnp.maximum(s, 0.0) * wh
        kpos = j * KC + lane
        acc = jnp.where(kpos <= qpos, acc + 0.0, NEG_INF)
        keys[j] = _score_key(acc)
        return carry

    lax.fori_loop(0, nsc, score_chunk, 0)

    def count_ge(cand):
        def body(j, acc):
            m = jnp.where(keys[j] >= cand, 1.0, 0.0)
            for t in range(KC // LANES):
                acc = acc + m[:, t * LANES:(t + 1) * LANES]
            return acc
        acc = lax.fori_loop(0, nsc, body, jnp.zeros((qb, LANES), F32))
        return jnp.sum(acc, axis=1, keepdims=True)

    prefix = jnp.full((qb, 1), -2 ** 31, I32)
    for bit in range(31, -1, -1):
        step = jnp.int32(-2 ** 31) if bit == 31 else jnp.int32(1 << bit)
        cand = prefix + step
        cnt = count_ge(cand)
        prefix = jnp.where(cnt >= float(top), cand, prefix)
    tau = prefix

    def count_gt(j, acc):
        m = jnp.where(keys[j] > tau, 1.0, 0.0)
        for t in range(KC // LANES):
            acc = acc + m[:, t * LANES:(t + 1) * LANES]
        return acc
    n_gt = jnp.sum(lax.fori_loop(0, nsc, count_gt, jnp.zeros((qb, LANES), F32)), axis=1, keepdims=True)
    need = float(top) - n_gt

    tri = _tri_incl()

    def bias_chunk(j, carry):
        kj = keys[j]
        gt = kj > tau
        eq = jnp.where(kj == tau, 1.0, 0.0)
        kpos = j * KC + lane
        parts = []
        for t in range(KC // LANES):
            e = eq[:, t * LANES:(t + 1) * LANES]
            pre = _dot(e.astype(BF16), tri) + carry
            parts.append(jnp.where((e > 0.0) & (pre <= need), 1.0, 0.0))
            carry = carry + jnp.sum(e, axis=1, keepdims=True)
        sel = jnp.concatenate(parts, axis=1)
        keep = (gt | (sel > 0.0)) & (kpos <= qpos)
        bias[j] = jnp.where(keep, 0.0, NEG_INF)
        return carry

    lax.fori_loop(0, nsc, bias_chunk, jnp.zeros((qb, 1), F32))

    q_all = q_ref[...]
    lane_q = _iota((qb, LANES), 1)
    outs = []
    for hp in range(H_B // 2):
        pair = []
        for h2 in range(2):
            h = 2 * hp + h2
            slope = 2.0 ** (-8.0 * (h + 1) / H_B)
            in_half = (lane_q >= h2 * D_B) & (lane_q < (h2 + 1) * D_B)
            qh = jnp.where(in_half, q_all[:, hp * LANES:(hp + 1) * LANES], 0.0).astype(BF16)

            def logit_chunk(j, m, qh=qh, slope=slope, hp=hp):
                kk = kbf[j][:, hp * LANES:(hp + 1) * LANES]
                dist = (qpos - (j * KC + lane)).astype(F32)
                s = _dot_nt(qh, kk) * (D_B ** -0.5) - slope * dist + bias[j]
                logit[j] = s
                return jnp.maximum(m, jnp.max(s, axis=1, keepdims=True))

            m = lax.fori_loop(0, nsc, logit_chunk, jnp.full((qb, 1), NEG_INF, F32))

            def pv_chunk(j, carry, m=m, hp=hp):
                l, acc = carry
                p = jnp.exp(logit[j] - m)
                l = l + jnp.sum(p, axis=1, keepdims=True)
                acc = acc + _dot(p.astype(BF16), vbf[j][:, hp * LANES:(hp + 1) * LANES])
                return l, acc

            l, acc = lax.fori_loop(0, nsc, pv_chunk,
                                   (jnp.zeros((qb, 1), F32), jnp.zeros((qb, LANES), F32)))
            pair.append(acc / l)
        outs.append(jnp.where(lane_q < D_B, pair[0], pair[1]))
    o_ref[...] = jnp.concatenate(outs, axis=1).astype(BF16)


def _dsa_prompt(p2, nb, t_len):
    top = min(TOPK_MAX, t_len // 4)
    nq = t_len // Q_BLOCK
    nkc = t_len // KC
    body = functools.partial(_dsa_prompt_body, t_len=t_len, top=top)
    qspec = lambda u, wdt: pl.BlockSpec((Q_BLOCK, wdt), lambda b, i: (b * nq + i, u * LANES // wdt))
    aspec = lambda u, wdt: pl.BlockSpec((t_len, wdt), lambda b, i: (b, u * LANES // wdt))
    return pl.pallas_call(
        body,
        grid=(nb, nq),
        in_specs=[qspec(U_QB, W_B), qspec(U_IQ, W_B), qspec(U_MISC, LANES),
                  aspec(U_KB, W_B), aspec(U_VB, W_B), aspec(U_MISC, LANES)],
        out_specs=pl.BlockSpec((Q_BLOCK, W_B), lambda b, i: (b * nq + i, 0)),
        out_shape=jax.ShapeDtypeStruct((nb * t_len, W_B), BF16),
        scratch_shapes=[pltpu.VMEM((nkc, KC, W_B), BF16), pltpu.VMEM((nkc, KC, W_B), BF16),
                        pltpu.VMEM((nkc, KC, LANES), BF16),
                        pltpu.VMEM((nkc, Q_BLOCK, KC), I32), pltpu.VMEM((nkc, Q_BLOCK, KC), F32),
                        pltpu.VMEM((nkc, Q_BLOCK, KC), F32)],
        compiler_params=_cp(("arbitrary", "arbitrary")),
        name="dsa_prompt",
    )(p2, p2, p2, p2, p2, p2)


PG_STEP = 16


def _dsa_s_score_body(pt_ref, iq_ref, iwb_ref, *rest, t_len):
    pages = rest[:PG_STEP]
    o_ref = rest[PG_STEP]
    iq = (iq_ref[0] * (D_I ** -0.5)).astype(BF16)
    iwb = iwb_ref[0] * (H_I ** -0.5)
    row = _iota((SUBLANES, LANES), 0)
    for j in range(PG_STEP):
        ik = pages[j][0, 0].astype(BF16)
        s = jnp.maximum(_dot_nt(iq, ik), 0.0) * iwb
        per_t = [jnp.sum(s[t * H_I:(t + 1) * H_I], axis=0, keepdims=True) for t in range(t_len)]
        sc = _rows_to_tile(per_t, LANES)
        o_ref[0, j] = jnp.where(row < t_len, sc + 0.0, NEG_INF)


def _dsa_sample_scores(layer, page_table, cache_kidx, iq32, iwb, t_len):
    nb, n_pages = page_table.shape
    nsteps = n_pages // PG_STEP
    rows = t_len * H_I

    def page_spec(j):
        return pl.BlockSpec((1, 1, PAGE_SIZE, D_I),
                            lambda b, s, pt: (layer, pt[b, s * PG_STEP + j], 0, 0))

    grid_spec = pltpu.PrefetchScalarGridSpec(
        num_scalar_prefetch=1,
        grid=(nb, nsteps),
        in_specs=[pl.BlockSpec((1, rows, D_I), lambda b, s, pt: (b, 0, 0)),
                  pl.BlockSpec((1, rows, LANES), lambda b, s, pt: (b, 0, 0))]
                 + [page_spec(j) for j in range(PG_STEP)],
        out_specs=pl.BlockSpec((1, PG_STEP, SUBLANES, LANES), lambda b, s, pt: (b, s, 0, 0)),
    )
    return pl.pallas_call(
        functools.partial(_dsa_s_score_body, t_len=t_len),
        grid_spec=grid_spec,
        out_shape=jax.ShapeDtypeStruct((nb, n_pages, SUBLANES, LANES), F32),
        compiler_params=_cp(("arbitrary", "arbitrary")),
        name="dsa_sample_scores",
    )(page_table, iq32, iwb, *([cache_kidx] * PG_STEP))


def _dsa_s_attn_body(pt_ref, sc_ref, q_ref, iq_ref, iwb_ref, knew_ref, vnew_ref, iknew_ref, *rest,
                     t_len, n_pages, top):
    kpages = rest[:PG_STEP]
    vpages = rest[PG_STEP:2 * PG_STEP]
    o_ref = rest[2 * PG_STEP]
    bias, m_scr, l_scr, acc_scr = rest[2 * PG_STEP + 1:]
    s_idx = pl.program_id(1)
    nsteps = n_pages // PG_STEP
    past = n_pages * PAGE_SIZE
    rows = t_len * H_B
    row8 = _iota((SUBLANES, LANES), 0)
    lane8 = _iota((SUBLANES, LANES), 1)

    @pl.when(s_idx == 0)
    def _():
        iq = (iq_ref[0] * (D_I ** -0.5)).astype(BF16)
        iwb = iwb_ref[0] * (H_I ** -0.5)
        s = jnp.maximum(_dot_nt(iq, iknew_ref[0].astype(BF16)), 0.0) * iwb
        per_t = [jnp.sum(s[t * H_I:(t + 1) * H_I], axis=0, keepdims=True) for t in range(t_len)]
        sc_new = _rows_to_tile(per_t, LANES)
        sc_new = jnp.where((row8 < t_len) & (lane8 <= row8), sc_new + 0.0, NEG_INF)

        keys_past = _score_key(sc_ref[0])
        keys_new = _score_key(sc_new)

        def count(pred_past, pred_new):
            c = jnp.sum(jnp.where(pred_past, 1.0, 0.0), axis=0) + jnp.where(pred_new, 1.0, 0.0)
            return jnp.sum(c, axis=1, keepdims=True)

        prefix = jnp.full((SUBLANES, 1), -2 ** 31, I32)
        for bit in range(31, -1, -1):
            step = jnp.int32(-2 ** 31) if bit == 31 else jnp.int32(1 << bit)
            cand = prefix + step
            cnt = count(keys_past >= cand[None], keys_new >= cand)
            prefix = jnp.where(cnt >= float(top), cand, prefix)
        tau = prefix
        n_gt = count(keys_past > tau[None], keys_new > tau)
        need = float(top) - n_gt

        tri = _tri_incl()
        eq_past = jnp.where(keys_past == tau[None], 1.0, 0.0)
        pre_in = _dot(eq_past.reshape(n_pages * SUBLANES, LANES).astype(BF16), tri)
        pre_in = pre_in.reshape(n_pages, SUBLANES, LANES)
        carry = jnp.zeros((SUBLANES, 1), F32)
        for p in range(n_pages):
            e = eq_past[p]
            keep = (keys_past[p] > tau) | ((e > 0.0) & (pre_in[p] + carry <= need))
            keep = keep & (row8 < t_len)
            bias[p] = jnp.where(keep, 0.0, NEG_INF)
            carry = carry + jnp.sum(e, axis=1, keepdims=True)
        e = jnp.where(keys_new == tau, 1.0, 0.0)
        pre = _dot(e.astype(BF16), tri) + carry
        keep = (keys_new > tau) | ((e > 0.0) & (pre <= need))
        keep = keep & (row8 < t_len) & (lane8 <= row8)
        bias[n_pages] = jnp.where(keep, 0.0, NEG_INF)

        m_scr[...] = jnp.full((rows, LANES), NEG_INF, F32)
        l_scr[...] = jnp.zeros((rows, LANES), F32)
        acc_scr[...] = jnp.zeros((rows, W_B), F32)

    q = q_ref[0]
    col = _iota((SUBLANES, W_B), 1)
    hrow = _iota((SUBLANES, W_B), 0)
    blockmask = (col // D_B) == hrow
    qexp = jnp.concatenate(
        [jnp.where(blockmask, jnp.broadcast_to(q[t:t + 1], (SUBLANES, W_B)), 0.0) for t in range(t_len)],
        axis=0).astype(BF16)
    hvec = _iota((rows, LANES), 0) % H_B
    slope = jnp.exp2(-8.0 * (hvec.astype(F32) + 1.0) / H_B)
    tvec = _iota((rows, LANES), 0) // H_B
    qpos = past + tvec
    lane = _iota((rows, LANES), 1)

    def expand_bias(bp):
        return jnp.concatenate([jnp.broadcast_to(bp[t:t + 1], (SUBLANES, LANES)) for t in range(t_len)],
                               axis=0)

    def attend(k_list, v_list, bias_list, kpos0_list):
        logits = []
        for kp, bp, kpos0 in zip(k_list, bias_list, kpos0_list):
            s = _dot_nt(qexp, kp.astype(BF16)) * (D_B ** -0.5)
            dist = (qpos - (kpos0 + lane)).astype(F32)
            logits.append(s - slope * dist + expand_bias(bp))
        m_old = m_scr[...]
        m_new = m_old
        for s in logits:
            m_new = jnp.maximum(m_new, jnp.max(s, axis=1, keepdims=True))
        m_safe = jnp.where(m_new == NEG_INF, 0.0, m_new)
        scale = jnp.exp(m_old - m_safe)
        l = l_scr[...] * scale
        acc = acc_scr[...] * scale[:, 0:1]
        for s, vp in zip(logits, v_list):
            p = jnp.exp(s - m_safe)
            l = l + jnp.sum(p, axis=1, keepdims=True)
            acc = acc + _dot(p.astype(BF16), vp.astype(BF16))
        m_scr[...] = m_new
        l_scr[...] = l
        acc_scr[...] = acc

    @pl.when(s_idx < nsteps)
    def _():
        base = s_idx * PG_STEP
        attend([kpages[j][0, 0] for j in range(PG_STEP)],
               [vpages[j][0, 0] for j in range(PG_STEP)],
               [bias[base + j] for j in range(PG_STEP)],
               [(base + j) * PAGE_SIZE for j in range(PG_STEP)])

    @pl.when(s_idx == nsteps)
    def _():
        attend([knew_ref[0]], [vnew_ref[0]], [bias[n_pages]], [past])
        o = acc_scr[...] / l_scr[...][:, 0:1]
        outs = [jnp.sum(jnp.where(blockmask, o[t * H_B:(t + 1) * H_B], 0.0), axis=0, keepdims=True)
                for t in range(t_len)]
        o_ref[0] = _rows_to_tile(outs, W_B).astype(BF16)


def _dsa_sample_attend(layer, page_table, scores, q3, iq32, iwb, knew, vnew, iknew, cache_k4, cache_v4,
                       t_len):
    nb, n_pages = page_table.shape
    nsteps = n_pages // PG_STEP
    top = min(TOPK_MAX, (n_pages * PAGE_SIZE + t_len) // 4)
    rows = t_len * H_B

    def page_spec(j):
        def imap(b, s, pt):
            return (layer, pt[b, jnp.minimum(s, nsteps - 1) * PG_STEP + j], 0, 0)
        return pl.BlockSpec((1, 1, PAGE_SIZE, W_B), imap)

    per_b = lambda shape: pl.BlockSpec((1,) + shape, lambda b, s, pt: (b,) + (0,) * len(shape))
    grid_spec = pltpu.PrefetchScalarGridSpec(
        num_scalar_prefetch=1,
        grid=(nb, nsteps + 1),
        in_specs=[per_b((n_pages, SUBLANES, LANES)), per_b((t_len, W_B)), per_b((rows, D_I)),
                  per_b((rows, LANES)), per_b((PAGE_SIZE, W_B)), per_b((PAGE_SIZE, W_B)),
                  per_b((PAGE_SIZE, D_I))]
                 + [page_spec(j) for j in range(PG_STEP)] + [page_spec(j) for j in range(PG_STEP)],
        out_specs=per_b((SUBLANES, W_B)),
        scratch_shapes=[pltpu.VMEM((n_pages + 1, SUBLANES, LANES), F32),
                        pltpu.VMEM((rows, LANES), F32), pltpu.VMEM((rows, LANES), F32),
                        pltpu.VMEM((rows, W_B), F32)],
    )
    body = functools.partial(_dsa_s_attn_body, t_len=t_len, n_pages=n_pages, top=top)
    return pl.pallas_call(
        body,
        grid_spec=grid_spec,
        out_shape=jax.ShapeDtypeStruct((nb, SUBLANES, W_B), BF16),
        compiler_params=_cp(("arbitrary", "arbitrary")),
        name="dsa_sample_attend",
    )(page_table, scores, q3, iq32, iwb, knew, vnew, iknew,
      *([cache_k4] * PG_STEP), *([cache_v4] * PG_STEP))


def _head_block_ones(n):
    return jnp.where((_iota((n, n), 0) // D_C) == (_iota((n, n), 1) // D_C), 1.0, 0.0).astype(BF16)


def _rwkv_prep_body(*refs, tm, seq_len):
    pc_refs = refs[:PC_BLOCKS]
    (sh_ref, mu_ref, w0_ref, a0_ref, kk_ref, ka_ref, rk_ref, wwa_ref, g2_ref,
     r_ref, w_ref, k_ref, v_ref, na_ref, b_ref, g_ref, bonus_ref, carry) = refs[PC_BLOCKS:]
    pc = jnp.concatenate([r[...] for r in pc_refs], axis=1)
    rolled = pltpu.roll(pc, 1, 0)
    row = _iota(pc.shape, 0)
    if seq_len >= tm:
        tiles_per_seq = seq_len // tm
        first = (pl.program_id(0) % tiles_per_seq) == 0
        head_row = jnp.where(first, sh_ref[0], carry[...])
        prev = jnp.where(row == 0, head_row, rolled)
        carry[...] = pc[tm - 1:tm, :]
    else:
        prev = jnp.where((row % seq_len) == 0, sh_ref[0], rolled)
    pm = pc + (prev - pc) * mu_ref[...]
    r = pm[:, 0:W_C]
    k = pm[:, W_C:2 * W_C]
    v = pm[:, 2 * W_C:3 * W_C]
    lwa = pm[:, 3 * W_C:3 * W_C + LANES]
    lg = pm[:, 3 * W_C + LANES:3 * W_C + 2 * LANES]
    lane = _iota(lwa.shape, 1)
    lwa = jnp.where(lane < LORA_W, jnp.tanh(lwa), lwa)
    wa = _dot(lwa.astype(BF16), wwa_ref[...])
    x = -(w0_ref[...] + wa[:, 0:W_C])
    softplus = jnp.maximum(x, 0.0) + jnp.log1p(jnp.exp(-jnp.abs(x)))
    w_raw = -softplus - 0.5
    a = _sigmoid(a0_ref[...] + wa[:, W_C:2 * W_C])
    g = _dot(_sigmoid(lg).astype(BF16), g2_ref[...])
    ones = _head_block_ones(W_C)
    kk = k * kk_ref[...]
    ss = _dot((kk * kk).astype(BF16), ones)
    kk = kk / jnp.maximum(jnp.sqrt(ss), 1e-12)
    k2 = k * (1.0 + (a - 1.0) * ka_ref[...])
    r_ref[...] = r
    w_ref[...] = jnp.exp(-jnp.exp(w_raw))
    k_ref[...] = k2
    v_ref[...] = v
    na_ref[...] = -kk
    b_ref[...] = kk * a
    g_ref[...] = g
    bonus_ref[...] = _dot((r * k2 * rk_ref[...]).astype(BF16), ones) * v


def _rwkv_prep(p2, shift_rows, prm, tm, seq_len):
    n = p2.shape[0]
    out_rm = pl.BlockSpec((tm, W_C), lambda i: (i, 0))
    shape_rm = jax.ShapeDtypeStruct((n, W_C), F32)
    if seq_len >= tm:
        tiles_per_seq = seq_len // tm
        sh_spec = pl.BlockSpec((1, 1, D_C_IN), lambda i: (i // tiles_per_seq, 0, 0))
        out_scan = pl.BlockSpec((tm, W_C), lambda i: (i % tiles_per_seq, i // tiles_per_seq))
        shape_scan = jax.ShapeDtypeStruct((seq_len, (n // seq_len) * W_C), F32)
    else:
        sh_spec = pl.BlockSpec((1, tm, D_C_IN), lambda i: (i, 0, 0))
        out_scan, shape_scan = out_rm, shape_rm
    row = lambda w: pl.BlockSpec((1, w), lambda i: (0, 0))
    body = functools.partial(_rwkv_prep_body, tm=tm, seq_len=seq_len)
    pc_block0 = U_PC * LANES // PC_BLOCK_W

    def pc_spec(j):
        return pl.BlockSpec((tm, PC_BLOCK_W), lambda i: (i, pc_block0 + j))

    return pl.pallas_call(
        body,
        grid=(n // tm,),
        in_specs=[pc_spec(j) for j in range(PC_BLOCKS)]
                 + [sh_spec, row(D_C_IN), row(W_C), row(W_C), row(W_C), row(W_C), row(W_C),
                    pl.BlockSpec((LANES, 2 * W_C), lambda i: (0, 0)),
                    pl.BlockSpec((LORA_G, W_C), lambda i: (0, 0))],
        out_specs=[out_scan] * 6 + [out_rm] * 2,
        out_shape=[shape_scan] * 6 + [shape_rm] * 2,
        scratch_shapes=[pltpu.VMEM((1, D_C_IN), F32)],
        compiler_params=_cp(("arbitrary",)),
        name="rwkv_prep",
    )(*([p2] * PC_BLOCKS), shift_rows, prm["mu"], prm["w0"], prm["a0"], prm["kk"], prm["ka"], prm["rk"],
      prm["wwa"], prm["g2"])


SCAN_B = 8
SCAN_TB = 128


def _rwkv_scan_body(r_ref, w_ref, k_ref, v_ref, na_ref, b_ref, s0_ref, y_ref, sout_ref, s_scr,
                    *, tb, ntb):
    tblk = pl.program_id(1)

    @pl.when(tblk == 0)
    def _():
        s_scr[...] = s0_ref[...]

    ones = _head_block_ones(LANES)
    vi = _iota((D_C, LANES), 0)
    li = _iota((D_C, LANES), 1)
    irep = jnp.where((li % D_C) == vi, 1.0, 0.0)
    npair = H_C // 2

    def step(t, carry):
        r_t, w_t, k_t, v_t, na_t, b_t = (ref[t] for ref in (r_ref, w_ref, k_ref, v_ref, na_ref, b_ref))
        for b in range(SCAN_B):
            for hp in range(npair):
                g = b * npair + hp
                sl = slice(hp * LANES, (hp + 1) * LANES)
                row = lambda x: x[b:b + 1, sl]
                s = s_scr[g]
                sa = _dot((s * row(na_t)).astype(BF16), ones)
                vb = _dot((irep * row(v_t)).astype(BF16), ones)
                s = s * row(w_t) + sa * row(b_t) + vb * row(k_t)
                s_scr[g] = s
                yb = _dot((s * row(r_t)).astype(BF16), ones)
                y_ref[t, b:b + 1, sl] = jnp.sum(yb * irep, axis=0, keepdims=True)
        return carry

    lax.fori_loop(0, tb, step, 0)

    @pl.when(tblk == ntb - 1)
    def _():
        sout_ref[...] = s_scr[...]


def _rwkv_scan(r, w, k, v, na, b, s0, nb, t_len, tb):
    ntb = t_len // tb
    npair = H_C // 2
    blk = pl.BlockSpec((tb, SCAN_B, W_C), lambda c, t: (t, c, 0))
    sblk = pl.BlockSpec((SCAN_B * npair, D_C, LANES), lambda c, t: (c, 0, 0))
    body = functools.partial(_rwkv_scan_body, tb=tb, ntb=ntb)
    return pl.pallas_call(
        body,
        grid=(nb // SCAN_B, ntb),
        in_specs=[blk] * 6 + [sblk],
        out_specs=[blk, sblk],
        out_shape=[jax.ShapeDtypeStruct((t_len, nb, W_C), F32),
                   jax.ShapeDtypeStruct((nb * npair, D_C, LANES), F32)],
        scratch_shapes=[pltpu.VMEM((SCAN_B * npair, D_C, LANES), F32)],
        compiler_params=_cp(("arbitrary", "arbitrary")),
        name="rwkv_scan",
    )(r, w, k, v, na, b, s0)


def _rwkv_post_body(y_ref, g_ref, bonus_ref, lg_ref, lb_ref, o_ref):
    y = y_ref[...]
    ones = _head_block_ones(W_C)
    mu = _dot(y.astype(BF16), ones) * (1.0 / D_C)
    d = y - mu
    var = _dot((d * d).astype(BF16), ones) * (1.0 / D_C)
    yn = d * lax.rsqrt(var + RWKV_GN_EPS) * lg_ref[...] + lb_ref[...]
    o_ref[...] = ((yn + bonus_ref[...]) * g_ref[...]).astype(BF16)


def _rwkv_post(y, g, bonus, lnx_g, lnx_b, tm, seq_len):
    n = g.shape[0]
    blk = pl.BlockSpec((tm, W_C), lambda i: (i, 0))
    row = pl.BlockSpec((1, W_C), lambda i: (0, 0))
    if seq_len >= tm:
        tiles_per_seq = seq_len // tm
        yblk = pl.BlockSpec((tm, W_C), lambda i: (i % tiles_per_seq, i // tiles_per_seq))
    else:
        yblk = blk
    return pl.pallas_call(
        _rwkv_post_body,
        grid=(n // tm,),
        in_specs=[yblk, blk, blk, row, row],
        out_specs=blk,
        out_shape=jax.ShapeDtypeStruct((n, W_C), BF16),
        compiler_params=_cp(("arbitrary",)),
        name="rwkv_post",
    )(y, g, bonus, lnx_g, lnx_b)


def _layer_norm(x, g, b):
    mu = jnp.mean(x, axis=-1, keepdims=True)
    var = jnp.mean(jnp.square(x - mu), axis=-1, keepdims=True)
    return (x - mu) * lax.rsqrt(var + LN_EPS) * g + b


def _merge_body(oa_ref, ob_ref, oc_ref, gla_ref, glb_ref, glc_ref, x_ref, wb_ref, wo_ref,
                gt_ref, sc_ref, sh_ref, lg_ref, lb_ref, x1_ref, h_ref):
    mixed = None
    for gidx, (o_ref, gl_ref) in enumerate(((oa_ref, gla_ref), (ob_ref, glb_ref), (oc_ref, glc_ref))):
        br = _dot(o_ref[...], wb_ref[gidx])
        term = _sigmoid(gl_ref[...]) * br
        mixed = term if mixed is None else mixed + term
    y = _dot(mixed.astype(BF16), wo_ref[...])
    x1 = _layer_norm(ALPHA * x_ref[...] + gt_ref[0] * y, lg_ref[...], lb_ref[...])
    x1_ref[...] = x1
    h_ref[...] = (x1 * (1.0 + sc_ref[0]) + sh_ref[0]).astype(BF16)


def _merge(oa, ob, oc, p2, x, wb, wo, gt, sc, sh, ln_g, ln_b, tm, tiles_per_group):
    n = x.shape[0]
    blk = lambda w: pl.BlockSpec((tm, w), lambda i: (i, 0))
    row = pl.BlockSpec((1, D_MODEL), lambda i: (0, 0))
    ms = lambda m: _mod_spec(m, tm, tiles_per_group)
    gl_block0 = U_GL * LANES // D_MODEL
    return pl.pallas_call(
        _merge_body,
        grid=(n // tm,),
        in_specs=[blk(BRANCH_W), blk(BRANCH_W), blk(BRANCH_W)]
                 + [pl.BlockSpec((tm, D_MODEL), lambda i, g=g: (i, gl_block0 + g)) for g in range(N_BRANCH)]
                 + [blk(D_MODEL),
                  pl.BlockSpec((N_BRANCH, BRANCH_W, D_MODEL), lambda i: (0, 0, 0)),
                  pl.BlockSpec((D_MODEL, D_MODEL), lambda i: (0, 0)),
                  ms(gt), ms(sc), ms(sh), row, row],
        out_specs=[blk(D_MODEL), blk(D_MODEL)],
        out_shape=[jax.ShapeDtypeStruct((n, D_MODEL), F32), jax.ShapeDtypeStruct((n, D_MODEL), BF16)],
        compiler_params=_cp(("arbitrary",)),
        name="merge",
    )(oa, ob, oc, p2, p2, p2, x, wb, wo, gt, sc, sh, ln_g, ln_b)


def _ffn_finish(acc, x_ref, gt_ref, sc_ref, sh_ref, lg_ref, lb_ref, x2_ref, h_ref):
    x2 = _layer_norm(ALPHA * x_ref[...] + gt_ref[0] * acc, lg_ref[...], lb_ref[...])
    x2_ref[...] = x2
    h_ref[...] = (x2 * (1.0 + sc_ref[0]) + sh_ref[0]).astype(BF16)


def _ffn_body(h_ref, x_ref, w1_ref, w3_ref, w2_ref, gt_ref, sc_ref, sh_ref, lg_ref, lb_ref,
              x2_ref, hn_ref, acc, *, nf):
    f = pl.program_id(1)
    h = h_ref[...]
    u = (_silu(_dot(h, w1_ref[...])) * _dot(h, w3_ref[...])).astype(BF16)
    part = _dot(u, w2_ref[...])

    @pl.when(f == 0)
    def _():
        acc[...] = part

    @pl.when(f > 0)
    def _():
        acc[...] = acc[...] + part

    @pl.when(f == nf - 1)
    def _():
        _ffn_finish(acc[...], x_ref, gt_ref, sc_ref, sh_ref, lg_ref, lb_ref, x2_ref, hn_ref)


def _ffn_dense(h, x, w1, w3, w2, gt, sc, sh, ln_g, ln_b, tm, tiles_per_group):
    n = x.shape[0]
    tf = D_FF // 2
    nf = D_FF // tf
    blk = pl.BlockSpec((tm, D_MODEL), lambda i, f: (i, 0))
    row = pl.BlockSpec((1, D_MODEL), lambda i, f: (0, 0))
    ms = lambda m: _mod_spec(m, tm, tiles_per_group)
    return pl.pallas_call(
        functools.partial(_ffn_body, nf=nf),
        grid=(n // tm, nf),
        in_specs=[blk, blk,
                  pl.BlockSpec((D_MODEL, tf), lambda i, f: (0, f)),
                  pl.BlockSpec((D_MODEL, tf), lambda i, f: (0, f)),
                  pl.BlockSpec((tf, D_MODEL), lambda i, f: (f, 0)),
                  ms(gt), ms(sc), ms(sh), row, row],
        out_specs=[blk, blk],
        out_shape=[jax.ShapeDtypeStruct((n, D_MODEL), F32), jax.ShapeDtypeStruct((n, D_MODEL), BF16)],
        scratch_shapes=[pltpu.VMEM((tm, D_MODEL), F32)],
        compiler_params=_cp(("arbitrary", "arbitrary")),
        name="ffn_dense",
    )(h, x, w1, w3, w2, gt, sc, sh, ln_g, ln_b)


def _moe_body(h_ref, x_ref, rw_ref, rb_ref, w1_ref, w3_ref, w2_ref, gt_ref, sc_ref, sh_ref, lg_ref, lb_ref,
              x2_ref, hn_ref, acc, gate):
    e = pl.program_id(1)
    h = h_ref[...]
    lane = _iota((h.shape[0], LANES), 1)

    @pl.when(e == 0)
    def _():
        logits = _dot(h, rw_ref[...]) + rb_ref[...]
        m1 = jnp.max(logits, axis=1, keepdims=True)
        i1 = jnp.min(jnp.where(logits == m1, lane, LANES), axis=1, keepdims=True)
        rest = jnp.where(lane == i1, NEG_INF, logits)
        m2 = jnp.max(rest, axis=1, keepdims=True)
        i2 = jnp.min(jnp.where(rest == m2, lane, LANES), axis=1, keepdims=True)
        e2 = jnp.exp(m2 - m1)
        p1 = 1.0 / (1.0 + e2)
        p2 = e2 / (1.0 + e2)
        gate[...] = jnp.where(lane == i1, p1, 0.0) + jnp.where(lane == i2, p2, 0.0)
        acc[...] = jnp.zeros(acc.shape, F32)

    ge = jnp.sum(jnp.where(lane == e, gate[...], 0.0), axis=1, keepdims=True)
    u = (_silu(_dot(h, w1_ref[0])) * _dot(h, w3_ref[0])).astype(BF16)
    acc[...] = acc[...] + ge * _dot(u, w2_ref[0])

    @pl.when(e == N_EXPERTS - 1)
    def _():
        _ffn_finish(acc[...], x_ref, gt_ref, sc_ref, sh_ref, lg_ref, lb_ref, x2_ref, hn_ref)


def _ffn_moe(h, x, rw, rb, w1, w3, w2, gt, sc, sh, ln_g, ln_b, tm, tiles_per_group):
    n = x.shape[0]
    blk = pl.BlockSpec((tm, D_MODEL), lambda i, e: (i, 0))
    row = pl.BlockSpec((1, D_MODEL), lambda i, e: (0, 0))
    ms = lambda m: _mod_spec(m, tm, tiles_per_group)
    return pl.pallas_call(
        _moe_body,
        grid=(n // tm, N_EXPERTS),
        in_specs=[blk, blk,
                  pl.BlockSpec((D_MODEL, LANES), lambda i, e: (0, 0)),
                  pl.BlockSpec((1, LANES), lambda i, e: (0, 0)),
                  pl.BlockSpec((1, D_MODEL, D_FF_E), lambda i, e: (e, 0, 0)),
                  pl.BlockSpec((1, D_MODEL, D_FF_E), lambda i, e: (e, 0, 0)),
                  pl.BlockSpec((1, D_FF_E, D_MODEL), lambda i, e: (e, 0, 0)),
                  ms(gt), ms(sc), ms(sh), row, row],
        out_specs=[blk, blk],
        out_shape=[jax.ShapeDtypeStruct((n, D_MODEL), F32), jax.ShapeDtypeStruct((n, D_MODEL), BF16)],
        scratch_shapes=[pltpu.VMEM((tm, D_MODEL), F32), pltpu.VMEM((tm, LANES), F32)],
        compiler_params=_cp(("arbitrary", "arbitrary")),
        name="ffn_moe",
    )(h, x, rw, rb, w1, w3, w2, gt, sc, sh, ln_g, ln_b)


def _pack_w_in(w_in):
    o_iw = 8 * 512
    o_ik = o_iw + H_I
    o_pc = o_ik + D_I
    o_gl = o_pc + D_C_IN
    z = lambda w: jnp.zeros(w_in.shape[:2] + (w,), w_in.dtype)
    packed = jnp.concatenate(
        [w_in[..., :o_iw], w_in[..., o_ik:o_pc], w_in[..., o_iw:o_ik], z(LANES - D_I - H_I), z(LANES),
         w_in[..., o_pc:o_gl], w_in[..., o_gl:]], axis=-1)
    assert packed.shape[-1] == D_PACK
    return packed.astype(BF16)


def _scan_state_in(s):
    b = s.shape[0]
    s = s.reshape(b, H_C // 2, 2, D_C, D_C).transpose(0, 1, 3, 2, 4)
    return s.reshape(b * (H_C // 2), D_C, 2 * D_C)


def _scan_state_out(s, b):
    s = s.reshape(b, H_C // 2, D_C, 2, D_C).transpose(0, 1, 3, 2, 4)
    return s.reshape(b, H_C, D_C, D_C)


def _mods(ada_l, lo, hi, per_row_t):
    parts = jnp.split(ada_l[lo:hi], 6, axis=-1)
    if per_row_t is None:
        return [p[:, None, :] for p in parts]
    return [jnp.repeat(p, per_row_t, axis=0)[None] for p in parts]


def _layer(l, x, h, mods, mods_next, prm, attend_fn, ret_s0, wkv_s0, shift_rows, nb, t_len):
    n = nb * t_len
    tm_proj = min(n, TM_PROJ)
    tm = min(n, TM_ROW)
    sh1, sc1, gt1, sh2, sc2, gt2 = mods
    p2 = _in_proj(h, prm["w_in"][l], tm_proj)

    rb = CHUNK_A if t_len % CHUNK_A == 0 else t_len
    nchunk = t_len // rb
    oa, ret_s = _retention(p2.reshape(nb * nchunk, rb, D_PACK), ret_s0, prm["lgam"], prm["ret_gn_g"][l],
                           nb, nchunk, rb)
    oa = oa.reshape(n, H_A * DV_A)

    ob = attend_fn(p2)

    rp = {k: prm["rw_" + k][l] for k in ("mu", "w0", "a0", "kk", "ka", "rk", "wwa", "g2")}
    r, w, k2, v, na, b, g, bonus = _rwkv_prep(p2, shift_rows, rp, tm, t_len)
    tb = min(t_len, SCAN_TB)
    if t_len >= tm:
        tmaj = lambda a: a.reshape(t_len, nb, W_C)
    else:
        tmaj = lambda a: a.reshape(nb, t_len, W_C).transpose(1, 0, 2)
    y, wkv_s = _rwkv_scan(tmaj(r), tmaj(w), tmaj(k2), tmaj(v), tmaj(na), tmaj(b), wkv_s0, nb, t_len, tb)
    y = y.reshape(t_len, nb * W_C) if t_len >= tm else y.transpose(1, 0, 2).reshape(n, W_C)
    oc = _rwkv_post(y, g, bonus, prm["rw_lnx_g"][l], prm["rw_lnx_b"][l], tm, t_len)

    tm_m = tm
    tpg_m = max(t_len // tm_m, 1)
    x1, h2 = _merge(oa, ob, oc, p2, x, prm["w_branch"][l], prm["w_out"][l], gt1, sc2, sh2,
                    prm["ln1_g"][l], prm["ln1_b"][l], tm_m, tpg_m)
    sh_n, sc_n = mods_next
    i = l // 2
    if l % 2 == 0:
        x2, hn = _ffn_dense(h2, x1, prm["ffn_w1"][i], prm["ffn_w3"][i], prm["ffn_w2"][i], gt2, sc_n, sh_n,
                            prm["ln2_g"][l], prm["ln2_b"][l], tm_m, tpg_m)
    else:
        x2, hn = _ffn_moe(h2, x1, prm["moe_router"][i], prm["moe_router_b"][i], prm["moe_w1"][i],
                          prm["moe_w3"][i], prm["moe_w2"][i], gt2, sc_n, sh_n,
                          prm["ln2_g"][l], prm["ln2_b"][l], tm_m, tpg_m)
    shift_s = p2.reshape(nb, t_len, D_PACK)[:, -1, U_PC * LANES:U_PC * LANES + D_C_IN]
    kh =p2[:, U_KB * LANES:U_KB * LANES + W_B].reshape(nb, t_len, H_B, D_B)
    vh = p2[:, U_VB * LANES:U_VB * LANES + W_B].reshape(nb, t_len, H_B, D_B)
    ik = p2[:, U_MISC * LANES:U_MISC * LANES + D_I].reshape(nb, t_len, D_I)
    return x2, hn, (kh, vh, ik, ret_s, _scan_state_out(wkv_s, nb), shift_s)


def kernel(x_prompt, x_sample, c_prompt, c_sample, cache_k, cache_v, cache_kidx, page_table, state_ret, state_wkv, state_shift, ada_w, ada_b, w_in, ret_gn_g, rw_mu, rw_w0, rw_w2, rw_a0, rw_a2, rw_g2, rw_kk, rw_ka, rw_rk, rw_lnx_g, rw_lnx_b, w_branch, w_out, ln1_g, ln1_b, ln2_g, ln2_b, ffn_w1, ffn_w3, ffn_w2, moe_router, moe_router_b, moe_w1, moe_w3, moe_w2):
    bp, tp, _ = x_prompt.shape
    bs, ts, _ = x_sample.shape
    n_pages = page_table.shape[1]
    n_pool = cache_k.shape[1]

    zw = jnp.zeros((DEPTH, LORA_W, W_C), F32)
    wwa = jnp.concatenate([jnp.concatenate([rw_w2, zw], axis=-1),
                           jnp.concatenate([zw, rw_a2], axis=-1)], axis=1).astype(BF16)
    n_moe = moe_router.shape[0]
    router_pad = jnp.concatenate(
        [moe_router, jnp.zeros((n_moe, D_MODEL, LANES - N_EXPERTS), F32)], axis=-1).astype(BF16)
    router_b_pad = jnp.concatenate(
        [moe_router_b, jnp.full((n_moe, LANES - N_EXPERTS), -1e30, F32)], axis=-1)[:, None, :]
    log_gamma = jnp.log1p(-jnp.exp2(-5.0 - jnp.arange(H_A, dtype=F32)))
    row = lambda a: a.reshape(DEPTH, 1, -1)
    prm = dict(
        w_in=_pack_w_in(w_in), lgam=jnp.repeat(log_gamma, DK_A)[None, :], ret_gn_g=row(ret_gn_g),
        rw_mu=row(rw_mu), rw_w0=row(rw_w0), rw_a0=row(rw_a0), rw_kk=row(rw_kk), rw_ka=row(rw_ka),
        rw_rk=row(rw_rk), rw_wwa=wwa, rw_g2=rw_g2.astype(BF16), rw_lnx_g=row(rw_lnx_g), rw_lnx_b=row(rw_lnx_b),
        w_branch=w_branch.astype(BF16), w_out=w_out.astype(BF16),
        ln1_g=row(ln1_g), ln1_b=row(ln1_b), ln2_g=row(ln2_g), ln2_b=row(ln2_b),
        ffn_w1=ffn_w1.astype(BF16), ffn_w3=ffn_w3.astype(BF16), ffn_w2=ffn_w2.astype(BF16),
        moe_router=router_pad, moe_router_b=router_b_pad,
        moe_w1=moe_w1.astype(BF16), moe_w3=moe_w3.astype(BF16), moe_w2=moe_w2.astype(BF16))

    ada = _ada_all(jnp.concatenate([c_prompt, c_sample], axis=0), ada_w.astype(BF16), ada_b)
    mods_p = [_mods(ada[l], 0, bp, None) for l in range(DEPTH)]
    mods_s = [_mods(ada[l], bp, bp + bs, ts) for l in range(DEPTH)]

    np_, ns_ = bp * tp, bs * ts
    tm_p = min(tp, TM_ROW)
    xp = x_prompt.reshape(np_, D_MODEL)
    xs = x_sample.reshape(ns_, D_MODEL)
    hp = _modulate(xp, mods_p[0][1], mods_p[0][0], tm_p, tp // tm_p)
    hs = _modulate(xs, mods_s[0][1], mods_s[0][0], ns_, 1)

    cache_k4 = cache_k.reshape(DEPTH, n_pool, PAGE_SIZE, W_B)
    cache_v4 = cache_v.reshape(DEPTH, n_pool, PAGE_SIZE, W_B)
    zeros_ret = jnp.zeros((bp, H_A, DK_A, DV_A), F32)
    zeros_wkv = jnp.zeros((bp * (H_C // 2), D_C, 2 * D_C), F32)
    zeros_shift = jnp.zeros((bp, 1, D_C_IN), F32)

    st_p, st_s = [], []
    for l in range(DEPTH):
        nxt = min(l + 1, DEPTH - 1)
        attend_p = lambda p2: _dsa_prompt(p2, bp, tp)
        xp, hp, sp = _layer(l, xp, hp, mods_p[l], (mods_p[nxt][0], mods_p[nxt][1]), prm, attend_p,
                            zeros_ret, zeros_wkv, zeros_shift, bp, tp)

        def attend_s(p2, l=l):
            pad = lambda a: jnp.pad(a.reshape(bs, ts, -1), ((0, 0), (0, PAGE_SIZE - ts), (0, 0)))
            q3 = p2[:, U_QB * LANES:U_QB * LANES + W_B].reshape(bs, ts, W_B)
            iq32 = p2[:, U_IQ * LANES:U_IQ * LANES + W_B].reshape(bs, ts * H_I, D_I)
            iw = p2[:, U_MISC * LANES + IW_OFF:U_MISC * LANES + IW_OFF + H_I].reshape(bs, ts * H_I, 1)
            iwb = jnp.broadcast_to(iw, (bs, ts * H_I, LANES))
            knew = pad(p2[:, U_KB * LANES:U_KB * LANES + W_B])
            vnew = pad(p2[:, U_VB * LANES:U_VB * LANES + W_B])
            iknew = pad(p2[:, U_MISC * LANES:U_MISC * LANES + D_I])
            scores = _dsa_sample_scores(l, page_table, cache_kidx, iq32, iwb, ts)
            o = _dsa_sample_attend(l, page_table, scores, q3, iq32, iwb, knew, vnew, iknew,
                                   cache_k4, cache_v4, ts)
            return o[:, :ts].reshape(ns_, W_B)

        shift_rows = jnp.repeat(state_shift[l], ts, axis=0)[None]
        xs, hs, ss = _layer(l, xs, hs, mods_s[l], (mods_s[nxt][0], mods_s[nxt][1]), prm, attend_s,
                            state_ret[l], _scan_state_in(state_wkv[l]), shift_rows, bs, ts)
        st_p.append(sp)
        st_s.append(ss)

    outs = [xp.reshape(bp, tp, D_MODEL), xs.reshape(bs, ts, D_MODEL)]
    for j in range(6):
        outs.append(jnp.stack([s[j] for s in st_p]))
        outs.append(jnp.stack([s[j] for s in st_s]))
    return tuple(outs)
```

```python
import functools
import math

import jax
import jax.numpy as jnp
import numpy as np
from jax import lax
from jax.experimental import pallas as pl
from jax.experimental.pallas import tpu as pltpu

F32 = jnp.float32
BF16 = jnp.bfloat16
I32 = jnp.int32

D_MODEL = 1024
DEPTH = 4
PAGE_SIZE = 128
H_A, DK_A, DV_A = 4, 128, 128
CHUNK_A = 128
H_B, D_B = 8, 64
W_B = H_B * D_B
H_I, D_I = 8, 64
TOPK_MAX = 256
H_C, D_C = 8, 64
W_C = H_C * D_C
LORA_W, LORA_A, LORA_G = 64, 64, 128
RWKV_GN_EPS = 64e-5
N_BRANCH = 3
BRANCH_W = 512
D_FF = 2816
N_EXPERTS = 8
D_FF_E = 1408
ALPHA = (2.0 * DEPTH) ** 0.25
LN_EPS = 1e-5
D_C_IN = 3 * W_C + LORA_W + LORA_A + LORA_G

LANES = 128
SUBLANES = 8
VMEM_LIMIT = 56 * 1024 * 1024
TM_PROJ = 1024
TM_ROW = 512

U_QA, U_KA, U_VA, U_GA = 0, 4, 8, 12
U_QB, U_KB, U_VB, U_IQ = 16, 20, 24, 28
U_MISC = 32
U_PC = 34
U_GL = 48
N_UNITS = 72
PC_BLOCK_W = 2 * LANES
PC_BLOCKS = 7
D_PACK = N_UNITS * LANES
IW_OFF = 64

NEG_INF = float("-inf")


def _cp(sem, vmem=VMEM_LIMIT):
    return pltpu.CompilerParams(dimension_semantics=sem, vmem_limit_bytes=vmem)


def _dot(a, b):
    return jnp.dot(a, b, preferred_element_type=F32)


def _dot_nt(a, b):
    return lax.dot_general(a, b, (((1,), (1,)), ((), ())), preferred_element_type=F32)


def _dot_tn(a, b):
    return lax.dot_general(a, b, (((0,), (0,)), ((), ())), preferred_element_type=F32)


def _sigmoid(x):
    return 1.0 / (1.0 + jnp.exp(-x))


def _silu(x):
    return x * _sigmoid(x)


def _iota(shape, dim):
    return lax.broadcasted_iota(I32, shape, dim)


def _rows_to_tile(rows, width, fill=0.0):
    ri = _iota((SUBLANES, width), 0)
    out = jnp.full((SUBLANES, width), fill, F32)
    for t, r in enumerate(rows):
        out = jnp.where(ri == t, jnp.broadcast_to(r, (SUBLANES, width)), out)
    return out


def _ada_body(c_ref, w_ref, b_ref, o_ref):
    c = c_ref[...]
    o_ref[0] = _dot(_silu(c).astype(BF16), w_ref[0]) + b_ref[0]


def _ada_all(c_all, ada_w_bf, ada_b):
    m = c_all.shape[0]
    n = ada_w_bf.shape[-1]
    tn = 1024
    return pl.pallas_call(
        _ada_body,
        grid=(DEPTH, n // tn),
        in_specs=[pl.BlockSpec((m, D_MODEL), lambda l, j: (0, 0)),
                  pl.BlockSpec((1, D_MODEL, tn), lambda l, j: (l, 0, j)),
                  pl.BlockSpec((1, 1, tn), lambda l, j: (l, 0, j))],
        out_specs=pl.BlockSpec((1, m, tn), lambda l, j: (l, 0, j)),
        out_shape=jax.ShapeDtypeStruct((DEPTH, m, n), F32),
        compiler_params=_cp(("arbitrary", "arbitrary")),
        name="ada",
    )(c_all, ada_w_bf, ada_b.reshape(DEPTH, 1, n))


def _mod_spec(mod, tm, tiles_per_group):
    g, r, d = mod.shape
    if r == 1:
        return pl.BlockSpec((1, 1, d), lambda i, *_: (i // tiles_per_group, 0, 0))
    assert r == tm and g == 1
    return pl.BlockSpec((1, r, d), lambda i, *_: (0, 0, 0))


def _modulate_body(x_ref, sc_ref, sh_ref, h_ref):
    h_ref[...] = (x_ref[...] * (1.0 + sc_ref[0]) + sh_ref[0]).astype(BF16)


def _modulate(x, sc, sh, tm, tiles_per_group):
    n = x.shape[0]
    return pl.pallas_call(
        _modulate_body,
        grid=(n // tm,),
        in_specs=[pl.BlockSpec((tm, D_MODEL), lambda i: (i, 0)),
                  _mod_spec(sc, tm, tiles_per_group), _mod_spec(sh, tm, tiles_per_group)],
        out_specs=pl.BlockSpec((tm, D_MODEL), lambda i: (i, 0)),
        out_shape=jax.ShapeDtypeStruct((n, D_MODEL), BF16),
        compiler_params=_cp(("arbitrary",)),
        name="modulate",
    )(x, sc, sh)


def _mm_body(x_ref, w_ref, o_ref):
    o_ref[...] = _dot(x_ref[...], w_ref[...])


def _in_proj(h, w_pack, tm):
    n = h.shape[0]
    tn = 1024
    return pl.pallas_call(
        _mm_body,
        grid=(n // tm, D_PACK // tn),
        in_specs=[pl.BlockSpec((tm, D_MODEL), lambda i, j: (i, 0)),
                  pl.BlockSpec((D_MODEL, tn), lambda i, j: (0, j))],
        out_specs=pl.BlockSpec((tm, tn), lambda i, j: (i, j)),
        out_shape=jax.ShapeDtypeStruct((n, D_PACK), F32),
        compiler_params=_cp(("arbitrary", "arbitrary")),
        name="in_proj",
    )(h, w_pack)


def _retention_body(q_ref, k_ref, v_ref, g_ref, s0_ref, lg_ref, gn_ref, o_ref, sout_ref,
                    s_scr, pad_scr, *, rb, nchunk):
    n = pl.program_id(1)
    c = CHUNK_A

    @pl.when(n == 0)
    def _():
        s_scr[...] = s0_ref[0]

    def padded(ref, slot):
        if rb == c:
            return ref[0]
        pad_scr[slot] = jnp.zeros((c, H_A * DK_A), F32)
        pad_scr[slot, 0:rb, :] = ref[0]
        return pad_scr[slot]

    q_all = padded(q_ref, 0)
    k_all = padded(k_ref, 1)
    v_all = padded(v_ref, 2)
    g_all = padded(g_ref, 3)

    row = _iota((c, c), 0).astype(F32)
    col = _iota((c, c), 1).astype(F32)
    rel = row - col
    outs = []
    for h in range(H_A):
        sl = slice(h * DK_A, (h + 1) * DK_A)
        lg = lg_ref[0:1, sl]
        q = (q_all[:, sl] * (DK_A ** -0.5)).astype(BF16)
        k = k_all[:, sl]
        v = v_all[:, sl].astype(BF16)
        s0 = s_scr[h]
        decay = jnp.where(rel >= 0.0, jnp.exp(lg * jnp.maximum(rel, 0.0)), 0.0)
        scores = _dot_nt(q, k.astype(BF16)) * decay
        inner = _dot(scores.astype(BF16), v)
        xi = jnp.exp(lg * (row + 1.0))
        cross = _dot(q, s0.astype(BF16)) * xi
        zeta = jnp.where(row < float(rb), jnp.exp(lg * (float(rb) - 1.0 - row)), 0.0)
        s_new = jnp.exp(lg * float(rb)) * s0 + _dot_tn((k * zeta).astype(BF16), v)
        s_scr[h] = s_new
        o = inner + cross
        mu = jnp.mean(o, axis=-1, keepdims=True)
        var = jnp.mean(jnp.square(o - mu), axis=-1, keepdims=True)
        o = (o - mu) * lax.rsqrt(var + LN_EPS) * gn_ref[0:1, sl]
        outs.append(_silu(g_all[:, sl]) * o)
    o_full = jnp.concatenate(outs, axis=-1)
    o_ref[0] = o_full[0:rb].astype(BF16)

    @pl.when(n == nchunk - 1)
    def _():
        sout_ref[0] = s_scr[...]


def _retention(p3, s0, lgam, gn_g, nb, nchunk, rb):
    w = H_A * DK_A
    col = lambda u: (lambda b, n: (b * nchunk + n, 0, u // 4))
    body = functools.partial(_retention_body, rb=rb, nchunk=nchunk)
    return pl.pallas_call(
        body,
        grid=(nb, nchunk),
        in_specs=[pl.BlockSpec((1, rb, w), col(U_QA)), pl.BlockSpec((1, rb, w), col(U_KA)),
                  pl.BlockSpec((1, rb, w), col(U_VA)), pl.BlockSpec((1, rb, w), col(U_GA)),
                  pl.BlockSpec((1, H_A, DK_A, DV_A), lambda b, n: (b, 0, 0, 0)),
                  pl.BlockSpec((1, w), lambda b, n: (0, 0)),
                  pl.BlockSpec((1, w), lambda b, n: (0, 0))],
        out_specs=[pl.BlockSpec((1, rb, w), lambda b, n: (b * nchunk + n, 0, 0)),
                   pl.BlockSpec((1, H_A, DK_A, DV_A), lambda b, n: (b, 0, 0, 0))],
        out_shape=[jax.ShapeDtypeStruct((nb * nchunk, rb, w), BF16),
                   jax.ShapeDtypeStruct((nb, H_A, DK_A, DV_A), F32)],
        scratch_shapes=[pltpu.VMEM((H_A, DK_A, DV_A), F32), pltpu.VMEM((4, CHUNK_A, w), F32)],
        compiler_params=_cp(("arbitrary", "arbitrary")),
        name="retention",
    )(p3, p3, p3, p3, s0, lgam, gn_g)


def _score_key(s):
    bits = pltpu.bitcast(s, I32)
    return jnp.where(bits < 0, bits ^ jnp.int32(0x7FFFFFFF), bits)


def _tri_incl():
    return jnp.where(_iota((LANES, LANES), 0) <= _iota((LANES, LANES), 1), 1.0, 0.0).astype(BF16)


KC = 512
DSA_QB = 256


def _dsa_prompt_body(q_ref, iq_ref, mq_ref, k_ref, v_ref, mk_ref, o_ref,
                     kx, vbf, ikbf, keys, bias, q2, *, t_len, top):
    i = pl.program_id(1)
    nkc = t_len // KC
    qb = min(DSA_QB, t_len)

    @pl.when(i == 0)
    def _():
        prow = _iota((KC, LANES), 0)
        plane = _iota((KC, LANES), 1)
        for j in range(nkc):
            kpos = j * KC + prow
            posf = jnp.where(plane == 0, kpos // 64, jnp.where(plane == 1, kpos % 64,
                                                                jnp.where(plane == 2, 1, 0)))
            posf = posf.astype(F32).astype(BF16)
            for hp in range(H_B // 2):
                kx[j, :, hp * 2 * LANES:hp * 2 * LANES + LANES] = (
                    k_ref[j * KC:(j + 1) * KC, hp * LANES:(hp + 1) * LANES].astype(BF16))
                kx[j, :, hp * 2 * LANES + LANES:(hp + 1) * 2 * LANES] = posf
            vbf[j] = v_ref[j * KC:(j + 1) * KC, :].astype(BF16)
            ikbf[j] = mk_ref[j * KC:(j + 1) * KC, :].astype(BF16)

    nsc = (i * qb) // KC + 1
    qpos = i * qb + _iota((qb, KC), 0)
    lane = _iota((qb, KC), 1)

    iq = (iq_ref[...] * (D_I ** -0.5)).astype(BF16)
    mq = mq_ref[...]
    iw = [jnp.broadcast_to(mq[:, IW_OFF + h:IW_OFF + h + 1] * (H_I ** -0.5), (qb, LANES))
          for h in range(H_I)]

    def score_chunk(j, carry):
        ik = ikbf[j][:, 0:D_I]
        acc = jnp.zeros((qb, KC), F32)
        for h in range(H_I):
            s = _dot_nt(iq[:, h * D_I:(h + 1) * D_I], ik)
            wh = jnp.concatenate([iw[h]] * (KC // LANES), axis=1)
            acc = acc + jnp.maximum(s, 0.0) * wh
        kpos = j * KC + lane
        acc = jnp.where(kpos <= qpos, acc + 0.0, NEG_INF)
        keys[j] = _score_key(acc)
        return carry

    lax.fori_loop(0, nsc, score_chunk, 0)

    def count_ge(cand):
        def body(j, acc):
            m = jnp.where(keys[j] >= cand, 1.0, 0.0)
            for t in range(KC // LANES):
                acc = acc + m[:, t * LANES:(t + 1) * LANES]
            return acc
        acc = lax.fori_loop(0, nsc, body, jnp.zeros((qb, LANES), F32))
        return jnp.sum(acc, axis=1, keepdims=True)

    lowest = jnp.full((qb, 1), -2 ** 31, I32)

    def bisect():
        prefix = lowest
        for bit in range(31, -1, -1):
            step = jnp.int32(-2 ** 31) if bit == 31 else jnp.int32(1 << bit)
            cand = prefix + step
            cnt = count_ge(cand)
            prefix = jnp.where(cnt >= float(top), cand, prefix)
        return prefix

    neg_inf_key = _score_key(jnp.full((qb, 1), NEG_INF, F32))
    tau = lax.cond((i + 1) * qb <= top, lambda: neg_inf_key, bisect)

    def count_gt(j, acc):
        m = jnp.where(keys[j] > tau, 1.0, 0.0)
        for t in range(KC // LANES):
            acc = acc + m[:, t * LANES:(t + 1) * LANES]
        return acc
    n_gt = jnp.sum(lax.fori_loop(0, nsc, count_gt, jnp.zeros((qb, LANES), F32)), axis=1, keepdims=True)
    need = float(top) - n_gt

    tri = _tri_incl()

    def bias_chunk(j, carry):
        kj = keys[j]
        gt = kj > tau
        eq = jnp.where(kj == tau, 1.0, 0.0)
        kpos = j * KC + lane
        parts = []
        for t in range(KC // LANES):
            e = eq[:, t * LANES:(t + 1) * LANES]
            pre = _dot(e.astype(BF16), tri) + carry
            parts.append(jnp.where((e > 0.0) & (pre <= need), 1.0, 0.0))
            carry = carry + jnp.sum(e, axis=1, keepdims=True)
        sel = jnp.concatenate(parts, axis=1)
        keep = (gt | (sel > 0.0)) & (kpos <= qpos)
        bias[j] = jnp.where(keep, 0.0, NEG_INF)
        return carry

    lax.fori_loop(0, nsc, bias_chunk, jnp.zeros((qb, 1), F32))

    npair = H_B // 2
    q_all = q_ref[...]
    lane_q = _iota((qb, LANES), 1)
    base = (i * qb + 0 * lane_q).astype(F32)
    for hp in range(npair):
        qp = q_all[:, hp * LANES:(hp + 1) * LANES] * (D_B ** -0.5)
        for h2 in range(2):
            slope = 2.0 ** (-8.0 * (2 * hp + h2 + 1) / H_B)
            in_half = (lane_q >= h2 * D_B) & (lane_q < (h2 + 1) * D_B)
            feat = jnp.where(lane_q == 0, 64.0 * slope,
                             jnp.where(lane_q == 1, slope, jnp.where(lane_q == 2, -slope * base, 0.0)))
            q2[hp, h2 * qb:(h2 + 1) * qb, 0:LANES] = jnp.where(in_half, qp, 0.0).astype(BF16)
            q2[hp, h2 * qb:(h2 + 1) * qb, LANES:2 * LANES] = feat.astype(BF16)

    def attn_chunk(j, carry):
        ms, ls, accs = carry
        b2 = jnp.concatenate([bias[j], bias[j]], axis=0)
        out_m, out_l, out_acc = [], [], []
        for hp in range(npair):
            s = _dot_nt(q2[hp], kx[j, :, hp * 2 * LANES:(hp + 1) * 2 * LANES]) + b2
            m_new = jnp.maximum(ms[hp], jnp.max(s, axis=1, keepdims=True))
            m_safe = jnp.where(m_new == NEG_INF, 0.0, m_new)
            alpha = jnp.exp(ms[hp] - m_safe)
            p = jnp.exp(s - m_safe)
            out_m.append(m_new)
            out_l.append(ls[hp] * alpha + jnp.sum(p, axis=1, keepdims=True))
            out_acc.append(accs[hp] * alpha + _dot(p.astype(BF16), vbf[j, :, hp * LANES:(hp + 1) * LANES]))
        return out_m, out_l, out_acc

    init = ([jnp.full((2 * qb, 1), NEG_INF, F32)] * npair, [jnp.zeros((2 * qb, 1), F32)] * npair,
            [jnp.zeros((2 * qb, LANES), F32)] * npair)
    _, ls, accs = lax.fori_loop(0, nsc, attn_chunk, init)
    outs = []
    for hp in range(npair):
        o = accs[hp] / ls[hp]
        outs.append(jnp.where(lane_q < D_B, o[0:qb], o[qb:2 * qb]))
    o_ref[...] = jnp.concatenate(outs, axis=1).astype(BF16)


def _dsa_prompt(p2, nb, t_len):
    top = min(TOPK_MAX, t_len // 4)
    qb = min(DSA_QB, t_len)
    nq = t_len // qb
    nkc = t_len // KC
    body = functools.partial(_dsa_prompt_body, t_len=t_len, top=top)
    qspec = lambda u, wdt: pl.BlockSpec((qb, wdt), lambda b, i: (b * nq + i, u * LANES // wdt))
    aspec = lambda u, wdt: pl.BlockSpec((t_len, wdt), lambda b, i: (b, u * LANES // wdt))
    return pl.pallas_call(
        body,
        grid=(nb, nq),
        in_specs=[qspec(U_QB, W_B), qspec(U_IQ, W_B), qspec(U_MISC, LANES),
                  aspec(U_KB, W_B), aspec(U_VB, W_B), aspec(U_MISC, LANES)],
        out_specs=pl.BlockSpec((qb, W_B), lambda b, i: (b * nq + i, 0)),
        out_shape=jax.ShapeDtypeStruct((nb * t_len, W_B), BF16),
        scratch_shapes=[pltpu.VMEM((nkc, KC, 2 * W_B), BF16), pltpu.VMEM((nkc, KC, W_B), BF16),
                        pltpu.VMEM((nkc, KC, LANES), BF16),
                        pltpu.VMEM((nkc, qb, KC), I32), pltpu.VMEM((nkc, qb, KC), F32),
                        pltpu.VMEM((H_B // 2, 2 * qb, 2 * LANES), BF16)],
        compiler_params=_cp(("arbitrary", "arbitrary")),
        name="dsa_prompt",
    )(p2, p2, p2, p2, p2, p2)


PG_STEP = 32


def _dsa_s_score_body(pt_ref, iq_ref, iwb_ref, *rest, t_len):
    pages = rest[:PG_STEP]
    o_ref = rest[PG_STEP]
    iq = (iq_ref[0] * (D_I ** -0.5)).astype(BF16)
    iwb = iwb_ref[0] * (H_I ** -0.5)
    row = _iota((SUBLANES, LANES), 0)
    for j in range(PG_STEP):
        ikt = pages[j][0, 0].astype(BF16)
        s = jnp.maximum(_dot(iq, ikt), 0.0) * iwb
        per_t = [jnp.sum(s[t * H_I:(t + 1) * H_I], axis=0, keepdims=True) for t in range(t_len)]
        sc = _rows_to_tile(per_t, LANES)
        o_ref[0, j] = jnp.where(row < t_len, sc + 0.0, NEG_INF)


def _dsa_sample_scores(layer, page_table, cache_kidx_t, iq32, iwb, t_len):
    nb, n_pages = page_table.shape
    nsteps = n_pages // PG_STEP
    rows = t_len * H_I

    def page_spec(j):
        return pl.BlockSpec((1, 1, D_I, PAGE_SIZE),
                            lambda b, s, pt: (layer, pt[b, s * PG_STEP + j], 0, 0))

    grid_spec = pltpu.PrefetchScalarGridSpec(
        num_scalar_prefetch=1,
        grid=(nb, nsteps),
        in_specs=[pl.BlockSpec((1, rows, D_I), lambda b, s, pt: (b, 0, 0)),
                  pl.BlockSpec((1, rows, LANES), lambda b, s, pt: (b, 0, 0))]
                 + [page_spec(j) for j in range(PG_STEP)],
        out_specs=pl.BlockSpec((1, PG_STEP, SUBLANES, LANES), lambda b, s, pt: (b, s, 0, 0)),
    )
    return pl.pallas_call(
        functools.partial(_dsa_s_score_body, t_len=t_len),
        grid_spec=grid_spec,
        out_shape=jax.ShapeDtypeStruct((nb, n_pages, SUBLANES, LANES), F32),
        compiler_params=_cp(("arbitrary", "arbitrary")),
        name="dsa_sample_scores",
    )(page_table, iq32, iwb, *([cache_kidx_t] * PG_STEP))


def _dsa_s_attn_body(pt_ref, sc_ref, q_ref, iq_ref, iwb_ref, knew_ref, vnew_ref, iknew_ref, *rest,
                     t_len, n_pages, top):
    kpages = rest[:PG_STEP]
    vpages = rest[PG_STEP:2 * PG_STEP]
    o_ref = rest[2 * PG_STEP]
    bias, m_scr, l_scr, acc_scr = rest[2 * PG_STEP + 1:]
    s_idx = pl.program_id(1)
    nsteps = n_pages // PG_STEP
    past = n_pages * PAGE_SIZE
    rows = t_len * H_B
    row8 = _iota((SUBLANES, LANES), 0)
    lane8 = _iota((SUBLANES, LANES), 1)

    @pl.when(s_idx == 0)
    def _():
        iq = (iq_ref[0] * (D_I ** -0.5)).astype(BF16)
        iwb = iwb_ref[0] * (H_I ** -0.5)
        s = jnp.maximum(_dot(iq, iknew_ref[0].astype(BF16)), 0.0) * iwb
        per_t = [jnp.sum(s[t * H_I:(t + 1) * H_I], axis=0, keepdims=True) for t in range(t_len)]
        sc_new = _rows_to_tile(per_t, LANES)
        sc_new = jnp.where((row8 < t_len) & (lane8 <= row8), sc_new + 0.0, NEG_INF)

        keys_past = _score_key(sc_ref[0])
        keys_new = _score_key(sc_new)

        def count(pred_past, pred_new):
            c = jnp.sum(jnp.where(pred_past, 1.0, 0.0), axis=0) + jnp.where(pred_new, 1.0, 0.0)
            return jnp.sum(c, axis=1, keepdims=True)

        prefix = jnp.full((SUBLANES, 1), -2 ** 31, I32)
        for bit in range(31, -1, -1):
            step = jnp.int32(-2 ** 31) if bit == 31 else jnp.int32(1 << bit)
            cand = prefix + step
            cnt = count(keys_past >= cand[None], keys_new >= cand)
            prefix = jnp.where(cnt >= float(top), cand, prefix)
        tau = prefix
        n_gt = count(keys_past > tau[None], keys_new > tau)
        need = float(top) - n_gt

        tri = _tri_incl()
        eq_past = jnp.where(keys_past == tau[None], 1.0, 0.0)
        pre_in = _dot(eq_past.reshape(n_pages * SUBLANES, LANES).astype(BF16), tri)
        pre_in = pre_in.reshape(n_pages, SUBLANES, LANES)
        carry = jnp.zeros((SUBLANES, 1), F32)
        for p in range(n_pages):
            e = eq_past[p]
            keep = (keys_past[p] > tau) | ((e > 0.0) & (pre_in[p] + carry <= need))
            keep = keep & (row8 < t_len)
            bias[p] = jnp.where(keep, 0.0, NEG_INF)
            carry = carry + jnp.sum(e, axis=1, keepdims=True)
        e = jnp.where(keys_new == tau, 1.0, 0.0)
        pre = _dot(e.astype(BF16), tri) + carry
        keep = (keys_new > tau) | ((e > 0.0) & (pre <= need))
        keep = keep & (row8 < t_len) & (lane8 <= row8)
        bias[n_pages] = jnp.where(keep, 0.0, NEG_INF)

        m_scr[...] = jnp.full((rows, LANES), NEG_INF, F32)
        l_scr[...] = jnp.zeros((rows, LANES), F32)
        acc_scr[...] = jnp.zeros((rows, W_B), F32)

    q = q_ref[0]
    col = _iota((SUBLANES, W_B), 1)
    hrow = _iota((SUBLANES, W_B), 0)
    blockmask = (col // D_B) == hrow
    qexp = jnp.concatenate(
        [jnp.where(blockmask, jnp.broadcast_to(q[t:t + 1], (SUBLANES, W_B)), 0.0) for t in range(t_len)],
        axis=0).astype(BF16)
    hvec = _iota((rows, LANES), 0) % H_B
    slope = jnp.exp2(-8.0 * (hvec.astype(F32) + 1.0) / H_B)
    tvec = _iota((rows, LANES), 0) // H_B
    qpos = past + tvec
    lane = _iota((rows, LANES), 1)

    def expand_bias(bp):
        return jnp.concatenate([jnp.broadcast_to(bp[t:t + 1], (SUBLANES, LANES)) for t in range(t_len)],
                               axis=0)

    def attend(k_list, v_list, bias_list, kpos0_list):
        logits = []
        for kp, bp, kpos0 in zip(k_list, bias_list, kpos0_list):
            s = _dot(qexp, kp.astype(BF16)) * (D_B ** -0.5)
            dist = (qpos - (kpos0 + lane)).astype(F32)
            logits.append(s - slope * dist + expand_bias(bp))
        m_old = m_scr[...]
        m_new = m_old
        for s in logits:
            m_new = jnp.maximum(m_new, jnp.max(s, axis=1, keepdims=True))
        m_safe = jnp.where(m_new == NEG_INF, 0.0, m_new)
        scale = jnp.exp(m_old - m_safe)
        l = l_scr[...] * scale
        acc = acc_scr[...] * scale[:, 0:1]
        for s, vp in zip(logits, v_list):
            p = jnp.exp(s - m_safe)
            l = l + jnp.sum(p, axis=1, keepdims=True)
            acc = acc + _dot_nt(p.astype(BF16), vp.astype(BF16))
        m_scr[...] = m_new
        l_scr[...] = l
        acc_scr[...] = acc

    @pl.when(s_idx < nsteps)
    def _():
        base = s_idx * PG_STEP
        attend([kpages[j][0, 0] for j in range(PG_STEP)],
               [vpages[j][0, 0] for j in range(PG_STEP)],
               [bias[base + j] for j in range(PG_STEP)],
               [(base + j) * PAGE_SIZE for j in range(PG_STEP)])

    @pl.when(s_idx == nsteps)
    def _():
        attend([knew_ref[0]], [vnew_ref[0]], [bias[n_pages]], [past])
        o = acc_scr[...] / l_scr[...][:, 0:1]
        outs = [jnp.sum(jnp.where(blockmask, o[t * H_B:(t + 1) * H_B], 0.0), axis=0, keepdims=True)
                for t in range(t_len)]
        o_ref[0] = _rows_to_tile(outs, W_B).astype(BF16)


def _dsa_sample_attend(layer, page_table, scores, q3, iq32, iwb, knew, vnew, iknew, cache_k4, cache_v4,
                       t_len):
    nb, n_pages = page_table.shape
    nsteps = n_pages // PG_STEP
    top = min(TOPK_MAX, (n_pages * PAGE_SIZE + t_len) // 4)
    rows = t_len * H_B

    def page_spec(j):
        def imap(b, s, pt):
            return (layer, pt[b, jnp.minimum(s, nsteps - 1) * PG_STEP + j], 0, 0)
        return pl.BlockSpec((1, 1, W_B, PAGE_SIZE), imap)

    per_b = lambda shape: pl.BlockSpec((1,) + shape, lambda b, s, pt: (b,) + (0,) * len(shape))
    grid_spec = pltpu.PrefetchScalarGridSpec(
        num_scalar_prefetch=1,
        grid=(nb, nsteps + 1),
        in_specs=[per_b((n_pages, SUBLANES, LANES)), per_b((t_len, W_B)), per_b((rows, D_I)),
                  per_b((rows, LANES)), per_b((W_B, PAGE_SIZE)), per_b((W_B, PAGE_SIZE)),
                  per_b((D_I, PAGE_SIZE))]
                 + [page_spec(j) for j in range(PG_STEP)] + [page_spec(j) for j in range(PG_STEP)],
        out_specs=per_b((SUBLANES, W_B)),
        scratch_shapes=[pltpu.VMEM((n_pages + 1, SUBLANES, LANES), F32),
                        pltpu.VMEM((rows, LANES), F32), pltpu.VMEM((rows, LANES), F32),
                        pltpu.VMEM((rows, W_B), F32)],
    )
    body = functools.partial(_dsa_s_attn_body, t_len=t_len, n_pages=n_pages, top=top)
    return pl.pallas_call(
        body,
        grid_spec=grid_spec,
        out_shape=jax.ShapeDtypeStruct((nb, SUBLANES, W_B), BF16),
        compiler_params=_cp(("arbitrary", "arbitrary")),
        name="dsa_sample_attend",
    )(page_table, scores, q3, iq32, iwb, knew, vnew, iknew,
      *([cache_k4] * PG_STEP), *([cache_v4] * PG_STEP))


def _head_block_ones(n):
    return jnp.where((_iota((n, n), 0) // D_C) == (_iota((n, n), 1) // D_C), 1.0, 0.0).astype(BF16)


def _rwkv_prep_body(*refs, tm, seq_len):
    pc_refs = refs[:PC_BLOCKS]
    (sh_ref, mu_ref, w0_ref, a0_ref, kk_ref, ka_ref, rk_ref, wwa_ref, g2_ref,
     r_ref, w_ref, k_ref, v_ref, na_ref, b_ref, g_ref, bonus_ref, carry) = refs[PC_BLOCKS:]
    pc = jnp.concatenate([r[...] for r in pc_refs], axis=1)
    rolled = pltpu.roll(pc, 1, 0)
    row = _iota(pc.shape, 0)
    if seq_len >= tm:
        tiles_per_seq = seq_len // tm
        first = (pl.program_id(0) % tiles_per_seq) == 0
        head_row = jnp.where(first, sh_ref[0], carry[...])
        prev = jnp.where(row == 0, head_row, rolled)
        carry[...] = pc[tm - 1:tm, :]
    else:
        prev = jnp.where((row % seq_len) == 0, sh_ref[0], rolled)
    pm = pc + (prev - pc) * mu_ref[...]
    r = pm[:, 0:W_C]
    k = pm[:, W_C:2 * W_C]
    v = pm[:, 2 * W_C:3 * W_C]
    lwa = pm[:, 3 * W_C:3 * W_C + LANES]
    lg = pm[:, 3 * W_C + LANES:3 * W_C + 2 * LANES]
    lane = _iota(lwa.shape, 1)
    lwa = jnp.where(lane < LORA_W, jnp.tanh(lwa), lwa)
    wa = _dot(lwa.astype(BF16), wwa_ref[...])
    x = -(w0_ref[...] + wa[:, 0:W_C])
    softplus = jnp.maximum(x, 0.0) + jnp.log1p(jnp.exp(-jnp.abs(x)))
    w_raw = -softplus - 0.5
    a = _sigmoid(a0_ref[...] + wa[:, W_C:2 * W_C])
    g = _dot(_sigmoid(lg).astype(BF16), g2_ref[...])
    ones = _head_block_ones(W_C)
    kk = k * kk_ref[...]
    ss = _dot((kk * kk).astype(BF16), ones)
    kk = kk / jnp.maximum(jnp.sqrt(ss), 1e-12)
    k2 = k * (1.0 + (a - 1.0) * ka_ref[...])
    r_ref[...] = r
    w_ref[...] = jnp.exp(-jnp.exp(w_raw))
    k_ref[...] = k2
    v_ref[...] = v
    na_ref[...] = -kk
    b_ref[...] = kk * a
    g_ref[...] = g
    bonus_ref[...] = _dot((r * k2 * rk_ref[...]).astype(BF16), ones) * v


def _rwkv_prep(p2, shift_rows, prm, tm, seq_len):
    n = p2.shape[0]
    out_rm = pl.BlockSpec((tm, W_C), lambda i: (i, 0))
    shape_rm = jax.ShapeDtypeStruct((n, W_C), F32)
    if seq_len >= tm:
        tiles_per_seq = seq_len // tm
        sh_spec = pl.BlockSpec((1, 1, D_C_IN), lambda i: (i // tiles_per_seq, 0, 0))
        out_scan = pl.BlockSpec((tm, W_C), lambda i: (i % tiles_per_seq, i // tiles_per_seq))
        shape_scan = jax.ShapeDtypeStruct((seq_len, (n // seq_len) * W_C), F32)
    else:
        sh_spec = pl.BlockSpec((1, tm, D_C_IN), lambda i: (i, 0, 0))
        out_scan, shape_scan = out_rm, shape_rm
    row = lambda w: pl.BlockSpec((1, w), lambda i: (0, 0))
    body = functools.partial(_rwkv_prep_body, tm=tm, seq_len=seq_len)
    pc_block0 = U_PC * LANES // PC_BLOCK_W

    def pc_spec(j):
        return pl.BlockSpec((tm, PC_BLOCK_W), lambda i: (i, pc_block0 + j))

    return pl.pallas_call(
        body,
        grid=(n // tm,),
        in_specs=[pc_spec(j) for j in range(PC_BLOCKS)]
                 + [sh_spec, row(D_C_IN), row(W_C), row(W_C), row(W_C), row(W_C), row(W_C),
                    pl.BlockSpec((LANES, 2 * W_C), lambda i: (0, 0)),
                    pl.BlockSpec((LORA_G, W_C), lambda i: (0, 0))],
        out_specs=[out_scan] * 6 + [out_rm] * 2,
        out_shape=[shape_scan] * 6 + [shape_rm] * 2,
        scratch_shapes=[pltpu.VMEM((1, D_C_IN), F32)],
        compiler_params=_cp(("arbitrary",)),
        name="rwkv_prep",
    )(*([p2] * PC_BLOCKS), shift_rows, prm["mu"], prm["w0"], prm["a0"], prm["kk"], prm["ka"], prm["rk"],
      prm["wwa"], prm["g2"])


SCAN_B = 8
SCAN_TB = 128


def _rwkv_scan_body(r_ref, w_ref, k_ref, v_ref, na_ref, b_ref, s0_ref, y_ref, sout_ref, s_scr, ybuf,
                    *, tb, ntb):
    tblk = pl.program_id(1)

    @pl.when(tblk == 0)
    def _():
        s_scr[...] = s0_ref[...]

    ones1 = _head_block_ones(LANES)
    ones2 = _head_block_ones(2 * LANES)
    vi = _iota((D_C, LANES), 0)
    li = _iota((D_C, LANES), 1)
    irep = jnp.where((li % D_C) == vi, 1.0, 0.0)
    ngrp = SCAN_B * (H_C // 2)

    def vec(x_row, g):
        return x_row[:, g * LANES:(g + 1) * LANES]

    def y_row(yb_all):
        return jnp.concatenate(
            [jnp.sum(yb_all[g * D_C:(g + 1) * D_C] * irep, axis=0, keepdims=True) for g in range(ngrp)], axis=1)

    def advance(w_t, k_t, v_t, na_t, b_t, r_p):
        s_old = [s_scr[g] for g in range(ngrp)]
        lhs1 = jnp.concatenate(
            [jnp.concatenate([(s_old[g] * vec(na_t, g)).astype(BF16), (s_old[g] * vec(r_p, g)).astype(BF16)],
                             axis=1) for g in range(ngrp)], axis=0)
        res1 = _dot(lhs1, ones2)
        lhs2 = jnp.concatenate(
            [jnp.concatenate([(irep * vec(v_t, g)).astype(BF16), (irep * vec(v_t, g + 1)).astype(BF16)], axis=1)
             for g in range(0, ngrp, 2)], axis=0)
        res2 = _dot(lhs2, ones2)
        for g in range(ngrp):
            sa = res1[g * D_C:(g + 1) * D_C, 0:LANES]
            pr, half = divmod(g, 2)
            vb = res2[pr * D_C:(pr + 1) * D_C, half * LANES:(half + 1) * LANES]
            s_scr[g] = s_old[g] * vec(w_t, g) + sa * vec(b_t, g) + vb * vec(k_t, g)
        return y_row(res1[:, LANES:2 * LANES])

    def final_y(r_last):
        lhs = jnp.concatenate([(s_scr[g] * vec(r_last, g)).astype(BF16) for g in range(ngrp)], axis=0)
        return y_row(_dot(lhs, ones1))

    refs = (w_ref, k_ref, v_ref, na_ref, b_ref)
    if tb % SUBLANES == 0:
        ybuf[...] = jnp.zeros(ybuf.shape, F32)

        def tile_step(t8, carry):
            base = pl.multiple_of(t8 * SUBLANES, SUBLANES)
            prev = pl.multiple_of(jnp.maximum(t8 - 1, 0) * SUBLANES, SUBLANES)
            tiles = [ref[pl.ds(base, SUBLANES), :] for ref in refs]
            r_tile = r_ref[pl.ds(base, SUBLANES), :]
            r_prev_tile = r_ref[pl.ds(prev, SUBLANES), :]
            for tt in range(SUBLANES):
                r_p = r_prev_tile[SUBLANES - 1:SUBLANES] if tt == 0 else r_tile[tt - 1:tt]
                y_p = advance(*[x[tt:tt + 1] for x in tiles], r_p)
                if tt == 0:
                    ybuf[SUBLANES - 1:SUBLANES, :] = y_p
                    y_ref[pl.ds(prev, SUBLANES), :] = ybuf[...]
                else:
                    ybuf[tt - 1:tt, :] = y_p
            return carry

        lax.fori_loop(0, tb // SUBLANES, tile_step, 0)
        ybuf[SUBLANES - 1:SUBLANES, :] = final_y(r_ref[tb - 1:tb, :])
        y_ref[tb - SUBLANES:tb, :] = ybuf[...]
    else:
        for t in range(tb):
            y_p = advance(*[ref[t:t + 1, :] for ref in refs], r_ref[max(t - 1, 0):max(t - 1, 0) + 1, :])
            if t > 0:
                y_ref[t - 1:t, :] = y_p
        y_ref[tb - 1:tb, :] = final_y(r_ref[tb - 1:tb, :])

    @pl.when(tblk == ntb - 1)
    def _():
        sout_ref[...] = s_scr[...]


def _rwkv_scan(r, w, k, v, na, b, s0, nb, t_len, tb):
    ntb = t_len // tb
    npair = H_C // 2
    blk = pl.BlockSpec((tb, SCAN_B * W_C), lambda c, t: (t, c))
    sblk = pl.BlockSpec((SCAN_B * npair, D_C, LANES), lambda c, t: (c, 0, 0))
    body = functools.partial(_rwkv_scan_body, tb=tb, ntb=ntb)
    return pl.pallas_call(
        body,
        grid=(nb // SCAN_B, ntb),
        in_specs=[blk] * 6 + [sblk],
        out_specs=[blk, sblk],
        out_shape=[jax.ShapeDtypeStruct((t_len, nb * W_C), F32),
                   jax.ShapeDtypeStruct((nb * npair, D_C, LANES), F32)],
        scratch_shapes=[pltpu.VMEM((SCAN_B * npair, D_C, LANES), F32),
                        pltpu.VMEM((SUBLANES, SCAN_B * W_C), F32)],
        compiler_params=_cp(("arbitrary", "arbitrary")),
        name="rwkv_scan",
    )(r, w, k, v, na, b, s0)


def _rwkv_post_body(y_ref, g_ref, bonus_ref, lg_ref, lb_ref, o_ref):
    y = y_ref[...]
    ones = _head_block_ones(W_C)
    mu = _dot(y.astype(BF16), ones) * (1.0 / D_C)
    d = y - mu
    var = _dot((d * d).astype(BF16), ones) * (1.0 / D_C)
    yn = d * lax.rsqrt(var + RWKV_GN_EPS) * lg_ref[...] + lb_ref[...]
    o_ref[...] = ((yn + bonus_ref[...]) * g_ref[...]).astype(BF16)


def _rwkv_post(y, g, bonus, lnx_g, lnx_b, tm, seq_len):
    n = g.shape[0]
    blk = pl.BlockSpec((tm, W_C), lambda i: (i, 0))
    row = pl.BlockSpec((1, W_C), lambda i: (0, 0))
    if seq_len >= tm:
        tiles_per_seq = seq_len // tm
        yblk = pl.BlockSpec((tm, W_C), lambda i: (i % tiles_per_seq, i // tiles_per_seq))
    else:
        yblk = blk
    return pl.pallas_call(
        _rwkv_post_body,
        grid=(n // tm,),
        in_specs=[yblk, blk, blk, row, row],
        out_specs=blk,
        out_shape=jax.ShapeDtypeStruct((n, W_C), BF16),
        compiler_params=_cp(("arbitrary",)),
        name="rwkv_post",
    )(y, g, bonus, lnx_g, lnx_b)


def _layer_norm(x, g, b):
    mu = jnp.mean(x, axis=-1, keepdims=True)
    var = jnp.mean(jnp.square(x - mu), axis=-1, keepdims=True)
    return (x - mu) * lax.rsqrt(var + LN_EPS) * g + b


def _merge_body(oa_ref, ob_ref, oc_ref, gla_ref, glb_ref, glc_ref, x_ref, wb_ref, wo_ref,
                gt_ref, sc_ref, sh_ref, lg_ref, lb_ref, x1_ref, h_ref):
    mixed = None
    for gidx, (o_ref, gl_ref) in enumerate(((oa_ref, gla_ref), (ob_ref, glb_ref), (oc_ref, glc_ref))):
        br = _dot(o_ref[...], wb_ref[gidx])
        term = _sigmoid(gl_ref[...]) * br
        mixed = term if mixed is None else mixed + term
    y = _dot(mixed.astype(BF16), wo_ref[...])
    x1 = _layer_norm(ALPHA * x_ref[...] + gt_ref[0] * y, lg_ref[...], lb_ref[...])
    x1_ref[...] = x1
    h_ref[...] = (x1 * (1.0 + sc_ref[0]) + sh_ref[0]).astype(BF16)


def _merge(oa, ob, oc, p2, x, wb, wo, gt, sc, sh, ln_g, ln_b, tm, tiles_per_group):
    n = x.shape[0]
    blk = lambda w: pl.BlockSpec((tm, w), lambda i: (i, 0))
    row = pl.BlockSpec((1, D_MODEL), lambda i: (0, 0))
    ms = lambda m: _mod_spec(m, tm, tiles_per_group)
    gl_block0 = U_GL * LANES // D_MODEL
    return pl.pallas_call(
        _merge_body,
        grid=(n // tm,),
        in_specs=[blk(BRANCH_W), blk(BRANCH_W), blk(BRANCH_W)]
                 + [pl.BlockSpec((tm, D_MODEL), lambda i, g=g: (i, gl_block0 + g)) for g in range(N_BRANCH)]
                 + [blk(D_MODEL),
                  pl.BlockSpec((N_BRANCH, BRANCH_W, D_MODEL), lambda i: (0, 0, 0)),
                  pl.BlockSpec((D_MODEL, D_MODEL), lambda i: (0, 0)),
                  ms(gt), ms(sc), ms(sh), row, row],
        out_specs=[blk(D_MODEL), blk(D_MODEL)],
        out_shape=[jax.ShapeDtypeStruct((n, D_MODEL), F32), jax.ShapeDtypeStruct((n, D_MODEL), BF16)],
        compiler_params=_cp(("arbitrary",)),
        name="merge",
    )(oa, ob, oc, p2, p2, p2, x, wb, wo, gt, sc, sh, ln_g, ln_b)


def _ffn_finish(acc, x_ref, gt_ref, sc_ref, sh_ref, lg_ref, lb_ref, x2_ref, h_ref):
    x2 = _layer_norm(ALPHA * x_ref[...] + gt_ref[0] * acc, lg_ref[...], lb_ref[...])
    x2_ref[...] = x2
    h_ref[...] = (x2 * (1.0 + sc_ref[0]) + sh_ref[0]).astype(BF16)


def _ffn_body(h_ref, x_ref, w1_ref, w3_ref, w2_ref, gt_ref, sc_ref, sh_ref, lg_ref, lb_ref,
              x2_ref, hn_ref, acc, *, nf):
    f = pl.program_id(1)
    h = h_ref[...]
    u = (_silu(_dot(h, w1_ref[...])) * _dot(h, w3_ref[...])).astype(BF16)
    part = _dot(u, w2_ref[...])

    @pl.when(f == 0)
    def _():
        acc[...] = part

    @pl.when(f > 0)
    def _():
        acc[...] = acc[...] + part

    @pl.when(f == nf - 1)
    def _():
        _ffn_finish(acc[...], x_ref, gt_ref, sc_ref, sh_ref, lg_ref, lb_ref, x2_ref, hn_ref)


def _ffn_dense(h, x, w1, w3, w2, gt, sc, sh, ln_g, ln_b, tm, tiles_per_group):
    n = x.shape[0]
    tf = D_FF // 2
    nf = D_FF // tf
    blk = pl.BlockSpec((tm, D_MODEL), lambda i, f: (i, 0))
    row = pl.BlockSpec((1, D_MODEL), lambda i, f: (0, 0))
    ms = lambda m: _mod_spec(m, tm, tiles_per_group)
    return pl.pallas_call(
        functools.partial(_ffn_body, nf=nf),
        grid=(n // tm, nf),
        in_specs=[blk, blk,
                  pl.BlockSpec((D_MODEL, tf), lambda i, f: (0, f)),
                  pl.BlockSpec((D_MODEL, tf), lambda i, f: (0, f)),
                  pl.BlockSpec((tf, D_MODEL), lambda i, f: (f, 0)),
                  ms(gt), ms(sc), ms(sh), row, row],
        out_specs=[blk, blk],
        out_shape=[jax.ShapeDtypeStruct((n, D_MODEL), F32), jax.ShapeDtypeStruct((n, D_MODEL), BF16)],
        scratch_shapes=[pltpu.VMEM((tm, D_MODEL), F32)],
        compiler_params=_cp(("arbitrary", "arbitrary")),
        name="ffn_dense",
    )(h, x, w1, w3, w2, gt, sc, sh, ln_g, ln_b)


def _moe_body(h_ref, x_ref, rw_ref, rb_ref, w1_ref, w3_ref, w2_ref, gt_ref, sc_ref, sh_ref, lg_ref, lb_ref,
              x2_ref, hn_ref, acc, gate):
    e = pl.program_id(1)
    h = h_ref[...]
    lane = _iota((h.shape[0], LANES), 1)

    @pl.when(e == 0)
    def _():
        logits = _dot(h, rw_ref[...]) + rb_ref[...]
        m1 = jnp.max(logits, axis=1, keepdims=True)
        i1 = jnp.min(jnp.where(logits == m1, lane, LANES), axis=1, keepdims=True)
        rest = jnp.where(lane == i1, NEG_INF, logits)
        m2 = jnp.max(rest, axis=1, keepdims=True)
        i2 = jnp.min(jnp.where(rest == m2, lane, LANES), axis=1, keepdims=True)
        e2 = jnp.exp(m2 - m1)
        p1 = 1.0 / (1.0 + e2)
        p2 = e2 / (1.0 + e2)
        gate[...] = jnp.where(lane == i1, p1, 0.0) + jnp.where(lane == i2, p2, 0.0)
        acc[...] = jnp.zeros(acc.shape, F32)

    ge = jnp.sum(jnp.where(lane == e, gate[...], 0.0), axis=1, keepdims=True)
    u = (_silu(_dot(h, w1_ref[0])) * _dot(h, w3_ref[0])).astype(BF16)
    acc[...] = acc[...] + ge * _dot(u, w2_ref[0])

    @pl.when(e == N_EXPERTS - 1)
    def _():
        _ffn_finish(acc[...], x_ref, gt_ref, sc_ref, sh_ref, lg_ref, lb_ref, x2_ref, hn_ref)


def _ffn_moe(h, x, rw, rb, w1, w3, w2, gt, sc, sh, ln_g, ln_b, tm, tiles_per_group):
    n = x.shape[0]
    blk = pl.BlockSpec((tm, D_MODEL), lambda i, e: (i, 0))
    row = pl.BlockSpec((1, D_MODEL), lambda i, e: (0, 0))
    ms = lambda m: _mod_spec(m, tm, tiles_per_group)
    return pl.pallas_call(
        _moe_body,
        grid=(n // tm, N_EXPERTS),
        in_specs=[blk, blk,
                  pl.BlockSpec((D_MODEL, LANES), lambda i, e: (0, 0)),
                  pl.BlockSpec((1, LANES), lambda i, e: (0, 0)),
                  pl.BlockSpec((1, D_MODEL, D_FF_E), lambda i, e: (e, 0, 0)),
                  pl.BlockSpec((1, D_MODEL, D_FF_E), lambda i, e: (e, 0, 0)),
                  pl.BlockSpec((1, D_FF_E, D_MODEL), lambda i, e: (e, 0, 0)),
                  ms(gt), ms(sc), ms(sh), row, row],
        out_specs=[blk, blk],
        out_shape=[jax.ShapeDtypeStruct((n, D_MODEL), F32), jax.ShapeDtypeStruct((n, D_MODEL), BF16)],
        scratch_shapes=[pltpu.VMEM((tm, D_MODEL), F32), pltpu.VMEM((tm, LANES), F32)],
        compiler_params=_cp(("arbitrary", "arbitrary")),
        name="ffn_moe",
    )(h, x, rw, rb, w1, w3, w2, gt, sc, sh, ln_g, ln_b)


def _pack_w_in(w_in):
    o_iw = 8 * 512
    o_ik = o_iw + H_I
    o_pc = o_ik + D_I
    o_gl = o_pc + D_C_IN
    z = lambda w: jnp.zeros(w_in.shape[:2] + (w,), w_in.dtype)
    packed = jnp.concatenate(
        [w_in[..., :o_iw], w_in[..., o_ik:o_pc], w_in[..., o_iw:o_ik], z(LANES - D_I - H_I), z(LANES),
         w_in[..., o_pc:o_gl], w_in[..., o_gl:]], axis=-1)
    assert packed.shape[-1] == D_PACK
    return packed.astype(BF16)


def _scan_state_in(s):
    b = s.shape[0]
    s = s.reshape(b, H_C // 2, 2, D_C, D_C).transpose(0, 1, 3, 2, 4)
    return s.reshape(b * (H_C // 2), D_C, 2 * D_C)


def _scan_state_out(s, b):
    s = s.reshape(b, H_C // 2, D_C, 2, D_C).transpose(0, 1, 3, 2, 4)
    return s.reshape(b, H_C, D_C, D_C)


def _mods(ada_l, lo, hi, per_row_t):
    parts = jnp.split(ada_l[lo:hi], 6, axis=-1)
    if per_row_t is None:
        return [p[:, None, :] for p in parts]
    return [jnp.repeat(p, per_row_t, axis=0)[None] for p in parts]


def _layer(l, x, h, mods, mods_next, prm, attend_fn, ret_s0, wkv_s0, shift_rows, nb, t_len):
    n = nb * t_len
    tm_proj = min(n, TM_PROJ)
    tm = min(n, TM_ROW)
    sh1, sc1, gt1, sh2, sc2, gt2 = mods
    p2 = _in_proj(h, prm["w_in"][l], tm_proj)

    rb = CHUNK_A if t_len % CHUNK_A == 0 else t_len
    nchunk = t_len // rb
    oa, ret_s = _retention(p2.reshape(nb * nchunk, rb, D_PACK), ret_s0, prm["lgam"], prm["ret_gn_g"][l],
                           nb, nchunk, rb)
    oa = oa.reshape(n, H_A * DV_A)

    ob = attend_fn(p2)

    rp = {k: prm["rw_" + k][l] for k in ("mu", "w0", "a0", "kk", "ka", "rk", "wwa", "g2")}
    r, w, k2, v, na, b, g, bonus = _rwkv_prep(p2, shift_rows, rp, tm, t_len)
    tb = min(t_len, SCAN_TB)
    if t_len >= tm:
        tmaj = lambda a: a
    else:
        tmaj = lambda a: a.reshape(nb, t_len, W_C).transpose(1, 0, 2).reshape(t_len, nb * W_C)
    y, wkv_s = _rwkv_scan(tmaj(r), tmaj(w), tmaj(k2), tmaj(v), tmaj(na), tmaj(b), wkv_s0, nb, t_len, tb)
    if t_len < tm:
        y = y.reshape(t_len, nb, W_C).transpose(1, 0, 2).reshape(n, W_C)
    oc = _rwkv_post(y, g, bonus, prm["rw_lnx_g"][l], prm["rw_lnx_b"][l], tm, t_len)

    tm_m = tm
    tpg_m = max(t_len // tm_m, 1)
    x1, h2 = _merge(oa, ob, oc, p2, x, prm["w_branch"][l], prm["w_out"][l], gt1, sc2, sh2,
                    prm["ln1_g"][l], prm["ln1_b"][l], tm_m, tpg_m)
    sh_n, sc_n = mods_next
    i = l // 2
    if l % 2 == 0:
        x2, hn = _ffn_dense(h2, x1, prm["ffn_w1"][i], prm["ffn_w3"][i], prm["ffn_w2"][i], gt2, sc_n, sh_n,
                            prm["ln2_g"][l], prm["ln2_b"][l], tm_m, tpg_m)
    else:
        x2, hn = _ffn_moe(h2, x1, prm["moe_router"][i], prm["moe_router_b"][i], prm["moe_w1"][i],
                          prm["moe_w3"][i], prm["moe_w2"][i], gt2, sc_n, sh_n,
                          prm["ln2_g"][l], prm["ln2_b"][l], tm_m, tpg_m)
    shift_s = p2.reshape(nb, t_len, D_PACK)[:, -1, U_PC * LANES:U_PC * LANES + D_C_IN]
    kh =p2[:, U_KB * LANES:U_KB * LANES + W_B].reshape(nb, t_len, H_B, D_B)
    vh = p2[:, U_VB * LANES:U_VB * LANES + W_B].reshape(nb, t_len, H_B, D_B)
    ik = p2[:, U_MISC * LANES:U_MISC * LANES + D_I].reshape(nb, t_len, D_I)
    return x2, hn, (kh, vh, ik, ret_s, _scan_state_out(wkv_s, nb), shift_s)


def kernel(x_prompt, x_sample, c_prompt, c_sample, cache_k, cache_v, cache_kidx, page_table, state_ret, state_wkv, state_shift, ada_w, ada_b, w_in, ret_gn_g, rw_mu, rw_w0, rw_w2, rw_a0, rw_a2, rw_g2, rw_kk, rw_ka, rw_rk, rw_lnx_g, rw_lnx_b, w_branch, w_out, ln1_g, ln1_b, ln2_g, ln2_b, ffn_w1, ffn_w3, ffn_w2, moe_router, moe_router_b, moe_w1, moe_w3, moe_w2):
    bp, tp, _ = x_prompt.shape
    bs, ts, _ = x_sample.shape
    n_pages = page_table.shape[1]
    n_pool = cache_k.shape[1]

    zw = jnp.zeros((DEPTH, LORA_W, W_C), F32)
    wwa = jnp.concatenate([jnp.concatenate([rw_w2, zw], axis=-1),
                           jnp.concatenate([zw, rw_a2], axis=-1)], axis=1).astype(BF16)
    n_moe = moe_router.shape[0]
    router_pad = jnp.concatenate(
        [moe_router, jnp.zeros((n_moe, D_MODEL, LANES - N_EXPERTS), F32)], axis=-1).astype(BF16)
    router_b_pad = jnp.concatenate(
        [moe_router_b, jnp.full((n_moe, LANES - N_EXPERTS), -1e30, F32)], axis=-1)[:, None, :]
    log_gamma = jnp.log1p(-jnp.exp2(-5.0 - jnp.arange(H_A, dtype=F32)))
    row = lambda a: a.reshape(DEPTH, 1, -1)
    prm = dict(
        w_in=_pack_w_in(w_in), lgam=jnp.repeat(log_gamma, DK_A)[None, :], ret_gn_g=row(ret_gn_g),
        rw_mu=row(rw_mu), rw_w0=row(rw_w0), rw_a0=row(rw_a0), rw_kk=row(rw_kk), rw_ka=row(rw_ka),
        rw_rk=row(rw_rk), rw_wwa=wwa, rw_g2=rw_g2.astype(BF16), rw_lnx_g=row(rw_lnx_g), rw_lnx_b=row(rw_lnx_b),
        w_branch=w_branch.astype(BF16), w_out=w_out.astype(BF16),
        ln1_g=row(ln1_g), ln1_b=row(ln1_b), ln2_g=row(ln2_g), ln2_b=row(ln2_b),
        ffn_w1=ffn_w1.astype(BF16), ffn_w3=ffn_w3.astype(BF16), ffn_w2=ffn_w2.astype(BF16),
        moe_router=router_pad, moe_router_b=router_b_pad,
        moe_w1=moe_w1.astype(BF16), moe_w3=moe_w3.astype(BF16), moe_w2=moe_w2.astype(BF16))

    ada = _ada_all(jnp.concatenate([c_prompt, c_sample], axis=0), ada_w.astype(BF16), ada_b)
    mods_p = [_mods(ada[l], 0, bp, None) for l in range(DEPTH)]
    mods_s = [_mods(ada[l], bp, bp + bs, ts) for l in range(DEPTH)]

    np_, ns_ = bp * tp, bs * ts
    tm_p = min(tp, TM_ROW)
    xp = x_prompt.reshape(np_, D_MODEL)
    xs = x_sample.reshape(ns_, D_MODEL)
    hp = _modulate(xp, mods_p[0][1], mods_p[0][0], tm_p, tp // tm_p)
    hs = _modulate(xs, mods_s[0][1], mods_s[0][0], ns_, 1)

    cache_k4 = cache_k.transpose(0, 1, 3, 4, 2).reshape(DEPTH, n_pool, W_B, PAGE_SIZE)
    cache_v4 = cache_v.transpose(0, 1, 3, 4, 2).reshape(DEPTH, n_pool, W_B, PAGE_SIZE)
    cache_kidx_t = cache_kidx.transpose(0, 1, 3, 2)
    zeros_ret = jnp.zeros((bp, H_A, DK_A, DV_A), F32)
    zeros_wkv = jnp.zeros((bp * (H_C // 2), D_C, 2 * D_C), F32)
    zeros_shift = jnp.zeros((bp, 1, D_C_IN), F32)

    st_p, st_s = [], []
    for l in range(DEPTH):
        nxt = min(l + 1, DEPTH - 1)
        attend_p = lambda p2: _dsa_prompt(p2, bp, tp)
        xp, hp, sp = _layer(l, xp, hp, mods_p[l], (mods_p[nxt][0], mods_p[nxt][1]), prm, attend_p,
                            zeros_ret, zeros_wkv, zeros_shift, bp, tp)

        def attend_s(p2, l=l):
            pad = lambda a: jnp.pad(a.reshape(bs, ts, -1).transpose(0, 2, 1),
                                    ((0, 0), (0, 0), (0, PAGE_SIZE - ts)))
            q3 = p2[:, U_QB * LANES:U_QB * LANES + W_B].reshape(bs, ts, W_B)
            iq32 = p2[:, U_IQ * LANES:U_IQ * LANES + W_B].reshape(bs, ts * H_I, D_I)
            iw = p2[:, U_MISC * LANES + IW_OFF:U_MISC * LANES + IW_OFF + H_I].reshape(bs, ts * H_I, 1)
            iwb = jnp.broadcast_to(iw, (bs, ts * H_I, LANES))
            knew = pad(p2[:, U_KB * LANES:U_KB * LANES + W_B])
            vnew = pad(p2[:, U_VB * LANES:U_VB * LANES + W_B])
            iknew = pad(p2[:, U_MISC * LANES:U_MISC * LANES + D_I])
            scores = _dsa_sample_scores(l, page_table, cache_kidx_t, iq32, iwb, ts)
            o = _dsa_sample_attend(l, page_table, scores, q3, iq32, iwb, knew, vnew, iknew,
                                   cache_k4, cache_v4, ts)
            return o[:, :ts].reshape(ns_, W_B)

        shift_rows = jnp.repeat(state_shift[l], ts, axis=0)[None]
        xs, hs, ss = _layer(l, xs, hs, mods_s[l], (mods_s[nxt][0], mods_s[nxt][1]), prm, attend_s,
                            state_ret[l], _scan_state_in(state_wkv[l]), shift_rows, bs, ts)
        st_p.append(sp)
        st_s.append(ss)

    outs = [xp.reshape(bp, tp, D_MODEL), xs.reshape(bs, ts, D_MODEL)]
    for j in range(6):
        outs.append(jnp.stack([s[j] for s in st_p]))
        outs.append(jnp.stack([s[j] for s in st_s]))
    return tuple(outs)
```

```python
import functools
import math

import jax
import jax.numpy as jnp
import numpy as np
from jax import lax
from jax.experimental import pallas as pl
from jax.experimental.pallas import tpu as pltpu

F32 = jnp.float32
BF16 = jnp.bfloat16
I32 = jnp.int32

D_MODEL = 1024
DEPTH = 4
PAGE_SIZE = 128
H_A, DK_A, DV_A = 4, 128, 128
CHUNK_A = 128
H_B, D_B = 8, 64
W_B = H_B * D_B
H_I, D_I = 8, 64
TOPK_MAX = 256
H_C, D_C = 8, 64
W_C = H_C * D_C
LORA_W, LORA_A, LORA_G = 64, 64, 128
RWKV_GN_EPS = 64e-5
N_BRANCH = 3
BRANCH_W = 512
D_FF = 2816
N_EXPERTS = 8
D_FF_E = 1408
ALPHA = (2.0 * DEPTH) ** 0.25
LN_EPS = 1e-5
D_C_IN = 3 * W_C + LORA_W + LORA_A + LORA_G

LANES = 128
SUBLANES = 8
VMEM_LIMIT = 56 * 1024 * 1024
TM_PROJ = 1024
TM_ROW = 512

U_QA, U_KA, U_VA, U_GA = 0, 4, 8, 12
U_QB, U_KB, U_VB, U_IQ = 16, 20, 24, 28
U_MISC = 32
U_PC = 34
U_GL = 48
N_UNITS = 72
PC_BLOCK_W = 2 * LANES
PC_BLOCKS = 7
D_PACK = N_UNITS * LANES
IW_OFF = 64

NEG_INF = float("-inf")


def _cp(sem, vmem=VMEM_LIMIT):
    return pltpu.CompilerParams(dimension_semantics=sem, vmem_limit_bytes=vmem)


def _dot(a, b):
    return jnp.dot(a, b, preferred_element_type=F32)


def _dot_nt(a, b):
    return lax.dot_general(a, b, (((1,), (1,)), ((), ())), preferred_element_type=F32)


def _dot_tn(a, b):
    return lax.dot_general(a, b, (((0,), (0,)), ((), ())), preferred_element_type=F32)


def _sigmoid(x):
    return 1.0 / (1.0 + jnp.exp(-x))


def _silu(x):
    return x * _sigmoid(x)


def _iota(shape, dim):
    return lax.broadcasted_iota(I32, shape, dim)


def _rows_to_tile(rows, width, fill=0.0):
    ri = _iota((SUBLANES, width), 0)
    out = jnp.full((SUBLANES, width), fill, F32)
    for t, r in enumerate(rows):
        out = jnp.where(ri == t, jnp.broadcast_to(r, (SUBLANES, width)), out)
    return out


def _ada_body(c_ref, w_ref, b_ref, o_ref):
    c = c_ref[...]
    o_ref[0] = _dot(_silu(c).astype(BF16), w_ref[0]) + b_ref[0]


def _ada_all(c_all, ada_w_bf, ada_b):
    m = c_all.shape[0]
    n = ada_w_bf.shape[-1]
    tn = 1024
    return pl.pallas_call(
        _ada_body,
        grid=(DEPTH, n // tn),
        in_specs=[pl.BlockSpec((m, D_MODEL), lambda l, j: (0, 0)),
                  pl.BlockSpec((1, D_MODEL, tn), lambda l, j: (l, 0, j)),
                  pl.BlockSpec((1, 1, tn), lambda l, j: (l, 0, j))],
        out_specs=pl.BlockSpec((1, m, tn), lambda l, j: (l, 0, j)),
        out_shape=jax.ShapeDtypeStruct((DEPTH, m, n), F32),
        compiler_params=_cp(("arbitrary", "arbitrary")),
        name="ada",
    )(c_all, ada_w_bf, ada_b.reshape(DEPTH, 1, n))


def _mod_spec(mod, tm, tiles_per_group):
    g, r, d = mod.shape
    if r == 1:
        return pl.BlockSpec((1, 1, d), lambda i, *_: (i // tiles_per_group, 0, 0))
    assert r == tm and g == 1
    return pl.BlockSpec((1, r, d), lambda i, *_: (0, 0, 0))


def _modulate_body(x_ref, sc_ref, sh_ref, h_ref):
    h_ref[...] = (x_ref[...] * (1.0 + sc_ref[0]) + sh_ref[0]).astype(BF16)


def _modulate(x, sc, sh, tm, tiles_per_group):
    n = x.shape[0]
    return pl.pallas_call(
        _modulate_body,
        grid=(n // tm,),
        in_specs=[pl.BlockSpec((tm, D_MODEL), lambda i: (i, 0)),
                  _mod_spec(sc, tm, tiles_per_group), _mod_spec(sh, tm, tiles_per_group)],
        out_specs=pl.BlockSpec((tm, D_MODEL), lambda i: (i, 0)),
        out_shape=jax.ShapeDtypeStruct((n, D_MODEL), BF16),
        compiler_params=_cp(("arbitrary",)),
        name="modulate",
    )(x, sc, sh)


def _mm_body(x_ref, w_ref, o_ref):
    o_ref[...] = _dot(x_ref[...], w_ref[...])


def _in_proj(h, w_pack, tm):
    n = h.shape[0]
    tn = 1024
    return pl.pallas_call(
        _mm_body,
        grid=(n // tm, D_PACK // tn),
        in_specs=[pl.BlockSpec((tm, D_MODEL), lambda i, j: (i, 0)),
                  pl.BlockSpec((D_MODEL, tn), lambda i, j: (0, j))],
        out_specs=pl.BlockSpec((tm, tn), lambda i, j: (i, j)),
        out_shape=jax.ShapeDtypeStruct((n, D_PACK), F32),
        compiler_params=_cp(("arbitrary", "arbitrary")),
        name="in_proj",
    )(h, w_pack)


def _retention_body(q_ref, k_ref, v_ref, g_ref, s0_ref, lg_ref, gn_ref, o_ref, sout_ref,
                    s_scr, pad_scr, *, rb, nchunk):
    n = pl.program_id(1)
    c = CHUNK_A

    @pl.when(n == 0)
    def _():
        s_scr[...] = s0_ref[0]

    def padded(ref, slot):
        if rb == c:
            return ref[0]
        pad_scr[slot] = jnp.zeros((c, H_A * DK_A), F32)
        pad_scr[slot, 0:rb, :] = ref[0]
        return pad_scr[slot]

    q_all = padded(q_ref, 0)
    k_all = padded(k_ref, 1)
    v_all = padded(v_ref, 2)
    g_all = padded(g_ref, 3)

    row = _iota((c, c), 0).astype(F32)
    col = _iota((c, c), 1).astype(F32)
    rel = row - col
    outs = []
    for h in range(H_A):
        sl = slice(h * DK_A, (h + 1) * DK_A)
        lg = lg_ref[0:1, sl]
        q = (q_all[:, sl] * (DK_A ** -0.5)).astype(BF16)
        k = k_all[:, sl]
        v = v_all[:, sl].astype(BF16)
        s0 = s_scr[h]
        decay = jnp.where(rel >= 0.0, jnp.exp(lg * jnp.maximum(rel, 0.0)), 0.0)
        scores = _dot_nt(q, k.astype(BF16)) * decay
        inner = _dot(scores.astype(BF16), v)
        xi = jnp.exp(lg * (row + 1.0))
        cross = _dot(q, s0.astype(BF16)) * xi
        zeta = jnp.where(row < float(rb), jnp.exp(lg * (float(rb) - 1.0 - row)), 0.0)
        s_new = jnp.exp(lg * float(rb)) * s0 + _dot_tn((k * zeta).astype(BF16), v)
        s_scr[h] = s_new
        o = inner + cross
        mu = jnp.mean(o, axis=-1, keepdims=True)
        var = jnp.mean(jnp.square(o - mu), axis=-1, keepdims=True)
        o = (o - mu) * lax.rsqrt(var + LN_EPS) * gn_ref[0:1, sl]
        outs.append(_silu(g_all[:, sl]) * o)
    o_full = jnp.concatenate(outs, axis=-1)
    o_ref[0] = o_full[0:rb].astype(BF16)

    @pl.when(n == nchunk - 1)
    def _():
        sout_ref[0] = s_scr[...]


def _retention(p3, s0, lgam, gn_g, nb, nchunk, rb):
    w = H_A * DK_A
    col = lambda u: (lambda b, n: (b * nchunk + n, 0, u // 4))
    body = functools.partial(_retention_body, rb=rb, nchunk=nchunk)
    return pl.pallas_call(
        body,
        grid=(nb, nchunk),
        in_specs=[pl.BlockSpec((1, rb, w), col(U_QA)), pl.BlockSpec((1, rb, w), col(U_KA)),
                  pl.BlockSpec((1, rb, w), col(U_VA)), pl.BlockSpec((1, rb, w), col(U_GA)),
                  pl.BlockSpec((1, H_A, DK_A, DV_A), lambda b, n: (b, 0, 0, 0)),
                  pl.BlockSpec((1, w), lambda b, n: (0, 0)),
                  pl.BlockSpec((1, w), lambda b, n: (0, 0))],
        out_specs=[pl.BlockSpec((1, rb, w), lambda b, n: (b * nchunk + n, 0, 0)),
                   pl.BlockSpec((1, H_A, DK_A, DV_A), lambda b, n: (b, 0, 0, 0))],
        out_shape=[jax.ShapeDtypeStruct((nb * nchunk, rb, w), BF16),
                   jax.ShapeDtypeStruct((nb, H_A, DK_A, DV_A), F32)],
        scratch_shapes=[pltpu.VMEM((H_A, DK_A, DV_A), F32), pltpu.VMEM((4, CHUNK_A, w), F32)],
        compiler_params=_cp(("arbitrary", "arbitrary")),
        name="retention",
    )(p3, p3, p3, p3, s0, lgam, gn_g)


def _score_key(s):
    bits = pltpu.bitcast(s, I32)
    return jnp.where(bits < 0, bits ^ jnp.int32(0x7FFFFFFF), bits)


def _tri_incl():
    return jnp.where(_iota((LANES, LANES), 0) <= _iota((LANES, LANES), 1), 1.0, 0.0).astype(BF16)


KC = 512
DSA_QB = 256


def _dsa_prompt_body(q_ref, iq_ref, mq_ref, k_ref, v_ref, mk_ref, o_ref,
                     kx, vt, ikbf, keys, bias, q2, *, t_len, top):
    i = pl.program_id(1)
    nkc = t_len // KC
    qb = min(DSA_QB, t_len)

    @pl.when(i == 0)
    def _():
        prow = _iota((KC, LANES), 0)
        plane = _iota((KC, LANES), 1)
        for j in range(nkc):
            kpos = j * KC + prow
            posf = jnp.where(plane == 0, kpos // 64, jnp.where(plane == 1, kpos % 64,
                                                                jnp.where(plane == 2, 1, 0)))
            posf = posf.astype(F32).astype(BF16)
            for hp in range(H_B // 2):
                kx[j, :, hp * 2 * LANES:hp * 2 * LANES + LANES] = (
                    k_ref[j * KC:(j + 1) * KC, hp * LANES:(hp + 1) * LANES].astype(BF16))
                kx[j, :, hp * 2 * LANES + LANES:(hp + 1) * 2 * LANES] = posf
                vt[j, hp * LANES:(hp + 1) * LANES, :] = (
                    v_ref[j * KC:(j + 1) * KC, hp * LANES:(hp + 1) * LANES].T.astype(BF16))
            ikbf[j] = mk_ref[j * KC:(j + 1) * KC, :].astype(BF16)

    nsc = (i * qb) // KC + 1
    qpos = i * qb + _iota((KC, qb), 1)
    krow = _iota((KC, qb), 0)

    iq = (iq_ref[...] * (D_I ** -0.5)).astype(BF16)
    iw_t = mq_ref[...].T * (H_I ** -0.5)

    def score_chunk(j, carry):
        ik = ikbf[j][:, 0:D_I]
        acc = jnp.zeros((KC, qb), F32)
        for h in range(H_I):
            s = _dot_nt(ik, iq[:, h * D_I:(h + 1) * D_I])
            acc = acc + jnp.maximum(s, 0.0) * iw_t[IW_OFF + h:IW_OFF + h + 1]
        acc = jnp.where(j * KC + krow <= qpos, acc + 0.0, NEG_INF)
        keys[j] = _score_key(acc)
        return carry

    lax.fori_loop(0, nsc, score_chunk, 0)

    def count(pred):
        fold = 4 * SUBLANES

        def body(j, acc):
            m = jnp.where(pred(keys[j]), 1.0, 0.0).reshape(KC // fold, fold, qb)
            return acc + jnp.sum(m, axis=0)
        acc = lax.fori_loop(0, nsc, body, jnp.zeros((fold, qb), F32))
        return jnp.sum(acc, axis=0, keepdims=True)

    def count_ge(cand):
        return count(lambda kj: kj >= cand)

    lowest = jnp.full((1, qb), -2 ** 31, I32)

    def bisect():
        prefix = lowest
        for bit in range(31, -1, -1):
            step = jnp.int32(-2 ** 31) if bit == 31 else jnp.int32(1 << bit)
            cand = prefix + step
            cnt = count_ge(cand)
            prefix = jnp.where(cnt >= float(top), cand, prefix)
        return prefix

    neg_inf_key = _score_key(jnp.full((1, qb), NEG_INF, F32))
    tau = lax.cond((i + 1) * qb <= top, lambda: neg_inf_key, bisect)
    need = float(top) - count(lambda kj: kj > tau)

    tri_l = jnp.where(_iota((KC, KC), 1) <= _iota((KC, KC), 0), 1.0, 0.0).astype(BF16)

    def bias_chunk(j, carry):
        kj = keys[j]
        eq = jnp.where(kj == tau, 1.0, 0.0)
        pre = _dot(tri_l, eq.astype(BF16)) + carry
        keep = ((kj > tau) | ((eq > 0.0) & (pre <= need))) & (j * KC + krow <= qpos)
        bias[j] = jnp.where(keep, 0.0, NEG_INF)
        return carry + jnp.sum(eq, axis=0, keepdims=True)

    lax.fori_loop(0, nsc, bias_chunk, jnp.zeros((1, qb), F32))

    npair = H_B // 2
    q_all = q_ref[...]
    lane_q = _iota((qb, LANES), 1)
    base = (i * qb + 0 * lane_q).astype(F32)
    for hp in range(npair):
        qp = q_all[:, hp * LANES:(hp + 1) * LANES] * (D_B ** -0.5)
        for h2 in range(2):
            slope = 2.0 ** (-8.0 * (2 * hp + h2 + 1) / H_B)
            in_half = (lane_q >= h2 * D_B) & (lane_q < (h2 + 1) * D_B)
            feat = jnp.where(lane_q == 0, 64.0 * slope,
                             jnp.where(lane_q == 1, slope, jnp.where(lane_q == 2, -slope * base, 0.0)))
            q2[hp, h2 * qb:(h2 + 1) * qb, 0:LANES] = jnp.where(in_half, qp, 0.0).astype(BF16)
            q2[hp, h2 * qb:(h2 + 1) * qb, LANES:2 * LANES] = feat.astype(BF16)

    def attn_chunk(j, carry):
        ms, ls, accs = carry
        b2 = jnp.concatenate([bias[j], bias[j]], axis=1)
        out_m, out_l, out_acc = [], [], []
        for hp in range(npair):
            s = _dot_nt(kx[j, :, hp * 2 * LANES:(hp + 1) * 2 * LANES], q2[hp]) + b2
            m_new = jnp.maximum(ms[hp], jnp.max(s, axis=0, keepdims=True))
            m_safe = jnp.where(m_new == NEG_INF, 0.0, m_new)
            alpha = jnp.exp(ms[hp] - m_safe)
            p = jnp.exp(s - m_safe)
            out_m.append(m_new)
            out_l.append(ls[hp] * alpha + jnp.sum(p, axis=0, keepdims=True))
            out_acc.append(accs[hp] * alpha + _dot(vt[j, hp * LANES:(hp + 1) * LANES, :], p.astype(BF16)))
        return out_m, out_l, out_acc

    init = ([jnp.full((1, 2 * qb), NEG_INF, F32)] * npair, [jnp.zeros((1, 2 * qb), F32)] * npair,
            [jnp.zeros((LANES, 2 * qb), F32)] * npair)
    _, ls, accs = lax.fori_loop(0, nsc, attn_chunk, init)
    row_o = _iota((LANES, qb), 0)
    outs = []
    for hp in range(npair):
        o = accs[hp] / ls[hp]
        outs.append(jnp.where(row_o < D_B, o[:, 0:qb], o[:, qb:2 * qb]).T)
    o_ref[...] = jnp.concatenate(outs, axis=1).astype(BF16)


def _dsa_prompt(p2, nb, t_len):
    top = min(TOPK_MAX, t_len // 4)
    qb = min(DSA_QB, t_len)
    nq = t_len // qb
    nkc = t_len // KC
    body = functools.partial(_dsa_prompt_body, t_len=t_len, top=top)
    qspec = lambda u, wdt: pl.BlockSpec((qb, wdt), lambda b, i: (b * nq + i, u * LANES // wdt))
    aspec = lambda u, wdt: pl.BlockSpec((t_len, wdt), lambda b, i: (b, u * LANES // wdt))
    return pl.pallas_call(
        body,
        grid=(nb, nq),
        in_specs=[qspec(U_QB, W_B), qspec(U_IQ, W_B), qspec(U_MISC, LANES),
                  aspec(U_KB, W_B), aspec(U_VB, W_B), aspec(U_MISC, LANES)],
        out_specs=pl.BlockSpec((qb, W_B), lambda b, i: (b * nq + i, 0)),
        out_shape=jax.ShapeDtypeStruct((nb * t_len, W_B), BF16),
        scratch_shapes=[pltpu.VMEM((nkc, KC, 2 * W_B), BF16), pltpu.VMEM((nkc, W_B, KC), BF16),
                        pltpu.VMEM((nkc, KC, LANES), BF16),
                        pltpu.VMEM((nkc, KC, qb), I32), pltpu.VMEM((nkc, KC, qb), F32),
                        pltpu.VMEM((H_B // 2, 2 * qb, 2 * LANES), BF16)],
        compiler_params=_cp(("arbitrary", "arbitrary")),
        name="dsa_prompt",
    )(p2, p2, p2, p2, p2, p2)


PG_STEP = 32


def _dsa_s_score_body(pt_ref, iq_ref, iwb_ref, iknew_ref, *rest, t_len):
    pages = rest[:PG_STEP]
    o_ref, onew_ref = rest[PG_STEP:PG_STEP + 2]
    iq = (iq_ref[0] * (D_I ** -0.5)).astype(BF16)
    iwb = iwb_ref[0] * (H_I ** -0.5)
    row = _iota((SUBLANES, LANES), 0)
    lane = _iota((SUBLANES, LANES), 1)

    def scores(ikt):
        s = jnp.maximum(_dot(iq, ikt.astype(BF16)), 0.0) * iwb
        per_t = [jnp.sum(s[t * H_I:(t + 1) * H_I], axis=0, keepdims=True) for t in range(t_len)]
        return _rows_to_tile(per_t, LANES) + 0.0

    for j in range(PG_STEP):
        o_ref[0, j] = jnp.where(row < t_len, scores(pages[j][0, 0]), NEG_INF)

    @pl.when(pl.program_id(1) == 0)
    def _():
        onew_ref[0, 0] = jnp.where((row < t_len) & (lane <= row), scores(iknew_ref[0]), NEG_INF)


def _dsa_sample_scores(layer, page_table, cache_kidx_t, iq32, iwb, iknew, t_len):
    nb, n_pages = page_table.shape
    nsteps = n_pages // PG_STEP
    rows = t_len * H_I

    def page_spec(j):
        return pl.BlockSpec((1, 1, D_I, PAGE_SIZE),
                            lambda b, s, pt: (layer, pt[b, s * PG_STEP + j], 0, 0))

    grid_spec = pltpu.PrefetchScalarGridSpec(
        num_scalar_prefetch=1,
        grid=(nb, nsteps),
        in_specs=[pl.BlockSpec((1, rows, D_I), lambda b, s, pt: (b, 0, 0)),
                  pl.BlockSpec((1, rows, LANES), lambda b, s, pt: (b, 0, 0)),
                  pl.BlockSpec((1, D_I, PAGE_SIZE), lambda b, s, pt: (b, 0, 0))]
                 + [page_spec(j) for j in range(PG_STEP)],
        out_specs=[pl.BlockSpec((1, PG_STEP, SUBLANES, LANES), lambda b, s, pt: (b, s, 0, 0)),
                   pl.BlockSpec((1, 1, SUBLANES, LANES), lambda b, s, pt: (b, 0, 0, 0))],
    )
    return pl.pallas_call(
        functools.partial(_dsa_s_score_body, t_len=t_len),
        grid_spec=grid_spec,
        out_shape=[jax.ShapeDtypeStruct((nb, n_pages, SUBLANES, LANES), F32),
                   jax.ShapeDtypeStruct((nb, 1, SUBLANES, LANES), F32)],
        compiler_params=_cp(("arbitrary", "arbitrary")),
        name="dsa_sample_scores",
    )(page_table, iq32, iwb, iknew, *([cache_kidx_t] * PG_STEP))


SEL_B = 8


def _dsa_s_select_body(sc_ref, scn_ref, bias_ref, biasn_ref, *, t_len, n_pages, top):
    row8 = _iota((SUBLANES, LANES), 0)
    lane8 = _iota((SUBLANES, LANES), 1)
    keys_p = [_score_key(sc_ref[b]) for b in range(SEL_B)]
    keys_n = [_score_key(scn_ref[b, 0]) for b in range(SEL_B)]

    def count(pred_past, pred_new):
        c = jnp.sum(jnp.where(pred_past, 1.0, 0.0), axis=0) + jnp.where(pred_new, 1.0, 0.0)
        return jnp.sum(c, axis=1, keepdims=True)

    prefix = [jnp.full((SUBLANES, 1), -2 ** 31, I32)] * SEL_B
    for bit in range(31, -1, -1):
        step = jnp.int32(-2 ** 31) if bit == 31 else jnp.int32(1 << bit)
        nxt = []
        for b in range(SEL_B):
            cand = prefix[b] + step
            cnt = count(keys_p[b] >= cand[None], keys_n[b] >= cand)
            nxt.append(jnp.where(cnt >= float(top), cand, prefix[b]))
        prefix = nxt

    tri = _tri_incl()
    ones = jnp.ones((LANES, LANES), BF16)
    for b in range(SEL_B):
        tau = prefix[b]
        need = float(top) - count(keys_p[b] > tau[None], keys_n[b] > tau)
        eq = jnp.where(keys_p[b] == tau[None], 1.0, 0.0)
        eq2 = eq.reshape(n_pages * SUBLANES, LANES).astype(BF16)
        pre_in = _dot(eq2, tri).reshape(n_pages, SUBLANES, LANES)
        totals = _dot(eq2, ones).reshape(n_pages, SUBLANES, LANES)
        carry = jnp.zeros((SUBLANES, LANES), F32)
        for p in range(n_pages):
            keep = (keys_p[b][p] > tau) | ((eq[p] > 0.0) & (pre_in[p] + carry <= need))
            bias_ref[b, p] = jnp.where(keep & (row8 < t_len), 0.0, NEG_INF)
            carry = carry + totals[p]
        e = jnp.where(keys_n[b] == tau, 1.0, 0.0)
        pre = _dot(e.astype(BF16), tri) + carry
        keep = (keys_n[b] > tau) | ((e > 0.0) & (pre <= need))
        biasn_ref[b, 0] = jnp.where(keep & (row8 < t_len) & (lane8 <= row8), 0.0, NEG_INF)


def _dsa_sample_select(scores, scores_new, t_len):
    nb, n_pages = scores.shape[:2]
    top = min(TOPK_MAX, (n_pages * PAGE_SIZE + t_len) // 4)
    past = pl.BlockSpec((SEL_B, n_pages, SUBLANES, LANES), lambda c: (c, 0, 0, 0))
    new = pl.BlockSpec((SEL_B, 1, SUBLANES, LANES), lambda c: (c, 0, 0, 0))
    return pl.pallas_call(
        functools.partial(_dsa_s_select_body, t_len=t_len, n_pages=n_pages, top=top),
        grid=(nb // SEL_B,),
        in_specs=[past, new],
        out_specs=[past, new],
        out_shape=[jax.ShapeDtypeStruct(scores.shape, F32), jax.ShapeDtypeStruct(scores_new.shape, F32)],
        compiler_params=_cp(("arbitrary",)),
        name="dsa_sample_select",
    )(scores, scores_new)


def _dsa_s_attn_body(pt_ref, bias_ref, biasn_ref, q_ref, knew_ref, vnew_ref, *rest, t_len, n_pages):
    kpages = rest[:PG_STEP]
    vpages = rest[PG_STEP:2 * PG_STEP]
    o_ref = rest[2 * PG_STEP]
    m_scr, l_scr, acc_scr = rest[2 * PG_STEP + 1:]
    s_idx = pl.program_id(1)
    nsteps = n_pages // PG_STEP
    past = n_pages * PAGE_SIZE
    rows = t_len * H_B

    @pl.when(s_idx == 0)
    def _():
        m_scr[...] = jnp.full((rows, LANES), NEG_INF, F32)
        l_scr[...] = jnp.zeros((rows, LANES), F32)
        acc_scr[...] = jnp.zeros((rows, W_B), F32)

    q = q_ref[0]
    col = _iota((SUBLANES, W_B), 1)
    hrow = _iota((SUBLANES, W_B), 0)
    blockmask = (col // D_B) == hrow
    qexp = jnp.concatenate(
        [jnp.where(blockmask, jnp.broadcast_to(q[t:t + 1], (SUBLANES, W_B)), 0.0) for t in range(t_len)],
        axis=0).astype(BF16)
    hvec = _iota((rows, LANES), 0) % H_B
    slope = jnp.exp2(-8.0 * (hvec.astype(F32) + 1.0) / H_B)
    tvec = _iota((rows, LANES), 0) // H_B
    qpos = past + tvec
    lane = _iota((rows, LANES), 1)

    def expand_bias(bp):
        return jnp.concatenate([jnp.broadcast_to(bp[t:t + 1], (SUBLANES, LANES)) for t in range(t_len)],
                               axis=0)

    def attend(k_list, v_list, bias_list, kpos0_list):
        logits = []
        for kp, bp, kpos0 in zip(k_list, bias_list, kpos0_list):
            s = _dot(qexp, kp.astype(BF16)) * (D_B ** -0.5)
            dist = (qpos - (kpos0 + lane)).astype(F32)
            logits.append(s - slope * dist + expand_bias(bp))
        m_old = m_scr[...]
        m_new = m_old
        for s in logits:
            m_new = jnp.maximum(m_new, jnp.max(s, axis=1, keepdims=True))
        m_safe = jnp.where(m_new == NEG_INF, 0.0, m_new)
        scale = jnp.exp(m_old - m_safe)
        l = l_scr[...] * scale
        acc = acc_scr[...] * scale[:, 0:1]
        for s, vp in zip(logits, v_list):
            p = jnp.exp(s - m_safe)
            l = l + jnp.sum(p, axis=1, keepdims=True)
            acc = acc + _dot_nt(p.astype(BF16), vp.astype(BF16))
        m_scr[...] = m_new
        l_scr[...] = l
        acc_scr[...] = acc

    @pl.when(s_idx < nsteps)
    def _():
        base = s_idx * PG_STEP
        attend([kpages[j][0, 0] for j in range(PG_STEP)],
               [vpages[j][0, 0] for j in range(PG_STEP)],
               [bias_ref[0, base + j] for j in range(PG_STEP)],
               [(base + j) * PAGE_SIZE for j in range(PG_STEP)])

    @pl.when(s_idx == nsteps)
    def _():
        attend([knew_ref[0]], [vnew_ref[0]], [biasn_ref[0, 0]], [past])
        o = acc_scr[...] / l_scr[...][:, 0:1]
        outs = [jnp.sum(jnp.where(blockmask, o[t * H_B:(t + 1) * H_B], 0.0), axis=0, keepdims=True)
                for t in range(t_len)]
        o_ref[0] = _rows_to_tile(outs, W_B).astype(BF16)


def _dsa_sample_attend(layer, page_table, bias, bias_new, q3, knew, vnew, cache_k4, cache_v4, t_len):
    nb, n_pages = page_table.shape
    nsteps = n_pages // PG_STEP
    rows = t_len * H_B

    def page_spec(j):
        def imap(b, s, pt):
            return (layer, pt[b, jnp.minimum(s, nsteps - 1) * PG_STEP + j], 0, 0)
        return pl.BlockSpec((1, 1, W_B, PAGE_SIZE), imap)

    per_b = lambda shape: pl.BlockSpec((1,) + shape, lambda b, s, pt: (b,) + (0,) * len(shape))
    grid_spec = pltpu.PrefetchScalarGridSpec(
        num_scalar_prefetch=1,
        grid=(nb, nsteps + 1),
        in_specs=[per_b((n_pages, SUBLANES, LANES)), per_b((1, SUBLANES, LANES)), per_b((t_len, W_B)),
                  per_b((W_B, PAGE_SIZE)), per_b((W_B, PAGE_SIZE))]
                 + [page_spec(j) for j in range(PG_STEP)] + [page_spec(j) for j in range(PG_STEP)],
        out_specs=per_b((SUBLANES, W_B)),
        scratch_shapes=[pltpu.VMEM((rows, LANES), F32), pltpu.VMEM((rows, LANES), F32),
                        pltpu.VMEM((rows, W_B), F32)],
    )
    body = functools.partial(_dsa_s_attn_body, t_len=t_len, n_pages=n_pages)
    return pl.pallas_call(
        body,
        grid_spec=grid_spec,
        out_shape=jax.ShapeDtypeStruct((nb, SUBLANES, W_B), BF16),
        compiler_params=_cp(("arbitrary", "arbitrary")),
        name="dsa_sample_attend",
    )(page_table, bias, bias_new, q3, knew, vnew, *([cache_k4] * PG_STEP), *([cache_v4] * PG_STEP))


def _head_block_ones(n):
    return jnp.where((_iota((n, n), 0) // D_C) == (_iota((n, n), 1) // D_C), 1.0, 0.0).astype(BF16)


def _rwkv_prep_body(*refs, tm, seq_len):
    pc_refs = refs[:PC_BLOCKS]
    (sh_ref, mu_ref, w0_ref, a0_ref, kk_ref, ka_ref, rk_ref, wwa_ref, g2_ref,
     r_ref, w_ref, k_ref, v_ref, na_ref, b_ref, g_ref, bonus_ref, carry) = refs[PC_BLOCKS:]
    pc = jnp.concatenate([r[...] for r in pc_refs], axis=1)
    rolled = pltpu.roll(pc, 1, 0)
    row = _iota(pc.shape, 0)
    if seq_len >= tm:
        tiles_per_seq = seq_len // tm
        first = (pl.program_id(0) % tiles_per_seq) == 0
        head_row = jnp.where(first, sh_ref[0], carry[...])
        prev = jnp.where(row == 0, head_row, rolled)
        carry[...] = pc[tm - 1:tm, :]
    else:
        prev = jnp.where((row % seq_len) == 0, sh_ref[0], rolled)
    pm = pc + (prev - pc) * mu_ref[...]
    r = pm[:, 0:W_C]
    k = pm[:, W_C:2 * W_C]
    v = pm[:, 2 * W_C:3 * W_C]
    lwa = pm[:, 3 * W_C:3 * W_C + LANES]
    lg = pm[:, 3 * W_C + LANES:3 * W_C + 2 * LANES]
    lane = _iota(lwa.shape, 1)
    lwa = jnp.where(lane < LORA_W, jnp.tanh(lwa), lwa)
    wa = _dot(lwa.astype(BF16), wwa_ref[...])
    x = -(w0_ref[...] + wa[:, 0:W_C])
    softplus = jnp.maximum(x, 0.0) + jnp.log1p(jnp.exp(-jnp.abs(x)))
    w_raw = -softplus - 0.5
    a = _sigmoid(a0_ref[...] + wa[:, W_C:2 * W_C])
    g = _dot(_sigmoid(lg).astype(BF16), g2_ref[...])
    ones = _head_block_ones(W_C)
    kk = k * kk_ref[...]
    ss = _dot((kk * kk).astype(BF16), ones)
    kk = kk / jnp.maximum(jnp.sqrt(ss), 1e-12)
    k2 = k * (1.0 + (a - 1.0) * ka_ref[...])
    r_ref[...] = r
    w_ref[...] = jnp.exp(-jnp.exp(w_raw))
    k_ref[...] = k2
    v_ref[...] = v
    na_ref[...] = -kk
    b_ref[...] = kk * a
    g_ref[...] = g
    bonus_ref[...] = _dot((r * k2 * rk_ref[...]).astype(BF16), ones) * v


def _rwkv_prep(p2, shift_rows, prm, tm, seq_len):
    n = p2.shape[0]
    out_rm = pl.BlockSpec((tm, W_C), lambda i: (i, 0))
    shape_rm = jax.ShapeDtypeStruct((n, W_C), F32)
    if seq_len >= tm:
        tiles_per_seq = seq_len // tm
        sh_spec = pl.BlockSpec((1, 1, D_C_IN), lambda i: (i // tiles_per_seq, 0, 0))
        out_scan = pl.BlockSpec((tm, W_C), lambda i: (i % tiles_per_seq, i // tiles_per_seq))
        shape_scan = jax.ShapeDtypeStruct((seq_len, (n // seq_len) * W_C), F32)
    else:
        sh_spec = pl.BlockSpec((1, tm, D_C_IN), lambda i: (i, 0, 0))
        out_scan, shape_scan = out_rm, shape_rm
    row = lambda w: pl.BlockSpec((1, w), lambda i: (0, 0))
    body = functools.partial(_rwkv_prep_body, tm=tm, seq_len=seq_len)
    pc_block0 = U_PC * LANES // PC_BLOCK_W

    def pc_spec(j):
        return pl.BlockSpec((tm, PC_BLOCK_W), lambda i: (i, pc_block0 + j))

    return pl.pallas_call(
        body,
        grid=(n // tm,),
        in_specs=[pc_spec(j) for j in range(PC_BLOCKS)]
                 + [sh_spec, row(D_C_IN), row(W_C), row(W_C), row(W_C), row(W_C), row(W_C),
                    pl.BlockSpec((LANES, 2 * W_C), lambda i: (0, 0)),
                    pl.BlockSpec((LORA_G, W_C), lambda i: (0, 0))],
        out_specs=[out_scan] * 6 + [out_rm] * 2,
        out_shape=[shape_scan] * 6 + [shape_rm] * 2,
        scratch_shapes=[pltpu.VMEM((1, D_C_IN), F32)],
        compiler_params=_cp(("arbitrary",)),
        name="rwkv_prep",
    )(*([p2] * PC_BLOCKS), shift_rows, prm["mu"], prm["w0"], prm["a0"], prm["kk"], prm["ka"], prm["rk"],
      prm["wwa"], prm["g2"])


SCAN_B = 8
SCAN_TB = 128


def _rwkv_scan_body(r_ref, w_ref, k_ref, v_ref, na_ref, b_ref, s0_ref, y_ref, sout_ref, s_scr, ybuf,
                    *, tb, ntb):
    tblk = pl.program_id(1)

    @pl.when(tblk == 0)
    def _():
        s_scr[...] = s0_ref[...]

    ones1 = _head_block_ones(LANES)
    ones2 = _head_block_ones(2 * LANES)
    vi = _iota((D_C, LANES), 0)
    li = _iota((D_C, LANES), 1)
    irep = jnp.where((li % D_C) == vi, 1.0, 0.0)
    ngrp = SCAN_B * (H_C // 2)

    def vec(x_row, g):
        return x_row[:, g * LANES:(g + 1) * LANES]

    def y_row(yb_all):
        return jnp.concatenate(
            [jnp.sum(yb_all[g * D_C:(g + 1) * D_C] * irep, axis=0, keepdims=True) for g in range(ngrp)], axis=1)

    def advance(w_t, k_t, v_t, na_t, b_t, r_p):
        s_old = [s_scr[g] for g in range(ngrp)]
        na_b, r_b = na_t.astype(BF16), r_p.astype(BF16)
        s_b = [s.astype(BF16) for s in s_old]
        lhs1 = jnp.concatenate(
            [jnp.concatenate([s_b[g] * vec(na_b, g), s_b[g] * vec(r_b, g)], axis=1) for g in range(ngrp)],
            axis=0)
        res1 = _dot(lhs1, ones2)
        lhs2 = jnp.concatenate(
            [jnp.concatenate([(irep * vec(v_t, g)).astype(BF16), (irep * vec(v_t, g + 1)).astype(BF16)], axis=1)
             for g in range(0, ngrp, 2)], axis=0)
        res2 = _dot(lhs2, ones2)
        for g in range(ngrp):
            sa = res1[g * D_C:(g + 1) * D_C, 0:LANES]
            pr, half = divmod(g, 2)
            vb = res2[pr * D_C:(pr + 1) * D_C, half * LANES:(half + 1) * LANES]
            s_scr[g] = s_old[g] * vec(w_t, g) + sa * vec(b_t, g) + vb * vec(k_t, g)
        return y_row(res1[:, LANES:2 * LANES])

    def final_y(r_last):
        lhs = jnp.concatenate([(s_scr[g] * vec(r_last, g)).astype(BF16) for g in range(ngrp)], axis=0)
        return y_row(_dot(lhs, ones1))

    refs = (w_ref, k_ref, v_ref, na_ref, b_ref)
    if tb % SUBLANES == 0:
        ybuf[...] = jnp.zeros(ybuf.shape, F32)

        def tile_step(t8, carry):
            base = pl.multiple_of(t8 * SUBLANES, SUBLANES)
            prev = pl.multiple_of(jnp.maximum(t8 - 1, 0) * SUBLANES, SUBLANES)
            tiles = [ref[pl.ds(base, SUBLANES), :] for ref in refs]
            r_tile = r_ref[pl.ds(base, SUBLANES), :]
            r_prev_tile = r_ref[pl.ds(prev, SUBLANES), :]
            for tt in range(SUBLANES):
                r_p = r_prev_tile[SUBLANES - 1:SUBLANES] if tt == 0 else r_tile[tt - 1:tt]
                y_p = advance(*[x[tt:tt + 1] for x in tiles], r_p)
                if tt == 0:
                    ybuf[SUBLANES - 1:SUBLANES, :] = y_p
                    y_ref[pl.ds(prev, SUBLANES), :] = ybuf[...]
                else:
                    ybuf[tt - 1:tt, :] = y_p
            return carry

        lax.fori_loop(0, tb // SUBLANES, tile_step, 0)
        ybuf[SUBLANES - 1:SUBLANES, :] = final_y(r_ref[tb - 1:tb, :])
        y_ref[tb - SUBLANES:tb, :] = ybuf[...]
    else:
        for t in range(tb):
            y_p = advance(*[ref[t:t + 1, :] for ref in refs], r_ref[max(t - 1, 0):max(t - 1, 0) + 1, :])
            if t > 0:
                y_ref[t - 1:t, :] = y_p
        y_ref[tb - 1:tb, :] = final_y(r_ref[tb - 1:tb, :])

    @pl.when(tblk == ntb - 1)
    def _():
        sout_ref[...] = s_scr[...]


def _rwkv_scan(r, w, k, v, na, b, s0, nb, t_len, tb):
    ntb = t_len // tb
    npair = H_C // 2
    blk = pl.BlockSpec((tb, SCAN_B * W_C), lambda c, t: (t, c))
    sblk = pl.BlockSpec((SCAN_B * npair, D_C, LANES), lambda c, t: (c, 0, 0))
    body = functools.partial(_rwkv_scan_body, tb=tb, ntb=ntb)
    return pl.pallas_call(
        body,
        grid=(nb // SCAN_B, ntb),
        in_specs=[blk] * 6 + [sblk],
        out_specs=[blk, sblk],
        out_shape=[jax.ShapeDtypeStruct((t_len, nb * W_C), F32),
                   jax.ShapeDtypeStruct((nb * npair, D_C, LANES), F32)],
        scratch_shapes=[pltpu.VMEM((SCAN_B * npair, D_C, LANES), F32),
                        pltpu.VMEM((SUBLANES, SCAN_B * W_C), F32)],
        compiler_params=_cp(("arbitrary", "arbitrary")),
        name="rwkv_scan",
    )(r, w, k, v, na, b, s0)


def _rwkv_post_body(y_ref, g_ref, bonus_ref, lg_ref, lb_ref, o_ref):
    y = y_ref[...]
    ones = _head_block_ones(W_C)
    mu = _dot(y.astype(BF16), ones) * (1.0 / D_C)
    d = y - mu
    var = _dot((d * d).astype(BF16), ones) * (1.0 / D_C)
    yn = d * lax.rsqrt(var + RWKV_GN_EPS) * lg_ref[...] + lb_ref[...]
    o_ref[...] = ((yn + bonus_ref[...]) * g_ref[...]).astype(BF16)


def _rwkv_post(y, g, bonus, lnx_g, lnx_b, tm, seq_len):
    n = g.shape[0]
    blk = pl.BlockSpec((tm, W_C), lambda i: (i, 0))
    row = pl.BlockSpec((1, W_C), lambda i: (0, 0))
    if seq_len >= tm:
        tiles_per_seq = seq_len // tm
        yblk = pl.BlockSpec((tm, W_C), lambda i: (i % tiles_per_seq, i // tiles_per_seq))
    else:
        yblk = blk
    return pl.pallas_call(
        _rwkv_post_body,
        grid=(n // tm,),
        in_specs=[yblk, blk, blk, row, row],
        out_specs=blk,
        out_shape=jax.ShapeDtypeStruct((n, W_C), BF16),
        compiler_params=_cp(("arbitrary",)),
        name="rwkv_post",
    )(y, g, bonus, lnx_g, lnx_b)


def _layer_norm(x, g, b):
    mu = jnp.mean(x, axis=-1, keepdims=True)
    var = jnp.mean(jnp.square(x - mu), axis=-1, keepdims=True)
    return (x - mu) * lax.rsqrt(var + LN_EPS) * g + b


def _merge_body(oa_ref, ob_ref, oc_ref, gla_ref, glb_ref, glc_ref, x_ref, wb_ref, wo_ref,
                gt_ref, sc_ref, sh_ref, lg_ref, lb_ref, x1_ref, h_ref):
    mixed = None
    for gidx, (o_ref, gl_ref) in enumerate(((oa_ref, gla_ref), (ob_ref, glb_ref), (oc_ref, glc_ref))):
        br = _dot(o_ref[...], wb_ref[gidx])
        term = _sigmoid(gl_ref[...]) * br
        mixed = term if mixed is None else mixed + term
    y = _dot(mixed.astype(BF16), wo_ref[...])
    x1 = _layer_norm(ALPHA * x_ref[...] + gt_ref[0] * y, lg_ref[...], lb_ref[...])
    x1_ref[...] = x1
    h_ref[...] = (x1 * (1.0 + sc_ref[0]) + sh_ref[0]).astype(BF16)


def _merge(oa, ob, oc, p2, x, wb, wo, gt, sc, sh, ln_g, ln_b, tm, tiles_per_group):
    n = x.shape[0]
    blk = lambda w: pl.BlockSpec((tm, w), lambda i: (i, 0))
    row = pl.BlockSpec((1, D_MODEL), lambda i: (0, 0))
    ms = lambda m: _mod_spec(m, tm, tiles_per_group)
    gl_block0 = U_GL * LANES // D_MODEL
    return pl.pallas_call(
        _merge_body,
        grid=(n // tm,),
        in_specs=[blk(BRANCH_W), blk(BRANCH_W), blk(BRANCH_W)]
                 + [pl.BlockSpec((tm, D_MODEL), lambda i, g=g: (i, gl_block0 + g)) for g in range(N_BRANCH)]
                 + [blk(D_MODEL),
                  pl.BlockSpec((N_BRANCH, BRANCH_W, D_MODEL), lambda i: (0, 0, 0)),
                  pl.BlockSpec((D_MODEL, D_MODEL), lambda i: (0, 0)),
                  ms(gt), ms(sc), ms(sh), row, row],
        out_specs=[blk(D_MODEL), blk(D_MODEL)],
        out_shape=[jax.ShapeDtypeStruct((n, D_MODEL), F32), jax.ShapeDtypeStruct((n, D_MODEL), BF16)],
        compiler_params=_cp(("arbitrary",)),
        name="merge",
    )(oa, ob, oc, p2, p2, p2, x, wb, wo, gt, sc, sh, ln_g, ln_b)


def _ffn_finish(acc, x_ref, gt_ref, sc_ref, sh_ref, lg_ref, lb_ref, x2_ref, h_ref):
    x2 = _layer_norm(ALPHA * x_ref[...] + gt_ref[0] * acc, lg_ref[...], lb_ref[...])
    x2_ref[...] = x2
    h_ref[...] = (x2 * (1.0 + sc_ref[0]) + sh_ref[0]).astype(BF16)


def _ffn_body(h_ref, x_ref, w1_ref, w3_ref, w2_ref, gt_ref, sc_ref, sh_ref, lg_ref, lb_ref,
              x2_ref, hn_ref, acc, *, nf):
    f = pl.program_id(1)
    h = h_ref[...]
    u = (_silu(_dot(h, w1_ref[...])) * _dot(h, w3_ref[...])).astype(BF16)
    part = _dot(u, w2_ref[...])

    @pl.when(f == 0)
    def _():
        acc[...] = part

    @pl.when(f > 0)
    def _():
        acc[...] = acc[...] + part

    @pl.when(f == nf - 1)
    def _():
        _ffn_finish(acc[...], x_ref, gt_ref, sc_ref, sh_ref, lg_ref, lb_ref, x2_ref, hn_ref)


def _ffn_dense(h, x, w1, w3, w2, gt, sc, sh, ln_g, ln_b, tm, tiles_per_group):
    n = x.shape[0]
    tf = D_FF // 2
    nf = D_FF // tf
    blk = pl.BlockSpec((tm, D_MODEL), lambda i, f: (i, 0))
    row = pl.BlockSpec((1, D_MODEL), lambda i, f: (0, 0))
    ms = lambda m: _mod_spec(m, tm, tiles_per_group)
    return pl.pallas_call(
        functools.partial(_ffn_body, nf=nf),
        grid=(n // tm, nf),
        in_specs=[blk, blk,
                  pl.BlockSpec((D_MODEL, tf), lambda i, f: (0, f)),
                  pl.BlockSpec((D_MODEL, tf), lambda i, f: (0, f)),
                  pl.BlockSpec((tf, D_MODEL), lambda i, f: (f, 0)),
                  ms(gt), ms(sc), ms(sh), row, row],
        out_specs=[blk, blk],
        out_shape=[jax.ShapeDtypeStruct((n, D_MODEL), F32), jax.ShapeDtypeStruct((n, D_MODEL), BF16)],
        scratch_shapes=[pltpu.VMEM((tm, D_MODEL), F32)],
        compiler_params=_cp(("arbitrary", "arbitrary")),
        name="ffn_dense",
    )(h, x, w1, w3, w2, gt, sc, sh, ln_g, ln_b)


def _moe_body(h_ref, x_ref, rw_ref, rb_ref, w1_ref, w3_ref, w2_ref, gt_ref, sc_ref, sh_ref, lg_ref, lb_ref,
              x2_ref, hn_ref, acc, gate):
    e = pl.program_id(1)
    h = h_ref[...]
    lane = _iota((h.shape[0], LANES), 1)

    @pl.when(e == 0)
    def _():
        logits = _dot(h, rw_ref[...]) + rb_ref[...]
        m1 = jnp.max(logits, axis=1, keepdims=True)
        i1 = jnp.min(jnp.where(logits == m1, lane, LANES), axis=1, keepdims=True)
        rest = jnp.where(lane == i1, NEG_INF, logits)
        m2 = jnp.max(rest, axis=1, keepdims=True)
        i2 = jnp.min(jnp.where(rest == m2, lane, LANES), axis=1, keepdims=True)
        e2 = jnp.exp(m2 - m1)
        p1 = 1.0 / (1.0 + e2)
        p2 = e2 / (1.0 + e2)
        gate[...] = jnp.where(lane == i1, p1, 0.0) + jnp.where(lane == i2, p2, 0.0)
        acc[...] = jnp.zeros(acc.shape, F32)

    ge = jnp.sum(jnp.where(lane == e, gate[...], 0.0), axis=1, keepdims=True)
    u = (_silu(_dot(h, w1_ref[0])) * _dot(h, w3_ref[0])).astype(BF16)
    acc[...] = acc[...] + ge * _dot(u, w2_ref[0])

    @pl.when(e == N_EXPERTS - 1)
    def _():
        _ffn_finish(acc[...], x_ref, gt_ref, sc_ref, sh_ref, lg_ref, lb_ref, x2_ref, hn_ref)


def _ffn_moe(h, x, rw, rb, w1, w3, w2, gt, sc, sh, ln_g, ln_b, tm, tiles_per_group):
    n = x.shape[0]
    blk = pl.BlockSpec((tm, D_MODEL), lambda i, e: (i, 0))
    row = pl.BlockSpec((1, D_MODEL), lambda i, e: (0, 0))
    ms = lambda m: _mod_spec(m, tm, tiles_per_group)
    return pl.pallas_call(
        _moe_body,
        grid=(n // tm, N_EXPERTS),
        in_specs=[blk, blk,
                  pl.BlockSpec((D_MODEL, LANES), lambda i, e: (0, 0)),
                  pl.BlockSpec((1, LANES), lambda i, e: (0, 0)),
                  pl.BlockSpec((1, D_MODEL, D_FF_E), lambda i, e: (e, 0, 0)),
                  pl.BlockSpec((1, D_MODEL, D_FF_E), lambda i, e: (e, 0, 0)),
                  pl.BlockSpec((1, D_FF_E, D_MODEL), lambda i, e: (e, 0, 0)),
                  ms(gt), ms(sc), ms(sh), row, row],
        out_specs=[blk, blk],
        out_shape=[jax.ShapeDtypeStruct((n, D_MODEL), F32), jax.ShapeDtypeStruct((n, D_MODEL), BF16)],
        scratch_shapes=[pltpu.VMEM((tm, D_MODEL), F32), pltpu.VMEM((tm, LANES), F32)],
        compiler_params=_cp(("arbitrary", "arbitrary")),
        name="ffn_moe",
    )(h, x, rw, rb, w1, w3, w2, gt, sc, sh, ln_g, ln_b)


def _pack_w_in(w_in):
    o_iw = 8 * 512
    o_ik = o_iw + H_I
    o_pc = o_ik + D_I
    o_gl = o_pc + D_C_IN
    z = lambda w: jnp.zeros(w_in.shape[:2] + (w,), w_in.dtype)
    packed = jnp.concatenate(
        [w_in[..., :o_iw], w_in[..., o_ik:o_pc], w_in[..., o_iw:o_ik], z(LANES - D_I - H_I), z(LANES),
         w_in[..., o_pc:o_gl], w_in[..., o_gl:]], axis=-1)
    assert packed.shape[-1] == D_PACK
    return packed.astype(BF16)


def _scan_state_in(s):
    b = s.shape[0]
    s = s.reshape(b, H_C // 2, 2, D_C, D_C).transpose(0, 1, 3, 2, 4)
    return s.reshape(b * (H_C // 2), D_C, 2 * D_C)


def _scan_state_out(s, b):
    s = s.reshape(b, H_C // 2, D_C, 2, D_C).transpose(0, 1, 3, 2, 4)
    return s.reshape(b, H_C, D_C, D_C)


def _mods(ada_l, lo, hi, per_row_t):
    parts = jnp.split(ada_l[lo:hi], 6, axis=-1)
    if per_row_t is None:
        return [p[:, None, :] for p in parts]
    return [jnp.repeat(p, per_row_t, axis=0)[None] for p in parts]


def _layer(l, x, h, mods, mods_next, prm, attend_fn, ret_s0, wkv_s0, shift_rows, nb, t_len):
    n = nb * t_len
    tm_proj = min(n, TM_PROJ)
    tm = min(n, TM_ROW)
    sh1, sc1, gt1, sh2, sc2, gt2 = mods
    p2 = _in_proj(h, prm["w_in"][l], tm_proj)

    rb = CHUNK_A if t_len % CHUNK_A == 0 else t_len
    nchunk = t_len // rb
    oa, ret_s = _retention(p2.reshape(nb * nchunk, rb, D_PACK), ret_s0, prm["lgam"], prm["ret_gn_g"][l],
                           nb, nchunk, rb)
    oa = oa.reshape(n, H_A * DV_A)

    ob = attend_fn(p2)

    rp = {k: prm["rw_" + k][l] for k in ("mu", "w0", "a0", "kk", "ka", "rk", "wwa", "g2")}
    r, w, k2, v, na, b, g, bonus = _rwkv_prep(p2, shift_rows, rp, tm, t_len)
    tb = min(t_len, SCAN_TB)
    if t_len >= tm:
        tmaj = lambda a: a
    else:
        tmaj = lambda a: a.reshape(nb, t_len, W_C).transpose(1, 0, 2).reshape(t_len, nb * W_C)
    y, wkv_s = _rwkv_scan(tmaj(r), tmaj(w), tmaj(k2), tmaj(v), tmaj(na), tmaj(b), wkv_s0, nb, t_len, tb)
    if t_len < tm:
        y = y.reshape(t_len, nb, W_C).transpose(1, 0, 2).reshape(n, W_C)
    oc = _rwkv_post(y, g, bonus, prm["rw_lnx_g"][l], prm["rw_lnx_b"][l], tm, t_len)

    tm_m = tm
    tpg_m = max(t_len // tm_m, 1)
    x1, h2 = _merge(oa, ob, oc, p2, x, prm["w_branch"][l], prm["w_out"][l], gt1, sc2, sh2,
                    prm["ln1_g"][l], prm["ln1_b"][l], tm_m, tpg_m)
    sh_n, sc_n = mods_next
    i = l // 2
    if l % 2 == 0:
        x2, hn = _ffn_dense(h2, x1, prm["ffn_w1"][i], prm["ffn_w3"][i], prm["ffn_w2"][i], gt2, sc_n, sh_n,
                            prm["ln2_g"][l], prm["ln2_b"][l], tm_m, tpg_m)
    else:
        x2, hn = _ffn_moe(h2, x1, prm["moe_router"][i], prm["moe_router_b"][i], prm["moe_w1"][i],
                          prm["moe_w3"][i], prm["moe_w2"][i], gt2, sc_n, sh_n,
                          prm["ln2_g"][l], prm["ln2_b"][l], tm_m, tpg_m)
    shift_s = p2.reshape(nb, t_len, D_PACK)[:, -1, U_PC * LANES:U_PC * LANES + D_C_IN]
    kh =p2[:, U_KB * LANES:U_KB * LANES + W_B].reshape(nb, t_len, H_B, D_B)
    vh = p2[:, U_VB * LANES:U_VB * LANES + W_B].reshape(nb, t_len, H_B, D_B)
    ik = p2[:, U_MISC * LANES:U_MISC * LANES + D_I].reshape(nb, t_len, D_I)
    return x2, hn, (kh, vh, ik, ret_s, _scan_state_out(wkv_s, nb), shift_s)


def kernel(x_prompt, x_sample, c_prompt, c_sample, cache_k, cache_v, cache_kidx, page_table, state_ret, state_wkv, state_shift, ada_w, ada_b, w_in, ret_gn_g, rw_mu, rw_w0, rw_w2, rw_a0, rw_a2, rw_g2, rw_kk, rw_ka, rw_rk, rw_lnx_g, rw_lnx_b, w_branch, w_out, ln1_g, ln1_b, ln2_g, ln2_b, ffn_w1, ffn_w3, ffn_w2, moe_router, moe_router_b, moe_w1, moe_w3, moe_w2):
    bp, tp, _ = x_prompt.shape
    bs, ts, _ = x_sample.shape
    n_pages = page_table.shape[1]
    n_pool = cache_k.shape[1]

    zw = jnp.zeros((DEPTH, LORA_W, W_C), F32)
    wwa = jnp.concatenate([jnp.concatenate([rw_w2, zw], axis=-1),
                           jnp.concatenate([zw, rw_a2], axis=-1)], axis=1).astype(BF16)
    n_moe = moe_router.shape[0]
    router_pad = jnp.concatenate(
        [moe_router, jnp.zeros((n_moe, D_MODEL, LANES - N_EXPERTS), F32)], axis=-1).astype(BF16)
    router_b_pad = jnp.concatenate(
        [moe_router_b, jnp.full((n_moe, LANES - N_EXPERTS), -1e30, F32)], axis=-1)[:, None, :]
    log_gamma = jnp.log1p(-jnp.exp2(-5.0 - jnp.arange(H_A, dtype=F32)))
    row = lambda a: a.reshape(DEPTH, 1, -1)
    prm = dict(
        w_in=_pack_w_in(w_in), lgam=jnp.repeat(log_gamma, DK_A)[None, :], ret_gn_g=row(ret_gn_g),
        rw_mu=row(rw_mu), rw_w0=row(rw_w0), rw_a0=row(rw_a0), rw_kk=row(rw_kk), rw_ka=row(rw_ka),
        rw_rk=row(rw_rk), rw_wwa=wwa, rw_g2=rw_g2.astype(BF16), rw_lnx_g=row(rw_lnx_g), rw_lnx_b=row(rw_lnx_b),
        w_branch=w_branch.astype(BF16), w_out=w_out.astype(BF16),
        ln1_g=row(ln1_g), ln1_b=row(ln1_b), ln2_g=row(ln2_g), ln2_b=row(ln2_b),
        ffn_w1=ffn_w1.astype(BF16), ffn_w3=ffn_w3.astype(BF16), ffn_w2=ffn_w2.astype(BF16),
        moe_router=router_pad, moe_router_b=router_b_pad,
        moe_w1=moe_w1.astype(BF16), moe_w3=moe_w3.astype(BF16), moe_w2=moe_w2.astype(BF16))

    ada = _ada_all(jnp.concatenate([c_prompt, c_sample], axis=0), ada_w.astype(BF16), ada_b)
    mods_p = [_mods(ada[l], 0, bp, None) for l in range(DEPTH)]
    mods_s = [_mods(ada[l], bp, bp + bs, ts) for l in range(DEPTH)]

    np_, ns_ = bp * tp, bs * ts
    tm_p = min(tp, TM_ROW)
    xp = x_prompt.reshape(np_, D_MODEL)
    xs = x_sample.reshape(ns_, D_MODEL)
    hp = _modulate(xp, mods_p[0][1], mods_p[0][0], tm_p, tp // tm_p)
    hs = _modulate(xs, mods_s[0][1], mods_s[0][0], ns_, 1)

    cache_k4 = cache_k.transpose(0, 1, 3, 4, 2).reshape(DEPTH, n_pool, W_B, PAGE_SIZE)
    cache_v4 = cache_v.transpose(0, 1, 3, 4, 2).reshape(DEPTH, n_pool, W_B, PAGE_SIZE)
    cache_kidx_t = cache_kidx.transpose(0, 1, 3, 2)
    zeros_ret = jnp.zeros((bp, H_A, DK_A, DV_A), F32)
    zeros_wkv = jnp.zeros((bp * (H_C // 2), D_C, 2 * D_C), F32)
    zeros_shift = jnp.zeros((bp, 1, D_C_IN), F32)

    st_p, st_s = [], []
    for l in range(DEPTH):
        nxt = min(l + 1, DEPTH - 1)
        attend_p = lambda p2: _dsa_prompt(p2, bp, tp)
        xp, hp, sp = _layer(l, xp, hp, mods_p[l], (mods_p[nxt][0], mods_p[nxt][1]), prm, attend_p,
                            zeros_ret, zeros_wkv, zeros_shift, bp, tp)

        def attend_s(p2, l=l):
            pad = lambda a: jnp.pad(a.reshape(bs, ts, -1).transpose(0, 2, 1),
                                    ((0, 0), (0, 0), (0, PAGE_SIZE - ts)))
            q3 = p2[:, U_QB * LANES:U_QB * LANES + W_B].reshape(bs, ts, W_B)
            iq32 = p2[:, U_IQ * LANES:U_IQ * LANES + W_B].reshape(bs, ts * H_I, D_I)
            iw = p2[:, U_MISC * LANES + IW_OFF:U_MISC * LANES + IW_OFF + H_I].reshape(bs, ts * H_I, 1)
            iwb = jnp.broadcast_to(iw, (bs, ts * H_I, LANES))
            knew = pad(p2[:, U_KB * LANES:U_KB * LANES + W_B])
            vnew = pad(p2[:, U_VB * LANES:U_VB * LANES + W_B])
            iknew = pad(p2[:, U_MISC * LANES:U_MISC * LANES + D_I])
            scores, scores_new = _dsa_sample_scores(l, page_table, cache_kidx_t, iq32, iwb, iknew, ts)
            bias, bias_new = _dsa_sample_select(scores, scores_new, ts)
            o = _dsa_sample_attend(l, page_table, bias, bias_new, q3, knew, vnew, cache_k4, cache_v4, ts)
            return o[:, :ts].reshape(ns_, W_B)

        shift_rows = jnp.repeat(state_shift[l], ts, axis=0)[None]
        xs, hs, ss = _layer(l, xs, hs, mods_s[l], (mods_s[nxt][0], mods_s[nxt][1]), prm, attend_s,
                            state_ret[l], _scan_state_in(state_wkv[l]), shift_rows, bs, ts)
        st_p.append(sp)
        st_s.append(ss)

    outs = [xp.reshape(bp, tp, D_MODEL), xs.reshape(bs, ts, D_MODEL)]
    for j in range(6):
        outs.append(jnp.stack([s[j] for s in st_p]))
        outs.append(jnp.stack([s[j] for s in st_s]))
    return tuple(outs)
```

```python
import functools
import math

import jax
import jax.numpy as jnp
import numpy as np
from jax import lax
from jax.experimental import pallas as pl
from jax.experimental.pallas import tpu as pltpu

F32 = jnp.float32
BF16 = jnp.bfloat16
I32 = jnp.int32

D_MODEL = 1024
DEPTH = 4
PAGE_SIZE = 128
H_A, DK_A, DV_A = 4, 128, 128
CHUNK_A = 128
H_B, D_B = 8, 64
W_B = H_B * D_B
H_I, D_I = 8, 64
TOPK_MAX = 256
H_C, D_C = 8, 64
W_C = H_C * D_C
LORA_W, LORA_A, LORA_G = 64, 64, 128
RWKV_GN_EPS = 64e-5
N_BRANCH = 3
BRANCH_W = 512
D_FF = 2816
N_EXPERTS = 8
D_FF_E = 1408
ALPHA = (2.0 * DEPTH) ** 0.25
LN_EPS = 1e-5
D_C_IN = 3 * W_C + LORA_W + LORA_A + LORA_G

LANES = 128
SUBLANES = 8
VMEM_LIMIT = 56 * 1024 * 1024
TM_PROJ = 1024
TM_ROW = 512

U_QA, U_KA, U_VA, U_GA = 0, 4, 8, 12
U_QB, U_KB, U_VB, U_IQ = 16, 20, 24, 28
U_MISC = 32
U_PC = 34
U_GL = 48
N_UNITS = 72
PC_BLOCK_W = 2 * LANES
PC_BLOCKS = 7
D_PACK = N_UNITS * LANES
IW_OFF = 64

NEG_INF = float("-inf")


def _cp(sem, vmem=VMEM_LIMIT):
    return pltpu.CompilerParams(dimension_semantics=sem, vmem_limit_bytes=vmem)


def _dot(a, b):
    return jnp.dot(a, b, preferred_element_type=F32)


def _dot_nt(a, b):
    return lax.dot_general(a, b, (((1,), (1,)), ((), ())), preferred_element_type=F32)


def _dot_tn(a, b):
    return lax.dot_general(a, b, (((0,), (0,)), ((), ())), preferred_element_type=F32)


def _sigmoid(x):
    return 1.0 / (1.0 + jnp.exp(-x))


def _silu(x):
    return x * _sigmoid(x)


def _iota(shape, dim):
    return lax.broadcasted_iota(I32, shape, dim)


def _rows_to_tile(rows, width, fill=0.0):
    ri = _iota((SUBLANES, width), 0)
    out = jnp.full((SUBLANES, width), fill, F32)
    for t, r in enumerate(rows):
        out = jnp.where(ri == t, jnp.broadcast_to(r, (SUBLANES, width)), out)
    return out


def _ada_body(c_ref, w_ref, b_ref, o_ref):
    c = c_ref[...]
    o_ref[0] = _dot(_silu(c).astype(BF16), w_ref[0]) + b_ref[0]


def _ada_all(c_all, ada_w_bf, ada_b):
    m = c_all.shape[0]
    n = ada_w_bf.shape[-1]
    tn = 1024
    return pl.pallas_call(
        _ada_body,
        grid=(DEPTH, n // tn),
        in_specs=[pl.BlockSpec((m, D_MODEL), lambda l, j: (0, 0)),
                  pl.BlockSpec((1, D_MODEL, tn), lambda l, j: (l, 0, j)),
                  pl.BlockSpec((1, 1, tn), lambda l, j: (l, 0, j))],
        out_specs=pl.BlockSpec((1, m, tn), lambda l, j: (l, 0, j)),
        out_shape=jax.ShapeDtypeStruct((DEPTH, m, n), F32),
        compiler_params=_cp(("arbitrary", "arbitrary")),
        name="ada",
    )(c_all, ada_w_bf, ada_b.reshape(DEPTH, 1, n))


def _mod_spec(mod, tm, tiles_per_group):
    g, r, d = mod.shape
    if r == 1:
        return pl.BlockSpec((1, 1, d), lambda i, *_: (i // tiles_per_group, 0, 0))
    assert r == tm and g == 1
    return pl.BlockSpec((1, r, d), lambda i, *_: (0, 0, 0))


def _modulate_body(x_ref, sc_ref, sh_ref, h_ref):
    h_ref[...] = (x_ref[...] * (1.0 + sc_ref[0]) + sh_ref[0]).astype(BF16)


def _modulate(x, sc, sh, tm, tiles_per_group):
    n = x.shape[0]
    return pl.pallas_call(
        _modulate_body,
        grid=(n // tm,),
        in_specs=[pl.BlockSpec((tm, D_MODEL), lambda i: (i, 0)),
                  _mod_spec(sc, tm, tiles_per_group), _mod_spec(sh, tm, tiles_per_group)],
        out_specs=pl.BlockSpec((tm, D_MODEL), lambda i: (i, 0)),
        out_shape=jax.ShapeDtypeStruct((n, D_MODEL), BF16),
        compiler_params=_cp(("arbitrary",)),
        name="modulate",
    )(x, sc, sh)


def _mm_body(x_ref, w_ref, o_ref):
    o_ref[...] = _dot(x_ref[...], w_ref[...])


def _in_proj(h, w_pack, tm):
    n = h.shape[0]
    tn = 1024
    return pl.pallas_call(
        _mm_body,
        grid=(n // tm, D_PACK // tn),
        in_specs=[pl.BlockSpec((tm, D_MODEL), lambda i, j: (i, 0)),
                  pl.BlockSpec((D_MODEL, tn), lambda i, j: (0, j))],
        out_specs=pl.BlockSpec((tm, tn), lambda i, j: (i, j)),
        out_shape=jax.ShapeDtypeStruct((n, D_PACK), F32),
        compiler_params=_cp(("arbitrary", "arbitrary")),
        name="in_proj",
    )(h, w_pack)


def _kv_t_body(h_ref, wk_ref, wv_ref, wi_ref, k_ref, v_ref, i_ref):
    h = h_ref[...]
    k_ref[0] = _dot(h, wk_ref[...]).T
    v_ref[0] = _dot(h, wv_ref[...]).T
    i_ref[0] = _dot(h, wi_ref[...]).T[0:D_I]


def _kv_transposed(h, wk_t, wv_t, wi_t, nb, t_len, tt):
    nt = t_len // tt
    wspec = lambda w: pl.BlockSpec(w.shape, lambda b, t: (0, 0))
    ospec = lambda f: pl.BlockSpec((1, f, tt), lambda b, t: (b, 0, t))
    return pl.pallas_call(
        _kv_t_body,
        grid=(nb, nt),
        in_specs=[pl.BlockSpec((tt, D_MODEL), lambda b, t: (b * nt + t, 0)),
                  wspec(wk_t), wspec(wv_t), wspec(wi_t)],
        out_specs=[ospec(W_B), ospec(W_B), ospec(D_I)],
        out_shape=[jax.ShapeDtypeStruct((nb, W_B, t_len), F32), jax.ShapeDtypeStruct((nb, W_B, t_len), F32),
                   jax.ShapeDtypeStruct((nb, D_I, t_len), F32)],
        compiler_params=_cp(("arbitrary", "arbitrary")),
        name="kv_transposed",
    )(h, wk_t, wv_t, wi_t)


def _retention_body(q_ref, k_ref, v_ref, g_ref, s0_ref, lg_ref, gn_ref, o_ref, sout_ref,
                    s_scr, pad_scr, *, rb, nchunk):
    n = pl.program_id(1)
    c = CHUNK_A

    @pl.when(n == 0)
    def _():
        s_scr[...] = s0_ref[0]

    def padded(ref, slot):
        if rb == c:
            return ref[0]
        pad_scr[slot] = jnp.zeros((c, H_A * DK_A), F32)
        pad_scr[slot, 0:rb, :] = ref[0]
        return pad_scr[slot]

    q_all = padded(q_ref, 0)
    k_all = padded(k_ref, 1)
    v_all = padded(v_ref, 2)
    g_all = padded(g_ref, 3)

    row = _iota((c, c), 0).astype(F32)
    col = _iota((c, c), 1).astype(F32)
    rel = row - col
    outs = []
    for h in range(H_A):
        sl = slice(h * DK_A, (h + 1) * DK_A)
        lg = lg_ref[0:1, sl]
        q = (q_all[:, sl] * (DK_A ** -0.5)).astype(BF16)
        k = k_all[:, sl]
        v = v_all[:, sl].astype(BF16)
        s0 = s_scr[h]
        decay = jnp.where(rel >= 0.0, jnp.exp(lg * jnp.maximum(rel, 0.0)), 0.0)
        scores = _dot_nt(q, k.astype(BF16)) * decay
        inner = _dot(scores.astype(BF16), v)
        xi = jnp.exp(lg * (row + 1.0))
        cross = _dot(q, s0.astype(BF16)) * xi
        zeta = jnp.where(row < float(rb), jnp.exp(lg * (float(rb) - 1.0 - row)), 0.0)
        s_new = jnp.exp(lg * float(rb)) * s0 + _dot_tn((k * zeta).astype(BF16), v)
        s_scr[h] = s_new
        o = inner + cross
        mu = jnp.mean(o, axis=-1, keepdims=True)
        var = jnp.mean(jnp.square(o - mu), axis=-1, keepdims=True)
        o = (o - mu) * lax.rsqrt(var + LN_EPS) * gn_ref[0:1, sl]
        outs.append(_silu(g_all[:, sl]) * o)
    o_full = jnp.concatenate(outs, axis=-1)
    o_ref[0] = o_full[0:rb].astype(BF16)

    @pl.when(n == nchunk - 1)
    def _():
        sout_ref[0] = s_scr[...]


def _retention(p3, s0, lgam, gn_g, nb, nchunk, rb):
    w = H_A * DK_A
    col = lambda u: (lambda b, n: (b * nchunk + n, 0, u // 4))
    body = functools.partial(_retention_body, rb=rb, nchunk=nchunk)
    return pl.pallas_call(
        body,
        grid=(nb, nchunk),
        in_specs=[pl.BlockSpec((1, rb, w), col(U_QA)), pl.BlockSpec((1, rb, w), col(U_KA)),
                  pl.BlockSpec((1, rb, w), col(U_VA)), pl.BlockSpec((1, rb, w), col(U_GA)),
                  pl.BlockSpec((1, H_A, DK_A, DV_A), lambda b, n: (b, 0, 0, 0)),
                  pl.BlockSpec((1, w), lambda b, n: (0, 0)),
                  pl.BlockSpec((1, w), lambda b, n: (0, 0))],
        out_specs=[pl.BlockSpec((1, rb, w), lambda b, n: (b * nchunk + n, 0, 0)),
                   pl.BlockSpec((1, H_A, DK_A, DV_A), lambda b, n: (b, 0, 0, 0))],
        out_shape=[jax.ShapeDtypeStruct((nb * nchunk, rb, w), BF16),
                   jax.ShapeDtypeStruct((nb, H_A, DK_A, DV_A), F32)],
        scratch_shapes=[pltpu.VMEM((H_A, DK_A, DV_A), F32), pltpu.VMEM((4, CHUNK_A, w), F32)],
        compiler_params=_cp(("arbitrary", "arbitrary")),
        name="retention",
    )(p3, p3, p3, p3, s0, lgam, gn_g)


def _score_key(s):
    bits = pltpu.bitcast(s, I32)
    return jnp.where(bits < 0, bits ^ jnp.int32(0x7FFFFFFF), bits)


def _tri_incl():
    return jnp.where(_iota((LANES, LANES), 0) <= _iota((LANES, LANES), 1), 1.0, 0.0).astype(BF16)


KC = 512
DSA_QB = 256


def _dsa_prompt_body(q_ref, iq_ref, mq_ref, k_ref, v_ref, mk_ref, o_ref,
                     kx, vt, ikbf, keys, bias, q2, *, t_len, top):
    i = pl.program_id(1)
    nkc = t_len // KC
    qb = min(DSA_QB, t_len)

    @pl.when(i == 0)
    def _():
        prow = _iota((KC, LANES), 0)
        plane = _iota((KC, LANES), 1)
        for j in range(nkc):
            kpos = j * KC + prow
            posf = jnp.where(plane == 0, kpos // 64, jnp.where(plane == 1, kpos % 64,
                                                                jnp.where(plane == 2, 1, 0)))
            posf = posf.astype(F32).astype(BF16)
            for hp in range(H_B // 2):
                kx[j, :, hp * 2 * LANES:hp * 2 * LANES + LANES] = (
                    k_ref[j * KC:(j + 1) * KC, hp * LANES:(hp + 1) * LANES].astype(BF16))
                kx[j, :, hp * 2 * LANES + LANES:(hp + 1) * 2 * LANES] = posf
                vt[j, hp * LANES:(hp + 1) * LANES, :] = (
                    v_ref[j * KC:(j + 1) * KC, hp * LANES:(hp + 1) * LANES].T.astype(BF16))
            ikbf[j] = mk_ref[j * KC:(j + 1) * KC, :].astype(BF16)

    nsc = (i * qb) // KC + 1
    qpos = i * qb + _iota((KC, qb), 1)
    krow = _iota((KC, qb), 0)

    iq = (iq_ref[...] * (D_I ** -0.5)).astype(BF16)
    iw_t = mq_ref[...].T * (H_I ** -0.5)

    def score_chunk(j, carry):
        ik = ikbf[j][:, 0:D_I]
        acc = jnp.zeros((KC, qb), F32)
        for h in range(H_I):
            s = _dot_nt(ik, iq[:, h * D_I:(h + 1) * D_I])
            acc = acc + jnp.maximum(s, 0.0) * iw_t[IW_OFF + h:IW_OFF + h + 1]
        acc = jnp.where(j * KC + krow <= qpos, acc + 0.0, NEG_INF)
        keys[j] = _score_key(acc)
        return carry

    lax.fori_loop(0, nsc, score_chunk, 0)

    def count(pred):
        fold = 4 * SUBLANES

        def body(j, acc):
            m = jnp.where(pred(keys[j]), 1.0, 0.0).reshape(KC // fold, fold, qb)
            return acc + jnp.sum(m, axis=0)
        acc = lax.fori_loop(0, nsc, body, jnp.zeros((fold, qb), F32))
        return jnp.sum(acc, axis=0, keepdims=True)

    def count_ge(cand):
        return count(lambda kj: kj >= cand)

    lowest = jnp.full((1, qb), -2 ** 31, I32)

    def bisect():
        prefix = lowest
        for bit in range(31, -1, -1):
            step = jnp.int32(-2 ** 31) if bit == 31 else jnp.int32(1 << bit)
            cand = prefix + step
            cnt = count_ge(cand)
            prefix = jnp.where(cnt >= float(top), cand, prefix)
        return prefix

    neg_inf_key = _score_key(jnp.full((1, qb), NEG_INF, F32))
    tau = lax.cond((i + 1) * qb <= top, lambda: neg_inf_key, bisect)
    need = float(top) - count(lambda kj: kj > tau)

    tri_l = jnp.where(_iota((KC, KC), 1) <= _iota((KC, KC), 0), 1.0, 0.0).astype(BF16)

    def bias_chunk(j, carry):
        kj = keys[j]
        eq = jnp.where(kj == tau, 1.0, 0.0)
        pre = _dot(tri_l, eq.astype(BF16)) + carry
        keep = ((kj > tau) | ((eq > 0.0) & (pre <= need))) & (j * KC + krow <= qpos)
        bias[j] = jnp.where(keep, 0.0, NEG_INF)
        return carry + jnp.sum(eq, axis=0, keepdims=True)

    lax.fori_loop(0, nsc, bias_chunk, jnp.zeros((1, qb), F32))

    npair = H_B // 2
    q_all = q_ref[...]
    lane_q = _iota((qb, LANES), 1)
    base = (i * qb + 0 * lane_q).astype(F32)
    for hp in range(npair):
        qp = q_all[:, hp * LANES:(hp + 1) * LANES] * (D_B ** -0.5)
        for h2 in range(2):
            slope = 2.0 ** (-8.0 * (2 * hp + h2 + 1) / H_B)
            in_half = (lane_q >= h2 * D_B) & (lane_q < (h2 + 1) * D_B)
            feat = jnp.where(lane_q == 0, 64.0 * slope,
                             jnp.where(lane_q == 1, slope, jnp.where(lane_q == 2, -slope * base, 0.0)))
            q2[hp, h2 * qb:(h2 + 1) * qb, 0:LANES] = jnp.where(in_half, qp, 0.0).astype(BF16)
            q2[hp, h2 * qb:(h2 + 1) * qb, LANES:2 * LANES] = feat.astype(BF16)

    def attn_chunk(j, carry):
        ms, ls, accs = carry
        b2 = jnp.concatenate([bias[j], bias[j]], axis=1)
        out_m, out_l, out_acc = [], [], []
        for hp in range(npair):
            s = _dot_nt(kx[j, :, hp * 2 * LANES:(hp + 1) * 2 * LANES], q2[hp]) + b2
            m_new = jnp.maximum(ms[hp], jnp.max(s, axis=0, keepdims=True))
            m_safe = jnp.where(m_new == NEG_INF, 0.0, m_new)
            alpha = jnp.exp(ms[hp] - m_safe)
            p = jnp.exp(s - m_safe)
            out_m.append(m_new)
            out_l.append(ls[hp] * alpha + jnp.sum(p, axis=0, keepdims=True))
            out_acc.append(accs[hp] * alpha + _dot(vt[j, hp * LANES:(hp + 1) * LANES, :], p.astype(BF16)))
        return out_m, out_l, out_acc

    init = ([jnp.full((1, 2 * qb), NEG_INF, F32)] * npair, [jnp.zeros((1, 2 * qb), F32)] * npair,
            [jnp.zeros((LANES, 2 * qb), F32)] * npair)
    _, ls, accs = lax.fori_loop(0, nsc, attn_chunk, init)
    row_o = _iota((LANES, qb), 0)
    outs = []
    for hp in range(npair):
        o = accs[hp] / ls[hp]
        outs.append(jnp.where(row_o < D_B, o[:, 0:qb], o[:, qb:2 * qb]).T)
    o_ref[...] = jnp.concatenate(outs, axis=1).astype(BF16)


def _dsa_prompt(p2, nb, t_len):
    top = min(TOPK_MAX, t_len // 4)
    qb = min(DSA_QB, t_len)
    nq = t_len // qb
    nkc = t_len // KC
    body = functools.partial(_dsa_prompt_body, t_len=t_len, top=top)
    qspec = lambda u, wdt: pl.BlockSpec((qb, wdt), lambda b, i: (b * nq + i, u * LANES // wdt))
    aspec = lambda u, wdt: pl.BlockSpec((t_len, wdt), lambda b, i: (b, u * LANES // wdt))
    return pl.pallas_call(
        body,
        grid=(nb, nq),
        in_specs=[qspec(U_QB, W_B), qspec(U_IQ, W_B), qspec(U_MISC, LANES),
                  aspec(U_KB, W_B), aspec(U_VB, W_B), aspec(U_MISC, LANES)],
        out_specs=pl.BlockSpec((qb, W_B), lambda b, i: (b * nq + i, 0)),
        out_shape=jax.ShapeDtypeStruct((nb * t_len, W_B), BF16),
        scratch_shapes=[pltpu.VMEM((nkc, KC, 2 * W_B), BF16), pltpu.VMEM((nkc, W_B, KC), BF16),
                        pltpu.VMEM((nkc, KC, LANES), BF16),
                        pltpu.VMEM((nkc, KC, qb), I32), pltpu.VMEM((nkc, KC, qb), F32),
                        pltpu.VMEM((H_B // 2, 2 * qb, 2 * LANES), BF16)],
        compiler_params=_cp(("arbitrary", "arbitrary")),
        name="dsa_prompt",
    )(p2, p2, p2, p2, p2, p2)


PG_STEP = 32


def _dsa_s_score_body(pt_ref, iq_ref, iwb_ref, iknew_ref, *rest, t_len):
    pages = rest[:PG_STEP]
    o_ref, onew_ref = rest[PG_STEP:PG_STEP + 2]
    iq = (iq_ref[0] * (D_I ** -0.5)).astype(BF16)
    iwb = iwb_ref[0] * (H_I ** -0.5)
    row = _iota((SUBLANES, LANES), 0)
    lane = _iota((SUBLANES, LANES), 1)

    def scores(ikt):
        s = jnp.maximum(_dot(iq, ikt.astype(BF16)), 0.0) * iwb
        per_t = [jnp.sum(s[t * H_I:(t + 1) * H_I], axis=0, keepdims=True) for t in range(t_len)]
        return _rows_to_tile(per_t, LANES) + 0.0

    for j in range(PG_STEP):
        o_ref[0, j] = jnp.where(row < t_len, scores(pages[j][0, 0]), NEG_INF)

    @pl.when(pl.program_id(1) == 0)
    def _():
        onew_ref[0, 0] = jnp.where((row < t_len) & (lane <= row), scores(iknew_ref[0]), NEG_INF)


def _dsa_sample_scores(layer, page_table, cache_kidx_t, iq32, iwb, iknew, t_len):
    nb, n_pages = page_table.shape
    nsteps = n_pages // PG_STEP
    rows = t_len * H_I

    def page_spec(j):
        return pl.BlockSpec((1, 1, D_I, PAGE_SIZE),
                            lambda b, s, pt: (layer, pt[b, s * PG_STEP + j], 0, 0))

    grid_spec = pltpu.PrefetchScalarGridSpec(
        num_scalar_prefetch=1,
        grid=(nb, nsteps),
        in_specs=[pl.BlockSpec((1, rows, D_I), lambda b, s, pt: (b, 0, 0)),
                  pl.BlockSpec((1, rows, LANES), lambda b, s, pt: (b, 0, 0)),
                  pl.BlockSpec((1, D_I, PAGE_SIZE), lambda b, s, pt: (b, 0, 0))]
                 + [page_spec(j) for j in range(PG_STEP)],
        out_specs=[pl.BlockSpec((1, PG_STEP, SUBLANES, LANES), lambda b, s, pt: (b, s, 0, 0)),
                   pl.BlockSpec((1, 1, SUBLANES, LANES), lambda b, s, pt: (b, 0, 0, 0))],
    )
    return pl.pallas_call(
        functools.partial(_dsa_s_score_body, t_len=t_len),
        grid_spec=grid_spec,
        out_shape=[jax.ShapeDtypeStruct((nb, n_pages, SUBLANES, LANES), F32),
                   jax.ShapeDtypeStruct((nb, 1, SUBLANES, LANES), F32)],
        compiler_params=_cp(("arbitrary", "arbitrary")),
        name="dsa_sample_scores",
    )(page_table, iq32, iwb, iknew, *([cache_kidx_t] * PG_STEP))


SEL_B = 8


def _dsa_s_select_body(sc_ref, scn_ref, bias_ref, biasn_ref, *, t_len, n_pages, top):
    row8 = _iota((SUBLANES, LANES), 0)
    lane8 = _iota((SUBLANES, LANES), 1)
    keys_p = [_score_key(sc_ref[b]) for b in range(SEL_B)]
    keys_n = [_score_key(scn_ref[b, 0]) for b in range(SEL_B)]

    def count(pred_past, pred_new):
        c = jnp.sum(jnp.where(pred_past, 1.0, 0.0), axis=0) + jnp.where(pred_new, 1.0, 0.0)
        return jnp.sum(c, axis=1, keepdims=True)

    prefix = [jnp.full((SUBLANES, 1), -2 ** 31, I32)] * SEL_B
    for bit in range(31, -1, -1):
        step = jnp.int32(-2 ** 31) if bit == 31 else jnp.int32(1 << bit)
        nxt = []
        for b in range(SEL_B):
            cand = prefix[b] + step
            cnt = count(keys_p[b] >= cand[None], keys_n[b] >= cand)
            nxt.append(jnp.where(cnt >= float(top), cand, prefix[b]))
        prefix = nxt

    tri = _tri_incl()
    ones = jnp.ones((LANES, LANES), BF16)
    for b in range(SEL_B):
        tau = prefix[b]
        need = float(top) - count(keys_p[b] > tau[None], keys_n[b] > tau)
        eq = jnp.where(keys_p[b] == tau[None], 1.0, 0.0)
        eq2 = eq.reshape(n_pages * SUBLANES, LANES).astype(BF16)
        pre_in = _dot(eq2, tri).reshape(n_pages, SUBLANES, LANES)
        totals = _dot(eq2, ones).reshape(n_pages, SUBLANES, LANES)
        carry = jnp.zeros((SUBLANES, LANES), F32)
        for p in range(n_pages):
            keep = (keys_p[b][p] > tau) | ((eq[p] > 0.0) & (pre_in[p] + carry <= need))
            bias_ref[b, p] = jnp.where(keep & (row8 < t_len), 0.0, NEG_INF)
            carry = carry + totals[p]
        e = jnp.where(keys_n[b] == tau, 1.0, 0.0)
        pre = _dot(e.astype(BF16), tri) + carry
        keep = (keys_n[b] > tau) | ((e > 0.0) & (pre <= need))
        biasn_ref[b, 0] = jnp.where(keep & (row8 < t_len) & (lane8 <= row8), 0.0, NEG_INF)


def _dsa_sample_select(scores, scores_new, t_len):
    nb, n_pages = scores.shape[:2]
    top = min(TOPK_MAX, (n_pages * PAGE_SIZE + t_len) // 4)
    past = pl.BlockSpec((SEL_B, n_pages, SUBLANES, LANES), lambda c: (c, 0, 0, 0))
    new = pl.BlockSpec((SEL_B, 1, SUBLANES, LANES), lambda c: (c, 0, 0, 0))
    return pl.pallas_call(
        functools.partial(_dsa_s_select_body, t_len=t_len, n_pages=n_pages, top=top),
        grid=(nb // SEL_B,),
        in_specs=[past, new],
        out_specs=[past, new],
        out_shape=[jax.ShapeDtypeStruct(scores.shape, F32), jax.ShapeDtypeStruct(scores_new.shape, F32)],
        compiler_params=_cp(("arbitrary",)),
        name="dsa_sample_select",
    )(scores, scores_new)


def _dsa_s_attn_body(pt_ref, bias_ref, biasn_ref, q_ref, knew_ref, vnew_ref, *rest, t_len, n_pages):
    kpages = rest[:PG_STEP]
    vpages = rest[PG_STEP:2 * PG_STEP]
    o_ref = rest[2 * PG_STEP]
    m_scr, l_scr, acc_scr = rest[2 * PG_STEP + 1:]
    s_idx = pl.program_id(1)
    nsteps = n_pages // PG_STEP
    past = n_pages * PAGE_SIZE
    rows = t_len * H_B

    @pl.when(s_idx == 0)
    def _():
        m_scr[...] = jnp.full((rows, LANES), NEG_INF, F32)
        l_scr[...] = jnp.zeros((rows, LANES), F32)
        acc_scr[...] = jnp.zeros((rows, W_B), F32)

    q = q_ref[0]
    col = _iota((SUBLANES, W_B), 1)
    hrow = _iota((SUBLANES, W_B), 0)
    blockmask = (col // D_B) == hrow
    qexp = jnp.concatenate(
        [jnp.where(blockmask, jnp.broadcast_to(q[t:t + 1], (SUBLANES, W_B)), 0.0) for t in range(t_len)],
        axis=0).astype(BF16)
    hvec = _iota((rows, LANES), 0) % H_B
    slope = jnp.exp2(-8.0 * (hvec.astype(F32) + 1.0) / H_B)
    tvec = _iota((rows, LANES), 0) // H_B
    qpos = past + tvec
    lane = _iota((rows, LANES), 1)

    def expand_bias(bp):
        return jnp.concatenate([jnp.broadcast_to(bp[t:t + 1], (SUBLANES, LANES)) for t in range(t_len)],
                               axis=0)

    def attend(k_list, v_list, bias_list, kpos0_list):
        logits = []
        for kp, bp, kpos0 in zip(k_list, bias_list, kpos0_list):
            s = _dot(qexp, kp.astype(BF16)) * (D_B ** -0.5)
            dist = (qpos - (kpos0 + lane)).astype(F32)
            logits.append(s - slope * dist + expand_bias(bp))
        m_old = m_scr[...]
        m_new = m_old
        for s in logits:
            m_new = jnp.maximum(m_new, jnp.max(s, axis=1, keepdims=True))
        m_safe = jnp.where(m_new == NEG_INF, 0.0, m_new)
        scale = jnp.exp(m_old - m_safe)
        l = l_scr[...] * scale
        acc = acc_scr[...] * scale[:, 0:1]
        for s, vp in zip(logits, v_list):
            p = jnp.exp(s - m_safe)
            l = l + jnp.sum(p, axis=1, keepdims=True)
            acc = acc + _dot_nt(p.astype(BF16), vp.astype(BF16))
        m_scr[...] = m_new
        l_scr[...] = l
        acc_scr[...] = acc

    @pl.when(s_idx < nsteps)
    def _():
        base = s_idx * PG_STEP
        attend([kpages[j][0, 0] for j in range(PG_STEP)],
               [vpages[j][0, 0] for j in range(PG_STEP)],
               [bias_ref[0, base + j] for j in range(PG_STEP)],
               [(base + j) * PAGE_SIZE for j in range(PG_STEP)])

    @pl.when(s_idx == nsteps)
    def _():
        attend([knew_ref[0]], [vnew_ref[0]], [biasn_ref[0, 0]], [past])
        o = acc_scr[...] / l_scr[...][:, 0:1]
        outs = [jnp.sum(jnp.where(blockmask, o[t * H_B:(t + 1) * H_B], 0.0), axis=0, keepdims=True)
                for t in range(t_len)]
        o_ref[0] = _rows_to_tile(outs, W_B).astype(BF16)


def _dsa_sample_attend(layer, page_table, bias, bias_new, q3, knew, vnew, cache_k4, cache_v4, t_len):
    nb, n_pages = page_table.shape
    nsteps = n_pages // PG_STEP
    rows = t_len * H_B

    def page_spec(j):
        def imap(b, s, pt):
            return (layer, pt[b, jnp.minimum(s, nsteps - 1) * PG_STEP + j], 0, 0)
        return pl.BlockSpec((1, 1, W_B, PAGE_SIZE), imap)

    per_b = lambda shape: pl.BlockSpec((1,) + shape, lambda b, s, pt: (b,) + (0,) * len(shape))
    grid_spec = pltpu.PrefetchScalarGridSpec(
        num_scalar_prefetch=1,
        grid=(nb, nsteps + 1),
        in_specs=[per_b((n_pages, SUBLANES, LANES)), per_b((1, SUBLANES, LANES)), per_b((t_len, W_B)),
                  per_b((W_B, PAGE_SIZE)), per_b((W_B, PAGE_SIZE))]
                 + [page_spec(j) for j in range(PG_STEP)] + [page_spec(j) for j in range(PG_STEP)],
        out_specs=per_b((SUBLANES, W_B)),
        scratch_shapes=[pltpu.VMEM((rows, LANES), F32), pltpu.VMEM((rows, LANES), F32),
                        pltpu.VMEM((rows, W_B), F32)],
    )
    body = functools.partial(_dsa_s_attn_body, t_len=t_len, n_pages=n_pages)
    return pl.pallas_call(
        body,
        grid_spec=grid_spec,
        out_shape=jax.ShapeDtypeStruct((nb, SUBLANES, W_B), BF16),
        compiler_params=_cp(("arbitrary", "arbitrary")),
        name="dsa_sample_attend",
    )(page_table, bias, bias_new, q3, knew, vnew, *([cache_k4] * PG_STEP), *([cache_v4] * PG_STEP))


def _head_block_ones(n):
    return jnp.where((_iota((n, n), 0) // D_C) == (_iota((n, n), 1) // D_C), 1.0, 0.0).astype(BF16)


def _rwkv_prep_body(*refs, tm, seq_len):
    pc_refs = refs[:PC_BLOCKS]
    (sh_ref, mu_ref, w0_ref, a0_ref, kk_ref, ka_ref, rk_ref, wwa_ref, g2_ref,
     r_ref, w_ref, k_ref, v_ref, na_ref, b_ref, g_ref, bonus_ref, carry) = refs[PC_BLOCKS:]
    pc = jnp.concatenate([r[...] for r in pc_refs], axis=1)
    rolled = pltpu.roll(pc, 1, 0)
    row = _iota(pc.shape, 0)
    if seq_len >= tm:
        tiles_per_seq = seq_len // tm
        first = (pl.program_id(0) % tiles_per_seq) == 0
        head_row = jnp.where(first, sh_ref[0], carry[...])
        prev = jnp.where(row == 0, head_row, rolled)
        carry[...] = pc[tm - 1:tm, :]
    else:
        prev = jnp.where((row % seq_len) == 0, sh_ref[0], rolled)
    pm = pc + (prev - pc) * mu_ref[...]
    r = pm[:, 0:W_C]
    k = pm[:, W_C:2 * W_C]
    v = pm[:, 2 * W_C:3 * W_C]
    lwa = pm[:, 3 * W_C:3 * W_C + LANES]
    lg = pm[:, 3 * W_C + LANES:3 * W_C + 2 * LANES]
    lane = _iota(lwa.shape, 1)
    lwa = jnp.where(lane < LORA_W, jnp.tanh(lwa), lwa)
    wa = _dot(lwa.astype(BF16), wwa_ref[...])
    x = -(w0_ref[...] + wa[:, 0:W_C])
    softplus = jnp.maximum(x, 0.0) + jnp.log1p(jnp.exp(-jnp.abs(x)))
    w_raw = -softplus - 0.5
    a = _sigmoid(a0_ref[...] + wa[:, W_C:2 * W_C])
    g = _dot(_sigmoid(lg).astype(BF16), g2_ref[...])
    ones = _head_block_ones(W_C)
    kk = k * kk_ref[...]
    ss = _dot((kk * kk).astype(BF16), ones)
    kk = kk / jnp.maximum(jnp.sqrt(ss), 1e-12)
    k2 = k * (1.0 + (a - 1.0) * ka_ref[...])
    r_ref[...] = r
    w_ref[...] = jnp.exp(-jnp.exp(w_raw))
    k_ref[...] = k2
    v_ref[...] = v
    na_ref[...] = -kk
    b_ref[...] = kk * a
    g_ref[...] = g
    bonus_ref[...] = _dot((r * k2 * rk_ref[...]).astype(BF16), ones) * v


def _rwkv_prep(p2, shift_rows, prm, tm, seq_len):
    n = p2.shape[0]
    out_rm = pl.BlockSpec((tm, W_C), lambda i: (i, 0))
    shape_rm = jax.ShapeDtypeStruct((n, W_C), F32)
    if seq_len >= tm:
        tiles_per_seq = seq_len // tm
        sh_spec = pl.BlockSpec((1, 1, D_C_IN), lambda i: (i // tiles_per_seq, 0, 0))
        out_scan = pl.BlockSpec((tm, W_C), lambda i: (i % tiles_per_seq, i // tiles_per_seq))
        shape_scan = jax.ShapeDtypeStruct((seq_len, (n // seq_len) * W_C), F32)
    else:
        sh_spec = pl.BlockSpec((1, tm, D_C_IN), lambda i: (i, 0, 0))
        out_scan, shape_scan = out_rm, shape_rm
    row = lambda w: pl.BlockSpec((1, w), lambda i: (0, 0))
    body = functools.partial(_rwkv_prep_body, tm=tm, seq_len=seq_len)
    pc_block0 = U_PC * LANES // PC_BLOCK_W

    def pc_spec(j):
        return pl.BlockSpec((tm, PC_BLOCK_W), lambda i: (i, pc_block0 + j))

    return pl.pallas_call(
        body,
        grid=(n // tm,),
        in_specs=[pc_spec(j) for j in range(PC_BLOCKS)]
                 + [sh_spec, row(D_C_IN), row(W_C), row(W_C), row(W_C), row(W_C), row(W_C),
                    pl.BlockSpec((LANES, 2 * W_C), lambda i: (0, 0)),
                    pl.BlockSpec((LORA_G, W_C), lambda i: (0, 0))],
        out_specs=[out_scan] * 6 + [out_rm] * 2,
        out_shape=[shape_scan] * 6 + [shape_rm] * 2,
        scratch_shapes=[pltpu.VMEM((1, D_C_IN), F32)],
        compiler_params=_cp(("arbitrary",)),
        name="rwkv_prep",
    )(*([p2] * PC_BLOCKS), shift_rows, prm["mu"], prm["w0"], prm["a0"], prm["kk"], prm["ka"], prm["rk"],
      prm["wwa"], prm["g2"])


SCAN_B = 8
SCAN_TB = 128


def _rwkv_scan_body(r_ref, w_ref, k_ref, v_ref, na_ref, b_ref, s0_ref, y_ref, sout_ref, s_scr, ybuf,
                    *, tb, ntb):
    tblk = pl.program_id(1)

    @pl.when(tblk == 0)
    def _():
        s_scr[...] = s0_ref[...]

    ones1 = _head_block_ones(LANES)
    ones2 = _head_block_ones(2 * LANES)
    vi = _iota((D_C, LANES), 0)
    li = _iota((D_C, LANES), 1)
    irep = jnp.where((li % D_C) == vi, 1.0, 0.0)
    ngrp = SCAN_B * (H_C // 2)

    def vec(x_row, g):
        return x_row[:, g * LANES:(g + 1) * LANES]

    def y_row(yb_all):
        return jnp.concatenate(
            [jnp.sum(yb_all[g * D_C:(g + 1) * D_C] * irep, axis=0, keepdims=True) for g in range(ngrp)], axis=1)

    def advance(w_t, k_t, v_t, na_t, b_t, r_p):
        s_old = [s_scr[g] for g in range(ngrp)]
        na_b, r_b = na_t.astype(BF16), r_p.astype(BF16)
        s_b = [s.astype(BF16) for s in s_old]
        lhs1 = jnp.concatenate(
            [jnp.concatenate([s_b[g] * vec(na_b, g), s_b[g] * vec(r_b, g)], axis=1) for g in range(ngrp)],
            axis=0)
        res1 = _dot(lhs1, ones2)
        lhs2 = jnp.concatenate(
            [jnp.concatenate([(irep * vec(v_t, g)).astype(BF16), (irep * vec(v_t, g + 1)).astype(BF16)], axis=1)
             for g in range(0, ngrp, 2)], axis=0)
        res2 = _dot(lhs2, ones2)
        for g in range(ngrp):
            sa = res1[g * D_C:(g + 1) * D_C, 0:LANES]
            pr, half = divmod(g, 2)
            vb = res2[pr * D_C:(pr + 1) * D_C, half * LANES:(half + 1) * LANES]
            s_scr[g] = s_old[g] * vec(w_t, g) + sa * vec(b_t, g) + vb * vec(k_t, g)
        return y_row(res1[:, LANES:2 * LANES])

    def final_y(r_last):
        lhs = jnp.concatenate([(s_scr[g] * vec(r_last, g)).astype(BF16) for g in range(ngrp)], axis=0)
        return y_row(_dot(lhs, ones1))

    refs = (w_ref, k_ref, v_ref, na_ref, b_ref)
    if tb % SUBLANES == 0:
        ybuf[...] = jnp.zeros(ybuf.shape, F32)

        def tile_step(t8, carry):
            base = pl.multiple_of(t8 * SUBLANES, SUBLANES)
            prev = pl.multiple_of(jnp.maximum(t8 - 1, 0) * SUBLANES, SUBLANES)
            tiles = [ref[pl.ds(base, SUBLANES), :] for ref in refs]
            r_tile = r_ref[pl.ds(base, SUBLANES), :]
            r_prev_tile = r_ref[pl.ds(prev, SUBLANES), :]
            for tt in range(SUBLANES):
                r_p = r_prev_tile[SUBLANES - 1:SUBLANES] if tt == 0 else r_tile[tt - 1:tt]
                y_p = advance(*[x[tt:tt + 1] for x in tiles], r_p)
                if tt == 0:
                    ybuf[SUBLANES - 1:SUBLANES, :] = y_p
                    y_ref[pl.ds(prev, SUBLANES), :] = ybuf[...]
                else:
                    ybuf[tt - 1:tt, :] = y_p
            return carry

        lax.fori_loop(0, tb // SUBLANES, tile_step, 0)
        ybuf[SUBLANES - 1:SUBLANES, :] = final_y(r_ref[tb - 1:tb, :])
        y_ref[tb - SUBLANES:tb, :] = ybuf[...]
    else:
        for t in range(tb):
            y_p = advance(*[ref[t:t + 1, :] for ref in refs], r_ref[max(t - 1, 0):max(t - 1, 0) + 1, :])
            if t > 0:
                y_ref[t - 1:t, :] = y_p
        y_ref[tb - 1:tb, :] = final_y(r_ref[tb - 1:tb, :])

    @pl.when(tblk == ntb - 1)
    def _():
        sout_ref[...] = s_scr[...]


def _rwkv_scan(r, w, k, v, na, b, s0, nb, t_len, tb):
    ntb = t_len // tb
    npair = H_C // 2
    blk = pl.BlockSpec((tb, SCAN_B * W_C), lambda c, t: (t, c))
    sblk = pl.BlockSpec((SCAN_B * npair, D_C, LANES), lambda c, t: (c, 0, 0))
    body = functools.partial(_rwkv_scan_body, tb=tb, ntb=ntb)
    return pl.pallas_call(
        body,
        grid=(nb // SCAN_B, ntb),
        in_specs=[blk] * 6 + [sblk],
        out_specs=[blk, sblk],
        out_shape=[jax.ShapeDtypeStruct((t_len, nb * W_C), F32),
                   jax.ShapeDtypeStruct((nb * npair, D_C, LANES), F32)],
        scratch_shapes=[pltpu.VMEM((SCAN_B * npair, D_C, LANES), F32),
                        pltpu.VMEM((SUBLANES, SCAN_B * W_C), F32)],
        compiler_params=_cp(("arbitrary", "arbitrary")),
        name="rwkv_scan",
    )(r, w, k, v, na, b, s0)


def _rwkv_post_body(y_ref, g_ref, bonus_ref, lg_ref, lb_ref, o_ref):
    y = y_ref[...]
    ones = _head_block_ones(W_C)
    mu = _dot(y.astype(BF16), ones) * (1.0 / D_C)
    d = y - mu
    var = _dot((d * d).astype(BF16), ones) * (1.0 / D_C)
    yn = d * lax.rsqrt(var + RWKV_GN_EPS) * lg_ref[...] + lb_ref[...]
    o_ref[...] = ((yn + bonus_ref[...]) * g_ref[...]).astype(BF16)


def _rwkv_post(y, g, bonus, lnx_g, lnx_b, tm, seq_len):
    n = g.shape[0]
    blk = pl.BlockSpec((tm, W_C), lambda i: (i, 0))
    row = pl.BlockSpec((1, W_C), lambda i: (0, 0))
    if seq_len >= tm:
        tiles_per_seq = seq_len // tm
        yblk = pl.BlockSpec((tm, W_C), lambda i: (i % tiles_per_seq, i // tiles_per_seq))
    else:
        yblk = blk
    return pl.pallas_call(
        _rwkv_post_body,
        grid=(n // tm,),
        in_specs=[yblk, blk, blk, row, row],
        out_specs=blk,
        out_shape=jax.ShapeDtypeStruct((n, W_C), BF16),
        compiler_params=_cp(("arbitrary",)),
        name="rwkv_post",
    )(y, g, bonus, lnx_g, lnx_b)


def _layer_norm(x, g, b):
    mu = jnp.mean(x, axis=-1, keepdims=True)
    var = jnp.mean(jnp.square(x - mu), axis=-1, keepdims=True)
    return (x - mu) * lax.rsqrt(var + LN_EPS) * g + b


def _merge_body(oa_ref, ob_ref, oc_ref, gla_ref, glb_ref, glc_ref, x_ref, wb_ref, wo_ref,
                gt_ref, sc_ref, sh_ref, lg_ref, lb_ref, x1_ref, h_ref):
    mixed = None
    for gidx, (o_ref, gl_ref) in enumerate(((oa_ref, gla_ref), (ob_ref, glb_ref), (oc_ref, glc_ref))):
        br = _dot(o_ref[...], wb_ref[gidx])
        term = _sigmoid(gl_ref[...]) * br
        mixed = term if mixed is None else mixed + term
    y = _dot(mixed.astype(BF16), wo_ref[...])
    x1 = _layer_norm(ALPHA * x_ref[...] + gt_ref[0] * y, lg_ref[...], lb_ref[...])
    x1_ref[...] = x1
    h_ref[...] = (x1 * (1.0 + sc_ref[0]) + sh_ref[0]).astype(BF16)


def _merge(oa, ob, oc, p2, x, wb, wo, gt, sc, sh, ln_g, ln_b, tm, tiles_per_group):
    n = x.shape[0]
    blk = lambda w: pl.BlockSpec((tm, w), lambda i: (i, 0))
    row = pl.BlockSpec((1, D_MODEL), lambda i: (0, 0))
    ms = lambda m: _mod_spec(m, tm, tiles_per_group)
    gl_block0 = U_GL * LANES // D_MODEL
    return pl.pallas_call(
        _merge_body,
        grid=(n // tm,),
        in_specs=[blk(BRANCH_W), blk(BRANCH_W), blk(BRANCH_W)]
                 + [pl.BlockSpec((tm, D_MODEL), lambda i, g=g: (i, gl_block0 + g)) for g in range(N_BRANCH)]
                 + [blk(D_MODEL),
                  pl.BlockSpec((N_BRANCH, BRANCH_W, D_MODEL), lambda i: (0, 0, 0)),
                  pl.BlockSpec((D_MODEL, D_MODEL), lambda i: (0, 0)),
                  ms(gt), ms(sc), ms(sh), row, row],
        out_specs=[blk(D_MODEL), blk(D_MODEL)],
        out_shape=[jax.ShapeDtypeStruct((n, D_MODEL), F32), jax.ShapeDtypeStruct((n, D_MODEL), BF16)],
        compiler_params=_cp(("arbitrary",)),
        name="merge",
    )(oa, ob, oc, p2, p2, p2, x, wb, wo, gt, sc, sh, ln_g, ln_b)


def _ffn_finish(acc, x_ref, gt_ref, sc_ref, sh_ref, lg_ref, lb_ref, x2_ref, h_ref):
    x2 = _layer_norm(ALPHA * x_ref[...] + gt_ref[0] * acc, lg_ref[...], lb_ref[...])
    x2_ref[...] = x2
    h_ref[...] = (x2 * (1.0 + sc_ref[0]) + sh_ref[0]).astype(BF16)


def _ffn_body(h_ref, x_ref, w1_ref, w3_ref, w2_ref, gt_ref, sc_ref, sh_ref, lg_ref, lb_ref,
              x2_ref, hn_ref, acc, *, nf):
    f = pl.program_id(1)
    h = h_ref[...]
    u = (_silu(_dot(h, w1_ref[...])) * _dot(h, w3_ref[...])).astype(BF16)
    part = _dot(u, w2_ref[...])

    @pl.when(f == 0)
    def _():
        acc[...] = part

    @pl.when(f > 0)
    def _():
        acc[...] = acc[...] + part

    @pl.when(f == nf - 1)
    def _():
        _ffn_finish(acc[...], x_ref, gt_ref, sc_ref, sh_ref, lg_ref, lb_ref, x2_ref, hn_ref)


def _ffn_dense(h, x, w1, w3, w2, gt, sc, sh, ln_g, ln_b, tm, tiles_per_group):
    n = x.shape[0]
    tf = D_FF // 2
    nf = D_FF // tf
    blk = pl.BlockSpec((tm, D_MODEL), lambda i, f: (i, 0))
    row = pl.BlockSpec((1, D_MODEL), lambda i, f: (0, 0))
    ms = lambda m: _mod_spec(m, tm, tiles_per_group)
    return pl.pallas_call(
        functools.partial(_ffn_body, nf=nf),
        grid=(n // tm, nf),
        in_specs=[blk, blk,
                  pl.BlockSpec((D_MODEL, tf), lambda i, f: (0, f)),
                  pl.BlockSpec((D_MODEL, tf), lambda i, f: (0, f)),
                  pl.BlockSpec((tf, D_MODEL), lambda i, f: (f, 0)),
                  ms(gt), ms(sc), ms(sh), row, row],
        out_specs=[blk, blk],
        out_shape=[jax.ShapeDtypeStruct((n, D_MODEL), F32), jax.ShapeDtypeStruct((n, D_MODEL), BF16)],
        scratch_shapes=[pltpu.VMEM((tm, D_MODEL), F32)],
        compiler_params=_cp(("arbitrary", "arbitrary")),
        name="ffn_dense",
    )(h, x, w1, w3, w2, gt, sc, sh, ln_g, ln_b)


def _moe_body(h_ref, x_ref, rw_ref, rb_ref, w1_ref, w3_ref, w2_ref, gt_ref, sc_ref, sh_ref, lg_ref, lb_ref,
              x2_ref, hn_ref, acc, gate):
    e = pl.program_id(1)
    h = h_ref[...]
    lane = _iota((h.shape[0], LANES), 1)

    @pl.when(e == 0)
    def _():
        logits = _dot(h, rw_ref[...]) + rb_ref[...]
        m1 = jnp.max(logits, axis=1, keepdims=True)
        i1 = jnp.min(jnp.where(logits == m1, lane, LANES), axis=1, keepdims=True)
        rest = jnp.where(lane == i1, NEG_INF, logits)
        m2 = jnp.max(rest, axis=1, keepdims=True)
        i2 = jnp.min(jnp.where(rest == m2, lane, LANES), axis=1, keepdims=True)
        e2 = jnp.exp(m2 - m1)
        p1 = 1.0 / (1.0 + e2)
        p2 = e2 / (1.0 + e2)
        gate[...] = jnp.where(lane == i1, p1, 0.0) + jnp.where(lane == i2, p2, 0.0)
        acc[...] = jnp.zeros(acc.shape, F32)

    ge = jnp.sum(jnp.where(lane == e, gate[...], 0.0), axis=1, keepdims=True)
    u = (_silu(_dot(h, w1_ref[0])) * _dot(h, w3_ref[0])).astype(BF16)
    acc[...] = acc[...] + ge * _dot(u, w2_ref[0])

    @pl.when(e == N_EXPERTS - 1)
    def _():
        _ffn_finish(acc[...], x_ref, gt_ref, sc_ref, sh_ref, lg_ref, lb_ref, x2_ref, hn_ref)


def _ffn_moe(h, x, rw, rb, w1, w3, w2, gt, sc, sh, ln_g, ln_b, tm, tiles_per_group):
    n = x.shape[0]
    blk = pl.BlockSpec((tm, D_MODEL), lambda i, e: (i, 0))
    row = pl.BlockSpec((1, D_MODEL), lambda i, e: (0, 0))
    ms = lambda m: _mod_spec(m, tm, tiles_per_group)
    return pl.pallas_call(
        _moe_body,
        grid=(n // tm, N_EXPERTS),
        in_specs=[blk, blk,
                  pl.BlockSpec((D_MODEL, LANES), lambda i, e: (0, 0)),
                  pl.BlockSpec((1, LANES), lambda i, e: (0, 0)),
                  pl.BlockSpec((1, D_MODEL, D_FF_E), lambda i, e: (e, 0, 0)),
                  pl.BlockSpec((1, D_MODEL, D_FF_E), lambda i, e: (e, 0, 0)),
                  pl.BlockSpec((1, D_FF_E, D_MODEL), lambda i, e: (e, 0, 0)),
                  ms(gt), ms(sc), ms(sh), row, row],
        out_specs=[blk, blk],
        out_shape=[jax.ShapeDtypeStruct((n, D_MODEL), F32), jax.ShapeDtypeStruct((n, D_MODEL), BF16)],
        scratch_shapes=[pltpu.VMEM((tm, D_MODEL), F32), pltpu.VMEM((tm, LANES), F32)],
        compiler_params=_cp(("arbitrary", "arbitrary")),
        name="ffn_moe",
    )(h, x, rw, rb, w1, w3, w2, gt, sc, sh, ln_g, ln_b)


def _pack_w_in(w):
    o_iw = 8 * 512
    o_ik = o_iw + H_I
    o_pc = o_ik + D_I
    o_gl = o_pc + D_C_IN
    z = lambda n: jnp.zeros((w.shape[0], n), w.dtype)
    packed = jnp.concatenate(
        [w[:, :o_iw], w[:, o_ik:o_pc], w[:, o_iw:o_ik], z(LANES - D_I - H_I), z(LANES),
         w[:, o_pc:o_gl], w[:, o_gl:]], axis=-1)
    assert packed.shape[-1] == D_PACK
    return packed


def _scan_state_in(s):
    b = s.shape[0]
    s = s.reshape(b, H_C // 2, 2, D_C, D_C).transpose(0, 1, 3, 2, 4)
    return s.reshape(b * (H_C // 2), D_C, 2 * D_C)


def _scan_state_out(s, b):
    s = s.reshape(b, H_C // 2, D_C, 2, D_C).transpose(0, 1, 3, 2, 4)
    return s.reshape(b, H_C, D_C, D_C)


def _mods(ada_l, lo, hi, per_row_t):
    parts = jnp.split(ada_l[lo:hi], 6, axis=-1)
    if per_row_t is None:
        return [p[:, None, :] for p in parts]
    return [jnp.repeat(p, per_row_t, axis=0)[None] for p in parts]


def _layer(l, x, h, mods, mods_next, prm, attend_fn, ret_s0, wkv_s0, shift_rows, nb, t_len):
    n = nb * t_len
    tm_proj = min(n, TM_PROJ)
    tm = min(n, TM_ROW)
    sh1, sc1, gt1, sh2, sc2, gt2 = mods
    p2 = _in_proj(h, prm["w_in"][l], tm_proj)

    rb = CHUNK_A if t_len % CHUNK_A == 0 else t_len
    nchunk = t_len // rb
    oa, ret_s = _retention(p2.reshape(nb * nchunk, rb, D_PACK), ret_s0, prm["lgam"], prm["ret_gn_g"][l],
                           nb, nchunk, rb)
    oa = oa.reshape(n, H_A * DV_A)

    ob = attend_fn(p2)

    rp = {k: prm["rw_" + k][l] for k in ("mu", "w0", "a0", "kk", "ka", "rk", "wwa", "g2")}
    r, w, k2, v, na, b, g, bonus = _rwkv_prep(p2, shift_rows, rp, tm, t_len)
    tb = min(t_len, SCAN_TB)
    if t_len >= tm:
        tmaj = lambda a: a
    else:
        tmaj = lambda a: a.reshape(nb, t_len, W_C).transpose(1, 0, 2).reshape(t_len, nb * W_C)
    y, wkv_s = _rwkv_scan(tmaj(r), tmaj(w), tmaj(k2), tmaj(v), tmaj(na), tmaj(b), wkv_s0, nb, t_len, tb)
    if t_len < tm:
        y = y.reshape(t_len, nb, W_C).transpose(1, 0, 2).reshape(n, W_C)
    oc = _rwkv_post(y, g, bonus, prm["rw_lnx_g"][l], prm["rw_lnx_b"][l], tm, t_len)

    tm_m = tm
    tpg_m = max(t_len // tm_m, 1)
    x1, h2 = _merge(oa, ob, oc, p2, x, prm["w_branch"][l], prm["w_out"][l], gt1, sc2, sh2,
                    prm["ln1_g"][l], prm["ln1_b"][l], tm_m, tpg_m)
    sh_n, sc_n = mods_next
    i = l // 2
    if l % 2 == 0:
        x2, hn = _ffn_dense(h2, x1, prm["ffn_w1"][i], prm["ffn_w3"][i], prm["ffn_w2"][i], gt2, sc_n, sh_n,
                            prm["ln2_g"][l], prm["ln2_b"][l], tm_m, tpg_m)
    else:
        x2, hn = _ffn_moe(h2, x1, prm["moe_router"][i], prm["moe_router_b"][i], prm["moe_w1"][i],
                          prm["moe_w3"][i], prm["moe_w2"][i], gt2, sc_n, sh_n,
                          prm["ln2_g"][l], prm["ln2_b"][l], tm_m, tpg_m)
    shift_s = p2.reshape(nb, t_len, D_PACK)[:, -1, U_PC * LANES:U_PC * LANES + D_C_IN]
    if t_len >= TM_PROJ:
        kh, vh, ik = _kv_transposed(h, prm["wk_t"][l], prm["wv_t"][l], prm["wi_t"][l], nb, t_len, TM_PROJ)
    else:
        kh = p2[:, U_KB * LANES:U_KB * LANES + W_B].reshape(nb, t_len, H_B, D_B)
        vh = p2[:, U_VB * LANES:U_VB * LANES + W_B].reshape(nb, t_len, H_B, D_B)
        ik = p2[:, U_MISC * LANES:U_MISC * LANES + D_I].reshape(nb, t_len, D_I)
    return x2, hn, (kh, vh, ik, ret_s, _scan_state_out(wkv_s, nb), shift_s)


def kernel(x_prompt, x_sample, c_prompt, c_sample, cache_k, cache_v, cache_kidx, page_table, state_ret, state_wkv, state_shift, ada_w, ada_b, w_in, ret_gn_g, rw_mu, rw_w0, rw_w2, rw_a0, rw_a2, rw_g2, rw_kk, rw_ka, rw_rk, rw_lnx_g, rw_lnx_b, w_branch, w_out, ln1_g, ln1_b, ln2_g, ln2_b, ffn_w1, ffn_w3, ffn_w2, moe_router, moe_router_b, moe_w1, moe_w3, moe_w2):
    bp, tp, _ = x_prompt.shape
    bs, ts, _ = x_sample.shape
    n_pages = page_table.shape[1]
    n_pool = cache_k.shape[1]

    zw = jnp.zeros((DEPTH, LORA_W, W_C), F32)
    wwa = jnp.concatenate([jnp.concatenate([rw_w2, zw], axis=-1),
                           jnp.concatenate([zw, rw_a2], axis=-1)], axis=1).astype(BF16)
    n_moe = moe_router.shape[0]
    router_pad = jnp.concatenate(
        [moe_router, jnp.zeros((n_moe, D_MODEL, LANES - N_EXPERTS), F32)], axis=-1).astype(BF16)
    router_b_pad = jnp.concatenate(
        [moe_router_b, jnp.full((n_moe, LANES - N_EXPERTS), -1e30, F32)], axis=-1)[:, None, :]
    log_gamma = jnp.log1p(-jnp.exp2(-5.0 - jnp.arange(H_A, dtype=F32)))
    row = lambda a: a.reshape(DEPTH, 1, -1)
    o_k, o_v, o_ik = 5 * W_B, 6 * W_B, 8 * W_B + H_I
    w_in_bf = [w_in[l].astype(BF16) for l in range(DEPTH)]
    w_t = lambda o, w: [wl[:, o:o + w] for wl in w_in_bf]
    zpad = jnp.zeros((D_MODEL, LANES - D_I), BF16)
    prm = dict(
        wk_t=w_t(o_k, W_B), wv_t=w_t(o_v, W_B),
        wi_t=[jnp.concatenate([wl[:, o_ik:o_ik + D_I], zpad], axis=1) for wl in w_in_bf],
        w_in=[_pack_w_in(wl) for wl in w_in_bf], lgam=jnp.repeat(log_gamma, DK_A)[None, :], ret_gn_g=row(ret_gn_g),
        rw_mu=row(rw_mu), rw_w0=row(rw_w0), rw_a0=row(rw_a0), rw_kk=row(rw_kk), rw_ka=row(rw_ka),
        rw_rk=row(rw_rk), rw_wwa=wwa, rw_g2=rw_g2.astype(BF16), rw_lnx_g=row(rw_lnx_g), rw_lnx_b=row(rw_lnx_b),
        w_branch=w_branch.astype(BF16), w_out=w_out.astype(BF16),
        ln1_g=row(ln1_g), ln1_b=row(ln1_b), ln2_g=row(ln2_g), ln2_b=row(ln2_b),
        ffn_w1=ffn_w1.astype(BF16), ffn_w3=ffn_w3.astype(BF16), ffn_w2=ffn_w2.astype(BF16),
        moe_router=router_pad, moe_router_b=router_b_pad,
        moe_w1=moe_w1.astype(BF16), moe_w3=moe_w3.astype(BF16), moe_w2=moe_w2.astype(BF16))

    ada = _ada_all(jnp.concatenate([c_prompt, c_sample], axis=0), ada_w.astype(BF16), ada_b)
    mods_p = [_mods(ada[l], 0, bp, None) for l in range(DEPTH)]
    mods_s = [_mods(ada[l], bp, bp + bs, ts) for l in range(DEPTH)]

    np_, ns_ = bp * tp, bs * ts
    tm_p = min(tp, TM_ROW)
    xp = x_prompt.reshape(np_, D_MODEL)
    xs = x_sample.reshape(ns_, D_MODEL)
    hp = _modulate(xp, mods_p[0][1], mods_p[0][0], tm_p, tp // tm_p)
    hs = _modulate(xs, mods_s[0][1], mods_s[0][0], ns_, 1)

    cache_k4 = cache_k.transpose(0, 1, 3, 4, 2).reshape(DEPTH, n_pool, W_B, PAGE_SIZE)
    cache_v4 = cache_v.transpose(0, 1, 3, 4, 2).reshape(DEPTH, n_pool, W_B, PAGE_SIZE)
    cache_kidx_t = cache_kidx.transpose(0, 1, 3, 2)
    zeros_ret = jnp.zeros((bp, H_A, DK_A, DV_A), F32)
    zeros_wkv = jnp.zeros((bp * (H_C // 2), D_C, 2 * D_C), F32)
    zeros_shift = jnp.zeros((bp, 1, D_C_IN), F32)

    st_p, st_s = [], []
    for l in range(DEPTH):
        nxt = min(l + 1, DEPTH - 1)
        attend_p = lambda p2: _dsa_prompt(p2, bp, tp)
        xp, hp, sp = _layer(l, xp, hp, mods_p[l], (mods_p[nxt][0], mods_p[nxt][1]), prm, attend_p,
                            zeros_ret, zeros_wkv, zeros_shift, bp, tp)

        def attend_s(p2, l=l):
            pad = lambda a: jnp.pad(a.reshape(bs, ts, -1).transpose(0, 2, 1),
                                    ((0, 0), (0, 0), (0, PAGE_SIZE - ts)))
            q3 = p2[:, U_QB * LANES:U_QB * LANES + W_B].reshape(bs, ts, W_B)
            iq32 = p2[:, U_IQ * LANES:U_IQ * LANES + W_B].reshape(bs, ts * H_I, D_I)
            iw = p2[:, U_MISC * LANES + IW_OFF:U_MISC * LANES + IW_OFF + H_I].reshape(bs, ts * H_I, 1)
            iwb = jnp.broadcast_to(iw, (bs, ts * H_I, LANES))
            knew = pad(p2[:, U_KB * LANES:U_KB * LANES + W_B])
            vnew = pad(p2[:, U_VB * LANES:U_VB * LANES + W_B])
            iknew = pad(p2[:, U_MISC * LANES:U_MISC * LANES + D_I])
            scores, scores_new = _dsa_sample_scores(l, page_table, cache_kidx_t, iq32, iwb, iknew, ts)
            bias, bias_new = _dsa_sample_select(scores, scores_new, ts)
            o = _dsa_sample_attend(l, page_table, bias, bias_new, q3, knew, vnew, cache_k4, cache_v4, ts)
            return o[:, :ts].reshape(ns_, W_B)

        shift_rows = jnp.repeat(state_shift[l], ts, axis=0)[None]
        xs, hs, ss = _layer(l, xs, hs, mods_s[l], (mods_s[nxt][0], mods_s[nxt][1]), prm, attend_s,
                            state_ret[l], _scan_state_in(state_wkv[l]), shift_rows, bs, ts)
        st_p.append(sp)
        st_s.append(ss)

    outs = [xp.reshape(bp, tp, D_MODEL), xs.reshape(bs, ts, D_MODEL)]
    for j in range(6):
        sp = jnp.stack([s[j] for s in st_p])
        if tp >= TM_PROJ and j < 2:
            sp = sp.reshape(DEPTH, bp, H_B, D_B, tp).transpose(0, 1, 4, 2, 3)
        elif tp >= TM_PROJ and j == 2:
            sp = sp.transpose(0, 1, 3, 2)
        outs.append(sp)
        outs.append(jnp.stack([s[j] for s in st_s]))
    return tuple(outs)
```

```python
import functools
import math

import jax
import jax.numpy as jnp
import numpy as np
from jax import lax
from jax.experimental import pallas as pl
from jax.experimental.pallas import tpu as pltpu

F32 = jnp.float32
BF16 = jnp.bfloat16
I32 = jnp.int32

D_MODEL = 1024
DEPTH = 4
PAGE_SIZE = 128
H_A, DK_A, DV_A = 4, 128, 128
CHUNK_A = 128
H_B, D_B = 8, 64
W_B = H_B * D_B
H_I, D_I = 8, 64
TOPK_MAX = 256
H_C, D_C = 8, 64
W_C = H_C * D_C
LORA_W, LORA_A, LORA_G = 64, 64, 128
RWKV_GN_EPS = 64e-5
N_BRANCH = 3
BRANCH_W = 512
D_FF = 2816
N_EXPERTS = 8
D_FF_E = 1408
ALPHA = (2.0 * DEPTH) ** 0.25
LN_EPS = 1e-5
D_C_IN = 3 * W_C + LORA_W + LORA_A + LORA_G

LANES = 128
SUBLANES = 8
VMEM_LIMIT = 56 * 1024 * 1024
TM_PROJ = 1024
TM_ROW = 512

U_QA, U_KA, U_VA, U_GA = 0, 4, 8, 12
U_QB, U_KB, U_VB, U_IQ = 16, 20, 24, 28
U_MISC = 32
U_PC = 34
U_GL = 48
N_UNITS = 72
PC_BLOCK_W = 2 * LANES
PC_BLOCKS = 7
D_PACK = N_UNITS * LANES
IW_OFF = 64

NEG_INF = float("-inf")


def _cp(sem, vmem=VMEM_LIMIT):
    return pltpu.CompilerParams(dimension_semantics=sem, vmem_limit_bytes=vmem)


def _dot(a, b):
    return jnp.dot(a, b, preferred_element_type=F32)


def _dot_nt(a, b):
    return lax.dot_general(a, b, (((1,), (1,)), ((), ())), preferred_element_type=F32)


def _dot_tn(a, b):
    return lax.dot_general(a, b, (((0,), (0,)), ((), ())), preferred_element_type=F32)


def _sigmoid(x):
    return 1.0 / (1.0 + jnp.exp(-x))


def _silu(x):
    return x * _sigmoid(x)


def _iota(shape, dim):
    return lax.broadcasted_iota(I32, shape, dim)


def _rows_to_tile(rows, width, fill=0.0):
    ri = _iota((SUBLANES, width), 0)
    out = jnp.full((SUBLANES, width), fill, F32)
    for t, r in enumerate(rows):
        out = jnp.where(ri == t, jnp.broadcast_to(r, (SUBLANES, width)), out)
    return out


def _ada_body(c_ref, w_ref, b_ref, o_ref):
    c = c_ref[...]
    o_ref[0] = _dot(_silu(c).astype(BF16), w_ref[0].astype(BF16)) + b_ref[0]


def _ada_all(c_all, ada_w, ada_b):
    m = c_all.shape[0]
    n = ada_w.shape[-1]
    tn = 1024
    return pl.pallas_call(
        _ada_body,
        grid=(DEPTH, n // tn),
        in_specs=[pl.BlockSpec((m, D_MODEL), lambda l, j: (0, 0)),
                  pl.BlockSpec((1, D_MODEL, tn), lambda l, j: (l, 0, j)),
                  pl.BlockSpec((1, 1, tn), lambda l, j: (l, 0, j))],
        out_specs=pl.BlockSpec((1, m, tn), lambda l, j: (l, 0, j)),
        out_shape=jax.ShapeDtypeStruct((DEPTH, m, n), F32),
        compiler_params=_cp(("arbitrary", "arbitrary")),
        name="ada",
    )(c_all, ada_w, ada_b.reshape(DEPTH, 1, n))


def _mod_spec(mod, tm, tiles_per_group):
    g, r, d = mod.shape
    if r == 1:
        return pl.BlockSpec((1, 1, d), lambda i, *_: (i // tiles_per_group, 0, 0))
    assert r == tm and g == 1
    return pl.BlockSpec((1, r, d), lambda i, *_: (0, 0, 0))


def _modulate_body(x_ref, sc_ref, sh_ref, h_ref):
    h_ref[...] = (x_ref[...] * (1.0 + sc_ref[0]) + sh_ref[0]).astype(BF16)


def _modulate(x, sc, sh, tm, tiles_per_group):
    n = x.shape[0]
    return pl.pallas_call(
        _modulate_body,
        grid=(n // tm,),
        in_specs=[pl.BlockSpec((tm, D_MODEL), lambda i: (i, 0)),
                  _mod_spec(sc, tm, tiles_per_group), _mod_spec(sh, tm, tiles_per_group)],
        out_specs=pl.BlockSpec((tm, D_MODEL), lambda i: (i, 0)),
        out_shape=jax.ShapeDtypeStruct((n, D_MODEL), BF16),
        compiler_params=_cp(("arbitrary",)),
        name="modulate",
    )(x, sc, sh)


def _mm_body(x_ref, w_ref, o_ref):
    o_ref[...] = _dot(x_ref[...], w_ref[...])


def _in_proj(h, w_pack, tm):
    n = h.shape[0]
    tn = 1024
    return pl.pallas_call(
        _mm_body,
        grid=(n // tm, D_PACK // tn),
        in_specs=[pl.BlockSpec((tm, D_MODEL), lambda i, j: (i, 0)),
                  pl.BlockSpec((D_MODEL, tn), lambda i, j: (0, j))],
        out_specs=pl.BlockSpec((tm, tn), lambda i, j: (i, j)),
        out_shape=jax.ShapeDtypeStruct((n, D_PACK), F32),
        compiler_params=_cp(("arbitrary", "arbitrary")),
        name="in_proj",
    )(h, w_pack)


def _kv_t_body(h_ref, wk_ref, wv_ref, wi_ref, k_ref, v_ref, i_ref):
    h = h_ref[...]
    k_ref[0] = _dot(h, wk_ref[...]).T
    v_ref[0] = _dot(h, wv_ref[...]).T
    i_ref[0] = _dot(h, wi_ref[...]).T[0:D_I]


def _kv_transposed(h, wk_t, wv_t, wi_t, nb, t_len, tt):
    nt = t_len // tt
    wspec = lambda w: pl.BlockSpec(w.shape, lambda b, t: (0, 0))
    ospec = lambda f: pl.BlockSpec((1, f, tt), lambda b, t: (b, 0, t))
    return pl.pallas_call(
        _kv_t_body,
        grid=(nb, nt),
        in_specs=[pl.BlockSpec((tt, D_MODEL), lambda b, t: (b * nt + t, 0)),
                  wspec(wk_t), wspec(wv_t), wspec(wi_t)],
        out_specs=[ospec(W_B), ospec(W_B), ospec(D_I)],
        out_shape=[jax.ShapeDtypeStruct((nb, W_B, t_len), F32), jax.ShapeDtypeStruct((nb, W_B, t_len), F32),
                   jax.ShapeDtypeStruct((nb, D_I, t_len), F32)],
        compiler_params=_cp(("arbitrary", "arbitrary")),
        name="kv_transposed",
    )(h, wk_t, wv_t, wi_t)


def _retention_body(q_ref, k_ref, v_ref, g_ref, s0_ref, lg_ref, gn_ref, o_ref, sout_ref,
                    s_scr, pad_scr, *, rb, nchunk):
    n = pl.program_id(1)
    c = CHUNK_A

    @pl.when(n == 0)
    def _():
        s_scr[...] = s0_ref[0]

    def padded(ref, slot):
        if rb == c:
            return ref[0]
        pad_scr[slot] = jnp.zeros((c, H_A * DK_A), F32)
        pad_scr[slot, 0:rb, :] = ref[0]
        return pad_scr[slot]

    q_all = padded(q_ref, 0)
    k_all = padded(k_ref, 1)
    v_all = padded(v_ref, 2)
    g_all = padded(g_ref, 3)

    row = _iota((c, c), 0).astype(F32)
    col = _iota((c, c), 1).astype(F32)
    rel = row - col
    outs = []
    for h in range(H_A):
        sl = slice(h * DK_A, (h + 1) * DK_A)
        lg = lg_ref[0:1, sl]
        q = (q_all[:, sl] * (DK_A ** -0.5)).astype(BF16)
        k = k_all[:, sl]
        v = v_all[:, sl].astype(BF16)
        s0 = s_scr[h]
        decay = jnp.where(rel >= 0.0, jnp.exp(lg * jnp.maximum(rel, 0.0)), 0.0)
        scores = _dot_nt(q, k.astype(BF16)) * decay
        inner = _dot(scores.astype(BF16), v)
        xi = jnp.exp(lg * (row + 1.0))
        cross = _dot(q, s0.astype(BF16)) * xi
        zeta = jnp.where(row < float(rb), jnp.exp(lg * (float(rb) - 1.0 - row)), 0.0)
        s_new = jnp.exp(lg * float(rb)) * s0 + _dot_tn((k * zeta).astype(BF16), v)
        s_scr[h] = s_new
        o = inner + cross
        mu = jnp.mean(o, axis=-1, keepdims=True)
        var = jnp.mean(jnp.square(o - mu), axis=-1, keepdims=True)
        o = (o - mu) * lax.rsqrt(var + LN_EPS) * gn_ref[0:1, sl]
        outs.append(_silu(g_all[:, sl]) * o)
    o_full = jnp.concatenate(outs, axis=-1)
    o_ref[0] = o_full[0:rb].astype(BF16)

    @pl.when(n == nchunk - 1)
    def _():
        sout_ref[0] = s_scr[...]


def _retention(p3, s0, lgam, gn_g, nb, nchunk, rb):
    w = H_A * DK_A
    col = lambda u: (lambda b, n: (b * nchunk + n, 0, u // 4))
    body = functools.partial(_retention_body, rb=rb, nchunk=nchunk)
    return pl.pallas_call(
        body,
        grid=(nb, nchunk),
        in_specs=[pl.BlockSpec((1, rb, w), col(U_QA)), pl.BlockSpec((1, rb, w), col(U_KA)),
                  pl.BlockSpec((1, rb, w), col(U_VA)), pl.BlockSpec((1, rb, w), col(U_GA)),
                  pl.BlockSpec((1, H_A, DK_A, DV_A), lambda b, n: (b, 0, 0, 0)),
                  pl.BlockSpec((1, w), lambda b, n: (0, 0)),
                  pl.BlockSpec((1, w), lambda b, n: (0, 0))],
        out_specs=[pl.BlockSpec((1, rb, w), lambda b, n: (b * nchunk + n, 0, 0)),
                   pl.BlockSpec((1, H_A, DK_A, DV_A), lambda b, n: (b, 0, 0, 0))],
        out_shape=[jax.ShapeDtypeStruct((nb * nchunk, rb, w), BF16),
                   jax.ShapeDtypeStruct((nb, H_A, DK_A, DV_A), F32)],
        scratch_shapes=[pltpu.VMEM((H_A, DK_A, DV_A), F32), pltpu.VMEM((4, CHUNK_A, w), F32)],
        compiler_params=_cp(("arbitrary", "arbitrary")),
        name="retention",
    )(p3, p3, p3, p3, s0, lgam, gn_g)


def _score_key(s):
    bits = pltpu.bitcast(s, I32)
    return jnp.where(bits < 0, bits ^ jnp.int32(0x7FFFFFFF), bits)


def _tri_incl():
    return jnp.where(_iota((LANES, LANES), 0) <= _iota((LANES, LANES), 1), 1.0, 0.0).astype(BF16)


KC = 512
DSA_QB = 256


def _dsa_prompt_body(q_ref, iq_ref, mq_ref, k_ref, v_ref, mk_ref, o_ref,
                     kx, vt, ikbf, keys, bias, q2, *, t_len, top):
    i = pl.program_id(1)
    nkc = t_len // KC
    qb = min(DSA_QB, t_len)

    @pl.when(i == 0)
    def _():
        prow = _iota((KC, LANES), 0)
        plane = _iota((KC, LANES), 1)
        for j in range(nkc):
            kpos = j * KC + prow
            posf = jnp.where(plane == 0, kpos // 64, jnp.where(plane == 1, kpos % 64,
                                                                jnp.where(plane == 2, 1, 0)))
            posf = posf.astype(F32).astype(BF16)
            for hp in range(H_B // 2):
                kx[j, :, hp * 2 * LANES:hp * 2 * LANES + LANES] = (
                    k_ref[j * KC:(j + 1) * KC, hp * LANES:(hp + 1) * LANES].astype(BF16))
                kx[j, :, hp * 2 * LANES + LANES:(hp + 1) * 2 * LANES] = posf
                vt[j, hp * LANES:(hp + 1) * LANES, :] = (
                    v_ref[j * KC:(j + 1) * KC, hp * LANES:(hp + 1) * LANES].T.astype(BF16))
            ikbf[j] = mk_ref[j * KC:(j + 1) * KC, :].astype(BF16)

    nsc = (i * qb) // KC + 1
    qpos = i * qb + _iota((KC, qb), 1)
    krow = _iota((KC, qb), 0)

    iq = (iq_ref[...] * (D_I ** -0.5)).astype(BF16)
    iw_t = mq_ref[...].T * (H_I ** -0.5)

    def score_chunk(j, carry):
        ik = ikbf[j][:, 0:D_I]
        acc = jnp.zeros((KC, qb), F32)
        for h in range(H_I):
            s = _dot_nt(ik, iq[:, h * D_I:(h + 1) * D_I])
            acc = acc + jnp.maximum(s, 0.0) * iw_t[IW_OFF + h:IW_OFF + h + 1]
        acc = jnp.where(j * KC + krow <= qpos, acc + 0.0, NEG_INF)
        keys[j] = _score_key(acc)
        return carry

    lax.fori_loop(0, nsc, score_chunk, 0)

    def count(pred):
        fold = 4 * SUBLANES

        def body(j, acc):
            m = jnp.where(pred(keys[j]), 1.0, 0.0).reshape(KC // fold, fold, qb)
            return acc + jnp.sum(m, axis=0)
        acc = lax.fori_loop(0, nsc, body, jnp.zeros((fold, qb), F32))
        return jnp.sum(acc, axis=0, keepdims=True)

    def count_ge(cand):
        return count(lambda kj: kj >= cand)

    lowest = jnp.full((1, qb), -2 ** 31, I32)

    def bisect():
        prefix = lowest
        for bit in range(31, -1, -1):
            step = jnp.int32(-2 ** 31) if bit == 31 else jnp.int32(1 << bit)
            cand = prefix + step
            cnt = count_ge(cand)
            prefix = jnp.where(cnt >= float(top), cand, prefix)
        return prefix

    neg_inf_key = _score_key(jnp.full((1, qb), NEG_INF, F32))
    tau = lax.cond((i + 1) * qb <= top, lambda: neg_inf_key, bisect)
    need = float(top) - count(lambda kj: kj > tau)

    tri_l = jnp.where(_iota((KC, KC), 1) <= _iota((KC, KC), 0), 1.0, 0.0).astype(BF16)

    def bias_chunk(j, carry):
        kj = keys[j]
        eq = jnp.where(kj == tau, 1.0, 0.0)
        pre = _dot(tri_l, eq.astype(BF16)) + carry
        keep = ((kj > tau) | ((eq > 0.0) & (pre <= need))) & (j * KC + krow <= qpos)
        bias[j] = jnp.where(keep, 0.0, NEG_INF)
        return carry + jnp.sum(eq, axis=0, keepdims=True)

    lax.fori_loop(0, nsc, bias_chunk, jnp.zeros((1, qb), F32))

    npair = H_B // 2
    q_all = q_ref[...]
    lane_q = _iota((qb, LANES), 1)
    base = (i * qb + 0 * lane_q).astype(F32)
    for hp in range(npair):
        qp = q_all[:, hp * LANES:(hp + 1) * LANES] * (D_B ** -0.5)
        for h2 in range(2):
            slope = 2.0 ** (-8.0 * (2 * hp + h2 + 1) / H_B)
            in_half = (lane_q >= h2 * D_B) & (lane_q < (h2 + 1) * D_B)
            feat = jnp.where(lane_q == 0, 64.0 * slope,
                             jnp.where(lane_q == 1, slope, jnp.where(lane_q == 2, -slope * base, 0.0)))
            q2[hp, h2 * qb:(h2 + 1) * qb, 0:LANES] = jnp.where(in_half, qp, 0.0).astype(BF16)
            q2[hp, h2 * qb:(h2 + 1) * qb, LANES:2 * LANES] = feat.astype(BF16)

    def attn_chunk(j, carry):
        ms, ls, accs = carry
        b2 = jnp.concatenate([bias[j], bias[j]], axis=1)
        out_m, out_l, out_acc = [], [], []
        for hp in range(npair):
            s = _dot_nt(kx[j, :, hp * 2 * LANES:(hp + 1) * 2 * LANES], q2[hp]) + b2
            m_new = jnp.maximum(ms[hp], jnp.max(s, axis=0, keepdims=True))
            m_safe = jnp.where(m_new == NEG_INF, 0.0, m_new)
            alpha = jnp.exp(ms[hp] - m_safe)
            p = jnp.exp(s - m_safe)
            out_m.append(m_new)
            out_l.append(ls[hp] * alpha + jnp.sum(p, axis=0, keepdims=True))
            out_acc.append(accs[hp] * alpha + _dot(vt[j, hp * LANES:(hp + 1) * LANES, :], p.astype(BF16)))
        return out_m, out_l, out_acc

    init = ([jnp.full((1, 2 * qb), NEG_INF, F32)] * npair, [jnp.zeros((1, 2 * qb), F32)] * npair,
            [jnp.zeros((LANES, 2 * qb), F32)] * npair)
    _, ls, accs = lax.fori_loop(0, nsc, attn_chunk, init)
    row_o = _iota((LANES, qb), 0)
    outs = []
    for hp in range(npair):
        o = accs[hp] / ls[hp]
        outs.append(jnp.where(row_o < D_B, o[:, 0:qb], o[:, qb:2 * qb]).T)
    o_ref[...] = jnp.concatenate(outs, axis=1).astype(BF16)


def _dsa_prompt(p2, nb, t_len):
    top = min(TOPK_MAX, t_len // 4)
    qb = min(DSA_QB, t_len)
    nq = t_len // qb
    nkc = t_len // KC
    body = functools.partial(_dsa_prompt_body, t_len=t_len, top=top)
    qspec = lambda u, wdt: pl.BlockSpec((qb, wdt), lambda b, i: (b * nq + i, u * LANES // wdt))
    aspec = lambda u, wdt: pl.BlockSpec((t_len, wdt), lambda b, i: (b, u * LANES // wdt))
    return pl.pallas_call(
        body,
        grid=(nb, nq),
        in_specs=[qspec(U_QB, W_B), qspec(U_IQ, W_B), qspec(U_MISC, LANES),
                  aspec(U_KB, W_B), aspec(U_VB, W_B), aspec(U_MISC, LANES)],
        out_specs=pl.BlockSpec((qb, W_B), lambda b, i: (b * nq + i, 0)),
        out_shape=jax.ShapeDtypeStruct((nb * t_len, W_B), BF16),
        scratch_shapes=[pltpu.VMEM((nkc, KC, 2 * W_B), BF16), pltpu.VMEM((nkc, W_B, KC), BF16),
                        pltpu.VMEM((nkc, KC, LANES), BF16),
                        pltpu.VMEM((nkc, KC, qb), I32), pltpu.VMEM((nkc, KC, qb), F32),
                        pltpu.VMEM((H_B // 2, 2 * qb, 2 * LANES), BF16)],
        compiler_params=_cp(("arbitrary", "arbitrary")),
        name="dsa_prompt",
    )(p2, p2, p2, p2, p2, p2)


PG_STEP = 32


def _dsa_s_score_body(pt_ref, iq_ref, iwb_ref, iknew_ref, *rest, t_len):
    pages = rest[:PG_STEP]
    o_ref, onew_ref = rest[PG_STEP:PG_STEP + 2]
    iq = (iq_ref[0] * (D_I ** -0.5)).astype(BF16)
    iwb = iwb_ref[0] * (H_I ** -0.5)
    row = _iota((SUBLANES, LANES), 0)
    lane = _iota((SUBLANES, LANES), 1)

    def scores(ikt):
        s = jnp.maximum(_dot(iq, ikt.astype(BF16)), 0.0) * iwb
        per_t = [jnp.sum(s[t * H_I:(t + 1) * H_I], axis=0, keepdims=True) for t in range(t_len)]
        return _rows_to_tile(per_t, LANES) + 0.0

    for j in range(PG_STEP):
        o_ref[0, j] = jnp.where(row < t_len, scores(pages[j][0, 0]), NEG_INF)

    @pl.when(pl.program_id(1) == 0)
    def _():
        onew_ref[0, 0] = jnp.where((row < t_len) & (lane <= row), scores(iknew_ref[0]), NEG_INF)


def _dsa_sample_scores(layer, page_table, cache_kidx_t, iq32, iwb, iknew, t_len):
    nb, n_pages = page_table.shape
    nsteps = n_pages // PG_STEP
    rows = t_len * H_I

    def page_spec(j):
        return pl.BlockSpec((1, 1, D_I, PAGE_SIZE),
                            lambda b, s, pt: (layer, pt[b, s * PG_STEP + j], 0, 0))

    grid_spec = pltpu.PrefetchScalarGridSpec(
        num_scalar_prefetch=1,
        grid=(nb, nsteps),
        in_specs=[pl.BlockSpec((1, rows, D_I), lambda b, s, pt: (b, 0, 0)),
                  pl.BlockSpec((1, rows, LANES), lambda b, s, pt: (b, 0, 0)),
                  pl.BlockSpec((1, D_I, PAGE_SIZE), lambda b, s, pt: (b, 0, 0))]
                 + [page_spec(j) for j in range(PG_STEP)],
        out_specs=[pl.BlockSpec((1, PG_STEP, SUBLANES, LANES), lambda b, s, pt: (b, s, 0, 0)),
                   pl.BlockSpec((1, 1, SUBLANES, LANES), lambda b, s, pt: (b, 0, 0, 0))],
    )
    return pl.pallas_call(
        functools.partial(_dsa_s_score_body, t_len=t_len),
        grid_spec=grid_spec,
        out_shape=[jax.ShapeDtypeStruct((nb, n_pages, SUBLANES, LANES), F32),
                   jax.ShapeDtypeStruct((nb, 1, SUBLANES, LANES), F32)],
        compiler_params=_cp(("arbitrary", "arbitrary")),
        name="dsa_sample_scores",
    )(page_table, iq32, iwb, iknew, *([cache_kidx_t] * PG_STEP))


SEL_B = 8


def _dsa_s_select_body(sc_ref, scn_ref, bias_ref, biasn_ref, *, t_len, n_pages, top):
    row8 = _iota((SUBLANES, LANES), 0)
    lane8 = _iota((SUBLANES, LANES), 1)
    keys_p = [_score_key(sc_ref[b]) for b in range(SEL_B)]
    keys_n = [_score_key(scn_ref[b, 0]) for b in range(SEL_B)]

    def count(pred_past, pred_new):
        c = jnp.sum(jnp.where(pred_past, 1.0, 0.0), axis=0) + jnp.where(pred_new, 1.0, 0.0)
        return jnp.sum(c, axis=1, keepdims=True)

    prefix = [jnp.full((SUBLANES, 1), -2 ** 31, I32)] * SEL_B
    for bit in range(31, -1, -1):
        step = jnp.int32(-2 ** 31) if bit == 31 else jnp.int32(1 << bit)
        nxt = []
        for b in range(SEL_B):
            cand = prefix[b] + step
            cnt = count(keys_p[b] >= cand[None], keys_n[b] >= cand)
            nxt.append(jnp.where(cnt >= float(top), cand, prefix[b]))
        prefix = nxt

    tri = _tri_incl()
    ones = jnp.ones((LANES, LANES), BF16)
    for b in range(SEL_B):
        tau = prefix[b]
        need = float(top) - count(keys_p[b] > tau[None], keys_n[b] > tau)
        eq = jnp.where(keys_p[b] == tau[None], 1.0, 0.0)
        eq2 = eq.reshape(n_pages * SUBLANES, LANES).astype(BF16)
        pre_in = _dot(eq2, tri).reshape(n_pages, SUBLANES, LANES)
        totals = _dot(eq2, ones).reshape(n_pages, SUBLANES, LANES)
        carry = jnp.zeros((SUBLANES, LANES), F32)
        for p in range(n_pages):
            keep = (keys_p[b][p] > tau) | ((eq[p] > 0.0) & (pre_in[p] + carry <= need))
            bias_ref[b, p] = jnp.where(keep & (row8 < t_len), 0.0, NEG_INF)
            carry = carry + totals[p]
        e = jnp.where(keys_n[b] == tau, 1.0, 0.0)
        pre = _dot(e.astype(BF16), tri) + carry
        keep = (keys_n[b] > tau) | ((e > 0.0) & (pre <= need))
        biasn_ref[b, 0] = jnp.where(keep & (row8 < t_len) & (lane8 <= row8), 0.0, NEG_INF)


def _dsa_sample_select(scores, scores_new, t_len):
    nb, n_pages = scores.shape[:2]
    top = min(TOPK_MAX, (n_pages * PAGE_SIZE + t_len) // 4)
    past = pl.BlockSpec((SEL_B, n_pages, SUBLANES, LANES), lambda c: (c, 0, 0, 0))
    new = pl.BlockSpec((SEL_B, 1, SUBLANES, LANES), lambda c: (c, 0, 0, 0))
    return pl.pallas_call(
        functools.partial(_dsa_s_select_body, t_len=t_len, n_pages=n_pages, top=top),
        grid=(nb // SEL_B,),
        in_specs=[past, new],
        out_specs=[past, new],
        out_shape=[jax.ShapeDtypeStruct(scores.shape, F32), jax.ShapeDtypeStruct(scores_new.shape, F32)],
        compiler_params=_cp(("arbitrary",)),
        name="dsa_sample_select",
    )(scores, scores_new)


def _dsa_s_attn_body(pt_ref, bias_ref, biasn_ref, q_ref, knew_ref, vnew_ref, *rest, t_len, n_pages):
    kpages = rest[:PG_STEP]
    vpages = rest[PG_STEP:2 * PG_STEP]
    o_ref = rest[2 * PG_STEP]
    m_scr, l_scr, acc_scr = rest[2 * PG_STEP + 1:]
    s_idx = pl.program_id(1)
    nsteps = n_pages // PG_STEP
    past = n_pages * PAGE_SIZE
    rows = t_len * H_B

    @pl.when(s_idx == 0)
    def _():
        m_scr[...] = jnp.full((rows, LANES), NEG_INF, F32)
        l_scr[...] = jnp.zeros((rows, LANES), F32)
        acc_scr[...] = jnp.zeros((rows, W_B), F32)

    q = q_ref[0]
    col = _iota((SUBLANES, W_B), 1)
    hrow = _iota((SUBLANES, W_B), 0)
    blockmask = (col // D_B) == hrow
    qexp = jnp.concatenate(
        [jnp.where(blockmask, jnp.broadcast_to(q[t:t + 1], (SUBLANES, W_B)), 0.0) for t in range(t_len)],
        axis=0).astype(BF16)
    hvec = _iota((rows, LANES), 0) % H_B
    slope = jnp.exp2(-8.0 * (hvec.astype(F32) + 1.0) / H_B)
    tvec = _iota((rows, LANES), 0) // H_B
    qpos = past + tvec
    lane = _iota((rows, LANES), 1)

    def expand_bias(bp):
        return jnp.concatenate([jnp.broadcast_to(bp[t:t + 1], (SUBLANES, LANES)) for t in range(t_len)],
                               axis=0)

    def attend(k_list, v_list, bias_list, kpos0_list):
        logits = []
        for kp, bp, kpos0 in zip(k_list, bias_list, kpos0_list):
            s = _dot(qexp, kp.astype(BF16)) * (D_B ** -0.5)
            dist = (qpos - (kpos0 + lane)).astype(F32)
            logits.append(s - slope * dist + expand_bias(bp))
        m_old = m_scr[...]
        m_new = m_old
        for s in logits:
            m_new = jnp.maximum(m_new, jnp.max(s, axis=1, keepdims=True))
        m_safe = jnp.where(m_new == NEG_INF, 0.0, m_new)
        scale = jnp.exp(m_old - m_safe)
        l = l_scr[...] * scale
        acc = acc_scr[...] * scale[:, 0:1]
        for s, vp in zip(logits, v_list):
            p = jnp.exp(s - m_safe)
            l = l + jnp.sum(p, axis=1, keepdims=True)
            acc = acc + _dot_nt(p.astype(BF16), vp.astype(BF16))
        m_scr[...] = m_new
        l_scr[...] = l
        acc_scr[...] = acc

    @pl.when(s_idx < nsteps)
    def _():
        base = s_idx * PG_STEP
        attend([kpages[j][0, 0] for j in range(PG_STEP)],
               [vpages[j][0, 0] for j in range(PG_STEP)],
               [bias_ref[0, base + j] for j in range(PG_STEP)],
               [(base + j) * PAGE_SIZE for j in range(PG_STEP)])

    @pl.when(s_idx == nsteps)
    def _():
        attend([knew_ref[0]], [vnew_ref[0]], [biasn_ref[0, 0]], [past])
        o = acc_scr[...] / l_scr[...][:, 0:1]
        outs = [jnp.sum(jnp.where(blockmask, o[t * H_B:(t + 1) * H_B], 0.0), axis=0, keepdims=True)
                for t in range(t_len)]
        o_ref[0] = _rows_to_tile(outs, W_B).astype(BF16)


def _dsa_sample_attend(layer, page_table, bias, bias_new, q3, knew, vnew, cache_k4, cache_v4, t_len):
    nb, n_pages = page_table.shape
    nsteps = n_pages // PG_STEP
    rows = t_len * H_B

    def page_spec(j):
        def imap(b, s, pt):
            return (layer, pt[b, jnp.minimum(s, nsteps - 1) * PG_STEP + j], 0, 0)
        return pl.BlockSpec((1, 1, W_B, PAGE_SIZE), imap)

    per_b = lambda shape: pl.BlockSpec((1,) + shape, lambda b, s, pt: (b,) + (0,) * len(shape))
    grid_spec = pltpu.PrefetchScalarGridSpec(
        num_scalar_prefetch=1,
        grid=(nb, nsteps + 1),
        in_specs=[per_b((n_pages, SUBLANES, LANES)), per_b((1, SUBLANES, LANES)), per_b((t_len, W_B)),
                  per_b((W_B, PAGE_SIZE)), per_b((W_B, PAGE_SIZE))]
                 + [page_spec(j) for j in range(PG_STEP)] + [page_spec(j) for j in range(PG_STEP)],
        out_specs=per_b((SUBLANES, W_B)),
        scratch_shapes=[pltpu.VMEM((rows, LANES), F32), pltpu.VMEM((rows, LANES), F32),
                        pltpu.VMEM((rows, W_B), F32)],
    )
    body = functools.partial(_dsa_s_attn_body, t_len=t_len, n_pages=n_pages)
    return pl.pallas_call(
        body,
        grid_spec=grid_spec,
        out_shape=jax.ShapeDtypeStruct((nb, SUBLANES, W_B), BF16),
        compiler_params=_cp(("arbitrary", "arbitrary")),
        name="dsa_sample_attend",
    )(page_table, bias, bias_new, q3, knew, vnew, *([cache_k4] * PG_STEP), *([cache_v4] * PG_STEP))


def _head_block_ones(n):
    return jnp.where((_iota((n, n), 0) // D_C) == (_iota((n, n), 1) // D_C), 1.0, 0.0).astype(BF16)


def _rwkv_prep_body(*refs, tm, seq_len):
    pc_refs = refs[:PC_BLOCKS]
    (sh_ref, mu_ref, w0_ref, a0_ref, kk_ref, ka_ref, rk_ref, wwa_ref, g2_ref,
     r_ref, w_ref, k_ref, v_ref, na_ref, b_ref, g_ref, bonus_ref, carry) = refs[PC_BLOCKS:]
    pc = jnp.concatenate([r[...] for r in pc_refs], axis=1)
    rolled = pltpu.roll(pc, 1, 0)
    row = _iota(pc.shape, 0)
    if seq_len >= tm:
        tiles_per_seq = seq_len // tm
        first = (pl.program_id(0) % tiles_per_seq) == 0
        head_row = jnp.where(first, sh_ref[0], carry[...])
        prev = jnp.where(row == 0, head_row, rolled)
        carry[...] = pc[tm - 1:tm, :]
    else:
        prev = jnp.where((row % seq_len) == 0, sh_ref[0], rolled)
    pm = pc + (prev - pc) * mu_ref[...]
    r = pm[:, 0:W_C]
    k = pm[:, W_C:2 * W_C]
    v = pm[:, 2 * W_C:3 * W_C]
    lwa = pm[:, 3 * W_C:3 * W_C + LANES]
    lg = pm[:, 3 * W_C + LANES:3 * W_C + 2 * LANES]
    lane = _iota(lwa.shape, 1)
    lwa = jnp.where(lane < LORA_W, jnp.tanh(lwa), lwa)
    wa = _dot(lwa.astype(BF16), wwa_ref[...])
    x = -(w0_ref[...] + wa[:, 0:W_C])
    softplus = jnp.maximum(x, 0.0) + jnp.log1p(jnp.exp(-jnp.abs(x)))
    w_raw = -softplus - 0.5
    a = _sigmoid(a0_ref[...] + wa[:, W_C:2 * W_C])
    g = _dot(_sigmoid(lg).astype(BF16), g2_ref[...])
    ones = _head_block_ones(W_C)
    kk = k * kk_ref[...]
    ss = _dot((kk * kk).astype(BF16), ones)
    kk = kk / jnp.maximum(jnp.sqrt(ss), 1e-12)
    k2 = k * (1.0 + (a - 1.0) * ka_ref[...])
    r_ref[...] = r
    w_ref[...] = jnp.exp(-jnp.exp(w_raw))
    k_ref[...] = k2
    v_ref[...] = v
    na_ref[...] = -kk
    b_ref[...] = kk * a
    g_ref[...] = g
    bonus_ref[...] = _dot((r * k2 * rk_ref[...]).astype(BF16), ones) * v


def _rwkv_prep(p2, shift_rows, prm, tm, seq_len):
    n = p2.shape[0]
    out_rm = pl.BlockSpec((tm, W_C), lambda i: (i, 0))
    shape_rm = jax.ShapeDtypeStruct((n, W_C), F32)
    if seq_len >= tm:
        tiles_per_seq = seq_len // tm
        sh_spec = pl.BlockSpec((1, 1, D_C_IN), lambda i: (i // tiles_per_seq, 0, 0))
        out_scan = pl.BlockSpec((tm, W_C), lambda i: (i % tiles_per_seq, i // tiles_per_seq))
        shape_scan = jax.ShapeDtypeStruct((seq_len, (n // seq_len) * W_C), F32)
    else:
        sh_spec = pl.BlockSpec((1, tm, D_C_IN), lambda i: (i, 0, 0))
        out_scan, shape_scan = out_rm, shape_rm
    row = lambda w: pl.BlockSpec((1, w), lambda i: (0, 0))
    body = functools.partial(_rwkv_prep_body, tm=tm, seq_len=seq_len)
    pc_block0 = U_PC * LANES // PC_BLOCK_W

    def pc_spec(j):
        return pl.BlockSpec((tm, PC_BLOCK_W), lambda i: (i, pc_block0 + j))

    return pl.pallas_call(
        body,
        grid=(n // tm,),
        in_specs=[pc_spec(j) for j in range(PC_BLOCKS)]
                 + [sh_spec, row(D_C_IN), row(W_C), row(W_C), row(W_C), row(W_C), row(W_C),
                    pl.BlockSpec((LANES, 2 * W_C), lambda i: (0, 0)),
                    pl.BlockSpec((LORA_G, W_C), lambda i: (0, 0))],
        out_specs=[out_scan] * 6 + [out_rm] * 2,
        out_shape=[shape_scan] * 6 + [shape_rm] * 2,
        scratch_shapes=[pltpu.VMEM((1, D_C_IN), F32)],
        compiler_params=_cp(("arbitrary",)),
        name="rwkv_prep",
    )(*([p2] * PC_BLOCKS), shift_rows, prm["mu"], prm["w0"], prm["a0"], prm["kk"], prm["ka"], prm["rk"],
      prm["wwa"], prm["g2"])


SCAN_B = 8
SCAN_TB = 128


def _rwkv_scan_body(r_ref, w_ref, k_ref, v_ref, na_ref, b_ref, s0_ref, y_ref, sout_ref, s_scr, ybuf,
                    *, tb, ntb):
    tblk = pl.program_id(1)

    @pl.when(tblk == 0)
    def _():
        s_scr[...] = s0_ref[...]

    ones1 = _head_block_ones(LANES)
    ones2 = _head_block_ones(2 * LANES)
    vi = _iota((D_C, LANES), 0)
    li = _iota((D_C, LANES), 1)
    irep = jnp.where((li % D_C) == vi, 1.0, 0.0)
    ngrp = SCAN_B * (H_C // 2)

    def vec(x_row, g):
        return x_row[:, g * LANES:(g + 1) * LANES]

    def y_row(yb_all):
        return jnp.concatenate(
            [jnp.sum(yb_all[g * D_C:(g + 1) * D_C] * irep, axis=0, keepdims=True) for g in range(ngrp)], axis=1)

    def advance(w_t, k_t, v_t, na_t, b_t, r_p):
        s_old = [s_scr[g] for g in range(ngrp)]
        na_b, r_b = na_t.astype(BF16), r_p.astype(BF16)
        s_b = [s.astype(BF16) for s in s_old]
        lhs1 = jnp.concatenate(
            [jnp.concatenate([s_b[g] * vec(na_b, g), s_b[g] * vec(r_b, g)], axis=1) for g in range(ngrp)],
            axis=0)
        res1 = _dot(lhs1, ones2)
        lhs2 = jnp.concatenate(
            [jnp.concatenate([(irep * vec(v_t, g)).astype(BF16), (irep * vec(v_t, g + 1)).astype(BF16)], axis=1)
             for g in range(0, ngrp, 2)], axis=0)
        res2 = _dot(lhs2, ones2)
        for g in range(ngrp):
            sa = res1[g * D_C:(g + 1) * D_C, 0:LANES]
            pr, half = divmod(g, 2)
            vb = res2[pr * D_C:(pr + 1) * D_C, half * LANES:(half + 1) * LANES]
            s_scr[g] = s_old[g] * vec(w_t, g) + sa * vec(b_t, g) + vb * vec(k_t, g)
        return y_row(res1[:, LANES:2 * LANES])

    def final_y(r_last):
        lhs = jnp.concatenate([(s_scr[g] * vec(r_last, g)).astype(BF16) for g in range(ngrp)], axis=0)
        return y_row(_dot(lhs, ones1))

    refs = (w_ref, k_ref, v_ref, na_ref, b_ref)
    if tb % SUBLANES == 0:
        ybuf[...] = jnp.zeros(ybuf.shape, F32)

        def tile_step(t8, carry):
            base = pl.multiple_of(t8 * SUBLANES, SUBLANES)
            prev = pl.multiple_of(jnp.maximum(t8 - 1, 0) * SUBLANES, SUBLANES)
            tiles = [ref[pl.ds(base, SUBLANES), :] for ref in refs]
            r_tile = r_ref[pl.ds(base, SUBLANES), :]
            r_prev_tile = r_ref[pl.ds(prev, SUBLANES), :]
            for tt in range(SUBLANES):
                r_p = r_prev_tile[SUBLANES - 1:SUBLANES] if tt == 0 else r_tile[tt - 1:tt]
                y_p = advance(*[x[tt:tt + 1] for x in tiles], r_p)
                if tt == 0:
                    ybuf[SUBLANES - 1:SUBLANES, :] = y_p
                    y_ref[pl.ds(prev, SUBLANES), :] = ybuf[...]
                else:
                    ybuf[tt - 1:tt, :] = y_p
            return carry

        lax.fori_loop(0, tb // SUBLANES, tile_step, 0)
        ybuf[SUBLANES - 1:SUBLANES, :] = final_y(r_ref[tb - 1:tb, :])
        y_ref[tb - SUBLANES:tb, :] = ybuf[...]
    else:
        for t in range(tb):
            y_p = advance(*[ref[t:t + 1, :] for ref in refs], r_ref[max(t - 1, 0):max(t - 1, 0) + 1, :])
            if t > 0:
                y_ref[t - 1:t, :] = y_p
        y_ref[tb - 1:tb, :] = final_y(r_ref[tb - 1:tb, :])

    @pl.when(tblk == ntb - 1)
    def _():
        sout_ref[...] = s_scr[...]


def _rwkv_scan(r, w, k, v, na, b, s0, nb, t_len, tb):
    ntb = t_len // tb
    npair = H_C // 2
    blk = pl.BlockSpec((tb, SCAN_B * W_C), lambda c, t: (t, c))
    sblk = pl.BlockSpec((SCAN_B * npair, D_C, LANES), lambda c, t: (c, 0, 0))
    body = functools.partial(_rwkv_scan_body, tb=tb, ntb=ntb)
    return pl.pallas_call(
        body,
        grid=(nb // SCAN_B, ntb),
        in_specs=[blk] * 6 + [sblk],
        out_specs=[blk, sblk],
        out_shape=[jax.ShapeDtypeStruct((t_len, nb * W_C), F32),
                   jax.ShapeDtypeStruct((nb * npair, D_C, LANES), F32)],
        scratch_shapes=[pltpu.VMEM((SCAN_B * npair, D_C, LANES), F32),
                        pltpu.VMEM((SUBLANES, SCAN_B * W_C), F32)],
        compiler_params=_cp(("arbitrary", "arbitrary")),
        name="rwkv_scan",
    )(r, w, k, v, na, b, s0)


def _rwkv_post_body(y_ref, g_ref, bonus_ref, lg_ref, lb_ref, o_ref):
    y = y_ref[...]
    ones = _head_block_ones(W_C)
    mu = _dot(y.astype(BF16), ones) * (1.0 / D_C)
    d = y - mu
    var = _dot((d * d).astype(BF16), ones) * (1.0 / D_C)
    yn = d * lax.rsqrt(var + RWKV_GN_EPS) * lg_ref[...] + lb_ref[...]
    o_ref[...] = ((yn + bonus_ref[...]) * g_ref[...]).astype(BF16)


def _rwkv_post(y, g, bonus, lnx_g, lnx_b, tm, seq_len):
    n = g.shape[0]
    blk = pl.BlockSpec((tm, W_C), lambda i: (i, 0))
    row = pl.BlockSpec((1, W_C), lambda i: (0, 0))
    if seq_len >= tm:
        tiles_per_seq = seq_len // tm
        yblk = pl.BlockSpec((tm, W_C), lambda i: (i % tiles_per_seq, i // tiles_per_seq))
    else:
        yblk = blk
    return pl.pallas_call(
        _rwkv_post_body,
        grid=(n // tm,),
        in_specs=[yblk, blk, blk, row, row],
        out_specs=blk,
        out_shape=jax.ShapeDtypeStruct((n, W_C), BF16),
        compiler_params=_cp(("arbitrary",)),
        name="rwkv_post",
    )(y, g, bonus, lnx_g, lnx_b)


def _layer_norm(x, g, b):
    mu = jnp.mean(x, axis=-1, keepdims=True)
    var = jnp.mean(jnp.square(x - mu), axis=-1, keepdims=True)
    return (x - mu) * lax.rsqrt(var + LN_EPS) * g + b


def _merge_body(oa_ref, ob_ref, oc_ref, gla_ref, glb_ref, glc_ref, x_ref, wb_ref, wo_ref,
                gt_ref, sc_ref, sh_ref, lg_ref, lb_ref, x1_ref, h_ref):
    mixed = None
    for gidx, (o_ref, gl_ref) in enumerate(((oa_ref, gla_ref), (ob_ref, glb_ref), (oc_ref, glc_ref))):
        br = _dot(o_ref[...], wb_ref[0, gidx])
        term = _sigmoid(gl_ref[...]) * br
        mixed = term if mixed is None else mixed + term
    y = _dot(mixed.astype(BF16), wo_ref[0])
    x1 = _layer_norm(ALPHA * x_ref[...] + gt_ref[0] * y, lg_ref[...], lb_ref[...])
    x1_ref[...] = x1
    h_ref[...] = (x1 * (1.0 + sc_ref[0]) + sh_ref[0]).astype(BF16)


def _merge(oa, ob, oc, p2, x, wb, wo, li, gt, sc, sh, ln_g, ln_b, tm, tiles_per_group):
    n = x.shape[0]
    blk = lambda w: pl.BlockSpec((tm, w), lambda i: (i, 0))
    row = pl.BlockSpec((1, D_MODEL), lambda i: (0, 0))
    ms = lambda m: _mod_spec(m, tm, tiles_per_group)
    gl_block0 = U_GL * LANES // D_MODEL
    return pl.pallas_call(
        _merge_body,
        grid=(n // tm,),
        in_specs=[blk(BRANCH_W), blk(BRANCH_W), blk(BRANCH_W)]
                 + [pl.BlockSpec((tm, D_MODEL), lambda i, g=g: (i, gl_block0 + g)) for g in range(N_BRANCH)]
                 + [blk(D_MODEL),
                  pl.BlockSpec((1, N_BRANCH, BRANCH_W, D_MODEL), lambda i: (li, 0, 0, 0)),
                  pl.BlockSpec((1, D_MODEL, D_MODEL), lambda i: (li, 0, 0)),
                  ms(gt), ms(sc), ms(sh), row, row],
        out_specs=[blk(D_MODEL), blk(D_MODEL)],
        out_shape=[jax.ShapeDtypeStruct((n, D_MODEL), F32), jax.ShapeDtypeStruct((n, D_MODEL), BF16)],
        compiler_params=_cp(("arbitrary",)),
        name="merge",
    )(oa, ob, oc, p2, p2, p2, x, wb, wo, gt, sc, sh, ln_g, ln_b)


def _ffn_finish(acc, x_ref, gt_ref, sc_ref, sh_ref, lg_ref, lb_ref, x2_ref, h_ref):
    x2 = _layer_norm(ALPHA * x_ref[...] + gt_ref[0] * acc, lg_ref[...], lb_ref[...])
    x2_ref[...] = x2
    h_ref[...] = (x2 * (1.0 + sc_ref[0]) + sh_ref[0]).astype(BF16)


def _ffn_body(h_ref, x_ref, w1_ref, w3_ref, w2_ref, gt_ref, sc_ref, sh_ref, lg_ref, lb_ref,
              x2_ref, hn_ref, acc, *, nf):
    f = pl.program_id(1)
    h = h_ref[...]
    u = (_silu(_dot(h, w1_ref[0])) * _dot(h, w3_ref[0])).astype(BF16)
    part = _dot(u, w2_ref[0])

    @pl.when(f == 0)
    def _():
        acc[...] = part

    @pl.when(f > 0)
    def _():
        acc[...] = acc[...] + part

    @pl.when(f == nf - 1)
    def _():
        _ffn_finish(acc[...], x_ref, gt_ref, sc_ref, sh_ref, lg_ref, lb_ref, x2_ref, hn_ref)


def _ffn_dense(h, x, w1, w3, w2, li, gt, sc, sh, ln_g, ln_b, tm, tiles_per_group):
    n = x.shape[0]
    tf = D_FF // 2
    nf = D_FF // tf
    blk = pl.BlockSpec((tm, D_MODEL), lambda i, f: (i, 0))
    row = pl.BlockSpec((1, D_MODEL), lambda i, f: (0, 0))
    ms = lambda m: _mod_spec(m, tm, tiles_per_group)
    return pl.pallas_call(
        functools.partial(_ffn_body, nf=nf),
        grid=(n // tm, nf),
        in_specs=[blk, blk,
                  pl.BlockSpec((1, D_MODEL, tf), lambda i, f: (li, 0, f)),
                  pl.BlockSpec((1, D_MODEL, tf), lambda i, f: (li, 0, f)),
                  pl.BlockSpec((1, tf, D_MODEL), lambda i, f: (li, f, 0)),
                  ms(gt), ms(sc), ms(sh), row, row],
        out_specs=[blk, blk],
        out_shape=[jax.ShapeDtypeStruct((n, D_MODEL), F32), jax.ShapeDtypeStruct((n, D_MODEL), BF16)],
        scratch_shapes=[pltpu.VMEM((tm, D_MODEL), F32)],
        compiler_params=_cp(("arbitrary", "arbitrary")),
        name="ffn_dense",
    )(h, x, w1, w3, w2, gt, sc, sh, ln_g, ln_b)


def _moe_body(h_ref, x_ref, rw_ref, rb_ref, w1_ref, w3_ref, w2_ref, gt_ref, sc_ref, sh_ref, lg_ref, lb_ref,
              x2_ref, hn_ref, acc, gate):
    e = pl.program_id(1)
    h = h_ref[...]
    lane = _iota((h.shape[0], LANES), 1)

    @pl.when(e == 0)
    def _():
        logits = _dot(h, rw_ref[...]) + rb_ref[...]
        m1 = jnp.max(logits, axis=1, keepdims=True)
        i1 = jnp.min(jnp.where(logits == m1, lane, LANES), axis=1, keepdims=True)
        rest = jnp.where(lane == i1, NEG_INF, logits)
        m2 = jnp.max(rest, axis=1, keepdims=True)
        i2 = jnp.min(jnp.where(rest == m2, lane, LANES), axis=1, keepdims=True)
        e2 = jnp.exp(m2 - m1)
        p1 = 1.0 / (1.0 + e2)
        p2 = e2 / (1.0 + e2)
        gate[...] = jnp.where(lane == i1, p1, 0.0) + jnp.where(lane == i2, p2, 0.0)
        acc[...] = jnp.zeros(acc.shape, F32)

    ge = jnp.sum(jnp.where(lane == e, gate[...], 0.0), axis=1, keepdims=True)
    u = (_silu(_dot(h, w1_ref[0, 0])) * _dot(h, w3_ref[0, 0])).astype(BF16)
    acc[...] = acc[...] + ge * _dot(u, w2_ref[0, 0])

    @pl.when(e == N_EXPERTS - 1)
    def _():
        _ffn_finish(acc[...], x_ref, gt_ref, sc_ref, sh_ref, lg_ref, lb_ref, x2_ref, hn_ref)


def _ffn_moe(h, x, rw, rb, w1, w3, w2, li, gt, sc, sh, ln_g, ln_b, tm, tiles_per_group):
    n = x.shape[0]
    blk = pl.BlockSpec((tm, D_MODEL), lambda i, e: (i, 0))
    row = pl.BlockSpec((1, D_MODEL), lambda i, e: (0, 0))
    ms = lambda m: _mod_spec(m, tm, tiles_per_group)
    return pl.pallas_call(
        _moe_body,
        grid=(n // tm, N_EXPERTS),
        in_specs=[blk, blk,
                  pl.BlockSpec((D_MODEL, LANES), lambda i, e: (0, 0)),
                  pl.BlockSpec((1, LANES), lambda i, e: (0, 0)),
                  pl.BlockSpec((1, 1, D_MODEL, D_FF_E), lambda i, e: (li, e, 0, 0)),
                  pl.BlockSpec((1, 1, D_MODEL, D_FF_E), lambda i, e: (li, e, 0, 0)),
                  pl.BlockSpec((1, 1, D_FF_E, D_MODEL), lambda i, e: (li, e, 0, 0)),
                  ms(gt), ms(sc), ms(sh), row, row],
        out_specs=[blk, blk],
        out_shape=[jax.ShapeDtypeStruct((n, D_MODEL), F32), jax.ShapeDtypeStruct((n, D_MODEL), BF16)],
        scratch_shapes=[pltpu.VMEM((tm, D_MODEL), F32), pltpu.VMEM((tm, LANES), F32)],
        compiler_params=_cp(("arbitrary", "arbitrary")),
        name="ffn_moe",
    )(h, x, rw, rb, w1, w3, w2, gt, sc, sh, ln_g, ln_b)


def _pack_w_in(w):
    o_iw = 8 * 512
    o_ik = o_iw + H_I
    o_pc = o_ik + D_I
    o_gl = o_pc + D_C_IN
    z = lambda n: jnp.zeros((w.shape[0], n), w.dtype)
    packed = jnp.concatenate(
        [w[:, :o_iw], w[:, o_ik:o_pc], w[:, o_iw:o_ik], z(LANES - D_I - H_I), z(LANES),
         w[:, o_pc:o_gl], w[:, o_gl:]], axis=-1)
    assert packed.shape[-1] == D_PACK
    return packed


def _scan_state_in(s):
    b = s.shape[0]
    s = s.reshape(b, H_C // 2, 2, D_C, D_C).transpose(0, 1, 3, 2, 4)
    return s.reshape(b * (H_C // 2), D_C, 2 * D_C)


def _scan_state_out(s, b):
    s = s.reshape(b, H_C // 2, D_C, 2, D_C).transpose(0, 1, 3, 2, 4)
    return s.reshape(b, H_C, D_C, D_C)


def _mods(ada_l, lo, hi, per_row_t):
    parts = jnp.split(ada_l[lo:hi], 6, axis=-1)
    if per_row_t is None:
        return [p[:, None, :] for p in parts]
    return [jnp.repeat(p, per_row_t, axis=0)[None] for p in parts]


def _layer(l, x, h, mods, mods_next, prm, attend_fn, ret_s0, wkv_s0, shift_rows, nb, t_len):
    n = nb * t_len
    tm_proj = min(n, TM_PROJ)
    tm = min(n, TM_ROW)
    sh1, sc1, gt1, sh2, sc2, gt2 = mods
    p2 = _in_proj(h, prm["w_in"][l], tm_proj)

    rb = CHUNK_A if t_len % CHUNK_A == 0 else t_len
    nchunk = t_len // rb
    oa, ret_s = _retention(p2.reshape(nb * nchunk, rb, D_PACK), ret_s0, prm["lgam"], prm["ret_gn_g"][l],
                           nb, nchunk, rb)
    oa = oa.reshape(n, H_A * DV_A)

    ob = attend_fn(p2)

    rp = {k: prm["rw_" + k][l] for k in ("mu", "w0", "a0", "kk", "ka", "rk", "wwa", "g2")}
    r, w, k2, v, na, b, g, bonus = _rwkv_prep(p2, shift_rows, rp, tm, t_len)
    tb = min(t_len, SCAN_TB)
    if t_len >= tm:
        tmaj = lambda a: a
    else:
        tmaj = lambda a: a.reshape(nb, t_len, W_C).transpose(1, 0, 2).reshape(t_len, nb * W_C)
    y, wkv_s = _rwkv_scan(tmaj(r), tmaj(w), tmaj(k2), tmaj(v), tmaj(na), tmaj(b), wkv_s0, nb, t_len, tb)
    if t_len < tm:
        y = y.reshape(t_len, nb, W_C).transpose(1, 0, 2).reshape(n, W_C)
    oc = _rwkv_post(y, g, bonus, prm["rw_lnx_g"][l], prm["rw_lnx_b"][l], tm, t_len)

    tm_m = tm
    tpg_m = max(t_len // tm_m, 1)
    x1, h2 = _merge(oa, ob, oc, p2, x, prm["w_branch"], prm["w_out"], l, gt1, sc2, sh2,
                    prm["ln1_g"][l], prm["ln1_b"][l], tm_m, tpg_m)
    sh_n, sc_n = mods_next
    i = l // 2
    if l % 2 == 0:
        x2, hn = _ffn_dense(h2, x1, prm["ffn_w1"], prm["ffn_w3"], prm["ffn_w2"], i, gt2, sc_n, sh_n,
                            prm["ln2_g"][l], prm["ln2_b"][l], tm_m, tpg_m)
    else:
        x2, hn = _ffn_moe(h2, x1, prm["moe_router"][i], prm["moe_router_b"][i], prm["moe_w1"],
                          prm["moe_w3"], prm["moe_w2"], i, gt2, sc_n, sh_n,
                          prm["ln2_g"][l], prm["ln2_b"][l], tm_m, tpg_m)
    shift_s = p2.reshape(nb, t_len, D_PACK)[:, -1, U_PC * LANES:U_PC * LANES + D_C_IN]
    if t_len >= TM_PROJ:
        kh, vh, ik = _kv_transposed(h, prm["wk_t"][l], prm["wv_t"][l], prm["wi_t"][l], nb, t_len, TM_PROJ)
    else:
        kh = p2[:, U_KB * LANES:U_KB * LANES + W_B].reshape(nb, t_len, H_B, D_B)
        vh = p2[:, U_VB * LANES:U_VB * LANES + W_B].reshape(nb, t_len, H_B, D_B)
        ik = p2[:, U_MISC * LANES:U_MISC * LANES + D_I].reshape(nb, t_len, D_I)
    return x2, hn, (kh, vh, ik, ret_s, _scan_state_out(wkv_s, nb), shift_s)


def kernel(x_prompt, x_sample, c_prompt, c_sample, cache_k, cache_v, cache_kidx, page_table, state_ret, state_wkv, state_shift, ada_w, ada_b, w_in, ret_gn_g, rw_mu, rw_w0, rw_w2, rw_a0, rw_a2, rw_g2, rw_kk, rw_ka, rw_rk, rw_lnx_g, rw_lnx_b, w_branch, w_out, ln1_g, ln1_b, ln2_g, ln2_b, ffn_w1, ffn_w3, ffn_w2, moe_router, moe_router_b, moe_w1, moe_w3, moe_w2):
    bp, tp, _ = x_prompt.shape
    bs, ts, _ = x_sample.shape
    n_pages = page_table.shape[1]
    n_pool = cache_k.shape[1]

    zw = jnp.zeros((DEPTH, LORA_W, W_C), F32)
    wwa = jnp.concatenate([jnp.concatenate([rw_w2, zw], axis=-1),
                           jnp.concatenate([zw, rw_a2], axis=-1)], axis=1).astype(BF16)
    n_moe = moe_router.shape[0]
    router_pad = jnp.concatenate(
        [moe_router, jnp.zeros((n_moe, D_MODEL, LANES - N_EXPERTS), F32)], axis=-1).astype(BF16)
    router_b_pad = jnp.concatenate(
        [moe_router_b, jnp.full((n_moe, LANES - N_EXPERTS), -1e30, F32)], axis=-1)[:, None, :]
    log_gamma = jnp.log1p(-jnp.exp2(-5.0 - jnp.arange(H_A, dtype=F32)))
    row = lambda a: a.reshape(DEPTH, 1, -1)
    o_k, o_v, o_ik = 5 * W_B, 6 * W_B, 8 * W_B + H_I
    w_in_bf = [w_in[l].astype(BF16) for l in range(DEPTH)]
    w_t = lambda o, w: [wl[:, o:o + w] for wl in w_in_bf]
    zpad = jnp.zeros((D_MODEL, LANES - D_I), BF16)
    prm = dict(
        wk_t=w_t(o_k, W_B), wv_t=w_t(o_v, W_B),
        wi_t=[jnp.concatenate([wl[:, o_ik:o_ik + D_I], zpad], axis=1) for wl in w_in_bf],
        w_in=[_pack_w_in(wl) for wl in w_in_bf], lgam=jnp.repeat(log_gamma, DK_A)[None, :], ret_gn_g=row(ret_gn_g),
        rw_mu=row(rw_mu), rw_w0=row(rw_w0), rw_a0=row(rw_a0), rw_kk=row(rw_kk), rw_ka=row(rw_ka),
        rw_rk=row(rw_rk), rw_wwa=wwa, rw_g2=rw_g2.astype(BF16), rw_lnx_g=row(rw_lnx_g), rw_lnx_b=row(rw_lnx_b),
        w_branch=w_branch.astype(BF16), w_out=w_out.astype(BF16),
        ln1_g=row(ln1_g), ln1_b=row(ln1_b), ln2_g=row(ln2_g), ln2_b=row(ln2_b),
        ffn_w1=ffn_w1.astype(BF16), ffn_w3=ffn_w3.astype(BF16), ffn_w2=ffn_w2.astype(BF16),
        moe_router=router_pad, moe_router_b=router_b_pad,
        moe_w1=moe_w1.astype(BF16), moe_w3=moe_w3.astype(BF16), moe_w2=moe_w2.astype(BF16))

    ada = _ada_all(jnp.concatenate([c_prompt, c_sample], axis=0), ada_w, ada_b)
    mods_p = [_mods(ada[l], 0, bp, None) for l in range(DEPTH)]
    mods_s = [_mods(ada[l], bp, bp + bs, ts) for l in range(DEPTH)]

    np_, ns_ = bp * tp, bs * ts
    tm_p = min(tp, TM_ROW)
    xp = x_prompt.reshape(np_, D_MODEL)
    xs = x_sample.reshape(ns_, D_MODEL)
    hp = _modulate(xp, mods_p[0][1], mods_p[0][0], tm_p, tp // tm_p)
    hs = _modulate(xs, mods_s[0][1], mods_s[0][0], ns_, 1)

    cache_k4 = cache_k.transpose(0, 1, 3, 4, 2).reshape(DEPTH, n_pool, W_B, PAGE_SIZE)
    cache_v4 = cache_v.transpose(0, 1, 3, 4, 2).reshape(DEPTH, n_pool, W_B, PAGE_SIZE)
    cache_kidx_t = cache_kidx.transpose(0, 1, 3, 2)
    zeros_ret = jnp.zeros((bp, H_A, DK_A, DV_A), F32)
    zeros_wkv = jnp.zeros((bp * (H_C // 2), D_C, 2 * D_C), F32)
    zeros_shift = jnp.zeros((bp, 1, D_C_IN), F32)

    st_p, st_s = [], []
    for l in range(DEPTH):
        nxt = min(l + 1, DEPTH - 1)
        attend_p = lambda p2: _dsa_prompt(p2, bp, tp)
        xp, hp, sp = _layer(l, xp, hp, mods_p[l], (mods_p[nxt][0], mods_p[nxt][1]), prm, attend_p,
                            zeros_ret, zeros_wkv, zeros_shift, bp, tp)

        def attend_s(p2, l=l):
            pad = lambda a: jnp.pad(a.reshape(bs, ts, -1).transpose(0, 2, 1),
                                    ((0, 0), (0, 0), (0, PAGE_SIZE - ts)))
            q3 = p2[:, U_QB * LANES:U_QB * LANES + W_B].reshape(bs, ts, W_B)
            iq32 = p2[:, U_IQ * LANES:U_IQ * LANES + W_B].reshape(bs, ts * H_I, D_I)
            iw = p2[:, U_MISC * LANES + IW_OFF:U_MISC * LANES + IW_OFF + H_I].reshape(bs, ts * H_I, 1)
            iwb = jnp.broadcast_to(iw, (bs, ts * H_I, LANES))
            knew = pad(p2[:, U_KB * LANES:U_KB * LANES + W_B])
            vnew = pad(p2[:, U_VB * LANES:U_VB * LANES + W_B])
            iknew = pad(p2[:, U_MISC * LANES:U_MISC * LANES + D_I])
            scores, scores_new = _dsa_sample_scores(l, page_table, cache_kidx_t, iq32, iwb, iknew, ts)
            bias, bias_new = _dsa_sample_select(scores, scores_new, ts)
            o = _dsa_sample_attend(l, page_table, bias, bias_new, q3, knew, vnew, cache_k4, cache_v4, ts)
            return o[:, :ts].reshape(ns_, W_B)

        shift_rows = jnp.repeat(state_shift[l], ts, axis=0)[None]
        xs, hs, ss = _layer(l, xs, hs, mods_s[l], (mods_s[nxt][0], mods_s[nxt][1]), prm, attend_s,
                            state_ret[l], _scan_state_in(state_wkv[l]), shift_rows, bs, ts)
        st_p.append(sp)
        st_s.append(ss)

    outs = [xp.reshape(bp, tp, D_MODEL), xs.reshape(bs, ts, D_MODEL)]
    for j in range(6):
        sp = jnp.stack([s[j] for s in st_p])
        if tp >= TM_PROJ and j < 2:
            sp = sp.reshape(DEPTH, bp, H_B, D_B, tp).transpose(0, 1, 4, 2, 3)
        elif tp >= TM_PROJ and j == 2:
            sp = sp.transpose(0, 1, 3, 2)
        outs.append(sp)
        outs.append(jnp.stack([s[j] for s in st_s]))
    return tuple(outs)
```

```python
import functools
import math

import jax
import jax.numpy as jnp
import numpy as np
from jax import lax
from jax.experimental import pallas as pl
from jax.experimental.pallas import tpu as pltpu

F32 = jnp.float32
BF16 = jnp.bfloat16
I32 = jnp.int32

D_MODEL = 1024
DEPTH = 4
PAGE_SIZE = 128
H_A, DK_A, DV_A = 4, 128, 128
CHUNK_A = 128
H_B, D_B = 8, 64
W_B = H_B * D_B
H_I, D_I = 8, 64
TOPK_MAX = 256
H_C, D_C = 8, 64
W_C = H_C * D_C
LORA_W, LORA_A, LORA_G = 64, 64, 128
RWKV_GN_EPS = 64e-5
N_BRANCH = 3
BRANCH_W = 512
D_FF = 2816
N_EXPERTS = 8
D_FF_E = 1408
ALPHA = (2.0 * DEPTH) ** 0.25
LN_EPS = 1e-5
D_C_IN = 3 * W_C + LORA_W + LORA_A + LORA_G

LANES = 128
SUBLANES = 8
VMEM_LIMIT = 56 * 1024 * 1024
TM_PROJ = 1024
TM_ROW = 512

U_QA, U_KA, U_VA, U_GA = 0, 4, 8, 12
U_QB, U_KB, U_VB, U_IQ = 16, 20, 24, 28
U_MISC = 32
U_PC = 34
U_GL = 48
N_UNITS = 72
PC_BLOCK_W = 2 * LANES
PC_BLOCKS = 7
D_PACK = N_UNITS * LANES
IW_OFF = 64

NEG_INF = float("-inf")


def _cp(sem, vmem=VMEM_LIMIT):
    return pltpu.CompilerParams(dimension_semantics=sem, vmem_limit_bytes=vmem)


def _dot(a, b):
    return jnp.dot(a, b, preferred_element_type=F32)


def _dot_nt(a, b):
    return lax.dot_general(a, b, (((1,), (1,)), ((), ())), preferred_element_type=F32)


def _dot_tn(a, b):
    return lax.dot_general(a, b, (((0,), (0,)), ((), ())), preferred_element_type=F32)


def _sigmoid(x):
    return 1.0 / (1.0 + jnp.exp(-x))


def _silu(x):
    return x * _sigmoid(x)


def _iota(shape, dim):
    return lax.broadcasted_iota(I32, shape, dim)


def _rows_to_tile(rows, width, fill=0.0):
    ri = _iota((SUBLANES, width), 0)
    out = jnp.full((SUBLANES, width), fill, F32)
    for t, r in enumerate(rows):
        out = jnp.where(ri == t, jnp.broadcast_to(r, (SUBLANES, width)), out)
    return out


def _ada_body(c_ref, w_ref, b_ref, o_ref):
    c = c_ref[...]
    o_ref[0] = _dot(_silu(c).astype(BF16), w_ref[0].astype(BF16)) + b_ref[0]


def _ada_all(c_all, ada_w, ada_b):
    m = c_all.shape[0]
    n = ada_w.shape[-1]
    tn = 1024
    return pl.pallas_call(
        _ada_body,
        grid=(DEPTH, n // tn),
        in_specs=[pl.BlockSpec((m, D_MODEL), lambda l, j: (0, 0)),
                  pl.BlockSpec((1, D_MODEL, tn), lambda l, j: (l, 0, j)),
                  pl.BlockSpec((1, 1, tn), lambda l, j: (l, 0, j))],
        out_specs=pl.BlockSpec((1, m, tn), lambda l, j: (l, 0, j)),
        out_shape=jax.ShapeDtypeStruct((DEPTH, m, n), F32),
        compiler_params=_cp(("arbitrary", "arbitrary")),
        name="ada",
    )(c_all, ada_w, ada_b.reshape(DEPTH, 1, n))


def _mod_spec(mod, tm, tiles_per_group):
    g, r, d = mod.shape
    if r == 1:
        return pl.BlockSpec((1, 1, d), lambda i, *_: (i // tiles_per_group, 0, 0))
    assert r == tm and g == 1
    return pl.BlockSpec((1, r, d), lambda i, *_: (0, 0, 0))


def _modulate_body(x_ref, sc_ref, sh_ref, h_ref):
    h_ref[...] = (x_ref[...] * (1.0 + sc_ref[0]) + sh_ref[0]).astype(BF16)


def _modulate(x, sc, sh, tm, tiles_per_group):
    n = x.shape[0]
    return pl.pallas_call(
        _modulate_body,
        grid=(n // tm,),
        in_specs=[pl.BlockSpec((tm, D_MODEL), lambda i: (i, 0)),
                  _mod_spec(sc, tm, tiles_per_group), _mod_spec(sh, tm, tiles_per_group)],
        out_specs=pl.BlockSpec((tm, D_MODEL), lambda i: (i, 0)),
        out_shape=jax.ShapeDtypeStruct((n, D_MODEL), BF16),
        compiler_params=_cp(("arbitrary",)),
        name="modulate",
    )(x, sc, sh)


def _mm_body(x_ref, w_ref, o_ref):
    o_ref[...] = _dot(x_ref[...], w_ref[...])


def _in_proj(h, w_pack, tm):
    n = h.shape[0]
    tn = 1024
    return pl.pallas_call(
        _mm_body,
        grid=(n // tm, D_PACK // tn),
        in_specs=[pl.BlockSpec((tm, D_MODEL), lambda i, j: (i, 0)),
                  pl.BlockSpec((D_MODEL, tn), lambda i, j: (0, j))],
        out_specs=pl.BlockSpec((tm, tn), lambda i, j: (i, j)),
        out_shape=jax.ShapeDtypeStruct((n, D_PACK), F32),
        compiler_params=_cp(("arbitrary", "arbitrary")),
        name="in_proj",
    )(h, w_pack)


def _kv_t_body(h_ref, wk_ref, wv_ref, wi_ref, k_ref, v_ref, i_ref):
    h = h_ref[...]
    k_ref[0] = _dot(h, wk_ref[...]).T
    v_ref[0] = _dot(h, wv_ref[...]).T
    i_ref[0] = _dot(h, wi_ref[...]).T[0:D_I]


def _kv_transposed(h, wk_t, wv_t, wi_t, nb, t_len, tt):
    nt = t_len // tt
    wspec = lambda w: pl.BlockSpec(w.shape, lambda b, t: (0, 0))
    ospec = lambda f: pl.BlockSpec((1, f, tt), lambda b, t: (b, 0, t))
    return pl.pallas_call(
        _kv_t_body,
        grid=(nb, nt),
        in_specs=[pl.BlockSpec((tt, D_MODEL), lambda b, t: (b * nt + t, 0)),
                  wspec(wk_t), wspec(wv_t), wspec(wi_t)],
        out_specs=[ospec(W_B), ospec(W_B), ospec(D_I)],
        out_shape=[jax.ShapeDtypeStruct((nb, W_B, t_len), F32), jax.ShapeDtypeStruct((nb, W_B, t_len), F32),
                   jax.ShapeDtypeStruct((nb, D_I, t_len), F32)],
        compiler_params=_cp(("arbitrary", "arbitrary")),
        name="kv_transposed",
    )(h, wk_t, wv_t, wi_t)


def _retention_body(q_ref, k_ref, v_ref, g_ref, s0_ref, lg_ref, gn_ref, o_ref, sout_ref,
                    s_scr, pad_scr, *, rb, nchunk):
    n = pl.program_id(1)
    c = CHUNK_A

    @pl.when(n == 0)
    def _():
        s_scr[...] = s0_ref[0]

    def padded(ref, slot):
        if rb == c:
            return ref[0]
        pad_scr[slot] = jnp.zeros((c, H_A * DK_A), F32)
        pad_scr[slot, 0:rb, :] = ref[0]
        return pad_scr[slot]

    q_all = padded(q_ref, 0)
    k_all = padded(k_ref, 1)
    v_all = padded(v_ref, 2)
    g_all = padded(g_ref, 3)

    row = _iota((c, c), 0).astype(F32)
    col = _iota((c, c), 1).astype(F32)
    rel = row - col
    outs = []
    for h in range(H_A):
        sl = slice(h * DK_A, (h + 1) * DK_A)
        lg = lg_ref[0:1, sl]
        q = (q_all[:, sl] * (DK_A ** -0.5)).astype(BF16)
        k = k_all[:, sl]
        v = v_all[:, sl].astype(BF16)
        s0 = s_scr[h]
        decay = jnp.where(rel >= 0.0, jnp.exp(lg * jnp.maximum(rel, 0.0)), 0.0)
        scores = _dot_nt(q, k.astype(BF16)) * decay
        inner = _dot(scores.astype(BF16), v)
        xi = jnp.exp(lg * (row + 1.0))
        cross = _dot(q, s0.astype(BF16)) * xi
        zeta = jnp.where(row < float(rb), jnp.exp(lg * (float(rb) - 1.0 - row)), 0.0)
        s_new = jnp.exp(lg * float(rb)) * s0 + _dot_tn((k * zeta).astype(BF16), v)
        s_scr[h] = s_new
        o = inner + cross
        mu = jnp.mean(o, axis=-1, keepdims=True)
        var = jnp.mean(jnp.square(o - mu), axis=-1, keepdims=True)
        o = (o - mu) * lax.rsqrt(var + LN_EPS) * gn_ref[0:1, sl]
        outs.append(_silu(g_all[:, sl]) * o)
    o_full = jnp.concatenate(outs, axis=-1)
    o_ref[0] = o_full[0:rb].astype(BF16)

    @pl.when(n == nchunk - 1)
    def _():
        sout_ref[0] = s_scr[...]


def _retention(p3, s0, lgam, gn_g, nb, nchunk, rb):
    w = H_A * DK_A
    col = lambda u: (lambda b, n: (b * nchunk + n, 0, u // 4))
    body = functools.partial(_retention_body, rb=rb, nchunk=nchunk)
    return pl.pallas_call(
        body,
        grid=(nb, nchunk),
        in_specs=[pl.BlockSpec((1, rb, w), col(U_QA)), pl.BlockSpec((1, rb, w), col(U_KA)),
                  pl.BlockSpec((1, rb, w), col(U_VA)), pl.BlockSpec((1, rb, w), col(U_GA)),
                  pl.BlockSpec((1, H_A, DK_A, DV_A), lambda b, n: (b, 0, 0, 0)),
                  pl.BlockSpec((1, w), lambda b, n: (0, 0)),
                  pl.BlockSpec((1, w), lambda b, n: (0, 0))],
        out_specs=[pl.BlockSpec((1, rb, w), lambda b, n: (b * nchunk + n, 0, 0)),
                   pl.BlockSpec((1, H_A, DK_A, DV_A), lambda b, n: (b, 0, 0, 0))],
        out_shape=[jax.ShapeDtypeStruct((nb * nchunk, rb, w), BF16),
                   jax.ShapeDtypeStruct((nb, H_A, DK_A, DV_A), F32)],
        scratch_shapes=[pltpu.VMEM((H_A, DK_A, DV_A), F32), pltpu.VMEM((4, CHUNK_A, w), F32)],
        compiler_params=_cp(("arbitrary", "arbitrary")),
        name="retention",
    )(p3, p3, p3, p3, s0, lgam, gn_g)


def _score_key(s):
    bits = pltpu.bitcast(s, I32)
    return jnp.where(bits < 0, bits ^ jnp.int32(0x7FFFFFFF), bits)


def _tri_incl():
    return jnp.where(_iota((LANES, LANES), 0) <= _iota((LANES, LANES), 1), 1.0, 0.0).astype(BF16)


KC = 512
DSA_QB = 256


def _dsa_prompt_body(q_ref, iq_ref, mq_ref, k_ref, v_ref, mk_ref, o_ref,
                     kx, vt, ikbf, keys, bias, q2, *, t_len, top):
    i = pl.program_id(1)
    nkc = t_len // KC
    qb = min(DSA_QB, t_len)

    @pl.when(i == 0)
    def _():
        prow = _iota((KC, LANES), 0)
        plane = _iota((KC, LANES), 1)
        for j in range(nkc):
            kpos = j * KC + prow
            posf = jnp.where(plane == 0, kpos // 64, jnp.where(plane == 1, kpos % 64,
                                                                jnp.where(plane == 2, 1, 0)))
            posf = posf.astype(F32).astype(BF16)
            for hp in range(H_B // 2):
                kx[j, :, hp * 2 * LANES:hp * 2 * LANES + LANES] = (
                    k_ref[j * KC:(j + 1) * KC, hp * LANES:(hp + 1) * LANES].astype(BF16))
                kx[j, :, hp * 2 * LANES + LANES:(hp + 1) * 2 * LANES] = posf
                vt[j, hp * LANES:(hp + 1) * LANES, :] = (
                    v_ref[j * KC:(j + 1) * KC, hp * LANES:(hp + 1) * LANES].T.astype(BF16))
            ikbf[j] = mk_ref[j * KC:(j + 1) * KC, :].astype(BF16)

    nsc = (i * qb) // KC + 1
    qpos = i * qb + _iota((KC, qb), 1)
    krow = _iota((KC, qb), 0)

    iq = (iq_ref[...] * (D_I ** -0.5)).astype(BF16)
    iw_t = mq_ref[...].T * (H_I ** -0.5)

    def score_chunk(j, carry):
        ik = ikbf[j][:, 0:D_I]
        acc = jnp.zeros((KC, qb), F32)
        for h in range(H_I):
            s = _dot_nt(ik, iq[:, h * D_I:(h + 1) * D_I])
            acc = acc + jnp.maximum(s, 0.0) * iw_t[IW_OFF + h:IW_OFF + h + 1]
        acc = jnp.where(j * KC + krow <= qpos, acc + 0.0, NEG_INF)
        keys[j] = _score_key(acc)
        return carry

    lax.fori_loop(0, nsc, score_chunk, 0)

    def count(pred):
        fold = 4 * SUBLANES

        def body(j, acc):
            m = jnp.where(pred(keys[j]), 1.0, 0.0).reshape(KC // fold, fold, qb)
            return acc + jnp.sum(m, axis=0)
        acc = lax.fori_loop(0, nsc, body, jnp.zeros((fold, qb), F32))
        return jnp.sum(acc, axis=0, keepdims=True)

    def count_ge(cand):
        return count(lambda kj: kj >= cand)

    lowest = jnp.full((1, qb), -2 ** 31, I32)

    def bisect():
        prefix = lowest
        for bit in range(31, -1, -1):
            step = jnp.int32(-2 ** 31) if bit == 31 else jnp.int32(1 << bit)
            cand = prefix + step
            cnt = count_ge(cand)
            prefix = jnp.where(cnt >= float(top), cand, prefix)
        return prefix

    neg_inf_key = _score_key(jnp.full((1, qb), NEG_INF, F32))
    tau = lax.cond((i + 1) * qb <= top, lambda: neg_inf_key, bisect)
    need = float(top) - count(lambda kj: kj > tau)

    tri_l = jnp.where(_iota((KC, KC), 1) <= _iota((KC, KC), 0), 1.0, 0.0).astype(BF16)

    def bias_chunk(j, carry):
        kj = keys[j]
        eq = jnp.where(kj == tau, 1.0, 0.0)
        pre = _dot(tri_l, eq.astype(BF16)) + carry
        keep = ((kj > tau) | ((eq > 0.0) & (pre <= need))) & (j * KC + krow <= qpos)
        bias[j] = jnp.where(keep, 0.0, NEG_INF)
        return carry + jnp.sum(eq, axis=0, keepdims=True)

    lax.fori_loop(0, nsc, bias_chunk, jnp.zeros((1, qb), F32))

    npair = H_B // 2
    q_all = q_ref[...]
    lane_q = _iota((qb, LANES), 1)
    base = (i * qb + 0 * lane_q).astype(F32)
    for hp in range(npair):
        qp = q_all[:, hp * LANES:(hp + 1) * LANES] * (D_B ** -0.5)
        for h2 in range(2):
            slope = 2.0 ** (-8.0 * (2 * hp + h2 + 1) / H_B)
            in_half = (lane_q >= h2 * D_B) & (lane_q < (h2 + 1) * D_B)
            feat = jnp.where(lane_q == 0, 64.0 * slope,
                             jnp.where(lane_q == 1, slope, jnp.where(lane_q == 2, -slope * base, 0.0)))
            q2[hp, h2 * qb:(h2 + 1) * qb, 0:LANES] = jnp.where(in_half, qp, 0.0).astype(BF16)
            q2[hp, h2 * qb:(h2 + 1) * qb, LANES:2 * LANES] = feat.astype(BF16)

    def attn_chunk(j, carry):
        ms, ls, accs = carry
        b2 = jnp.concatenate([bias[j], bias[j]], axis=1)
        out_m, out_l, out_acc = [], [], []
        for hp in range(npair):
            s = _dot_nt(kx[j, :, hp * 2 * LANES:(hp + 1) * 2 * LANES], q2[hp]) + b2
            m_new = jnp.maximum(ms[hp], jnp.max(s, axis=0, keepdims=True))
            m_safe = jnp.where(m_new == NEG_INF, 0.0, m_new)
            alpha = jnp.exp(ms[hp] - m_safe)
            p = jnp.exp(s - m_safe)
            out_m.append(m_new)
            out_l.append(ls[hp] * alpha + jnp.sum(p, axis=0, keepdims=True))
            out_acc.append(accs[hp] * alpha + _dot(vt[j, hp * LANES:(hp + 1) * LANES, :], p.astype(BF16)))
        return out_m, out_l, out_acc

    init = ([jnp.full((1, 2 * qb), NEG_INF, F32)] * npair, [jnp.zeros((1, 2 * qb), F32)] * npair,
            [jnp.zeros((LANES, 2 * qb), F32)] * npair)
    _, ls, accs = lax.fori_loop(0, nsc, attn_chunk, init)
    row_o = _iota((LANES, qb), 0)
    outs = []
    for hp in range(npair):
        o = accs[hp] / ls[hp]
        outs.append(jnp.where(row_o < D_B, o[:, 0:qb], o[:, qb:2 * qb]).T)
    o_ref[...] = jnp.concatenate(outs, axis=1).astype(BF16)


def _dsa_prompt(p2, nb, t_len):
    top = min(TOPK_MAX, t_len // 4)
    qb = min(DSA_QB, t_len)
    nq = t_len // qb
    nkc = t_len // KC
    body = functools.partial(_dsa_prompt_body, t_len=t_len, top=top)
    qspec = lambda u, wdt: pl.BlockSpec((qb, wdt), lambda b, i: (b * nq + i, u * LANES // wdt))
    aspec = lambda u, wdt: pl.BlockSpec((t_len, wdt), lambda b, i: (b, u * LANES // wdt))
    return pl.pallas_call(
        body,
        grid=(nb, nq),
        in_specs=[qspec(U_QB, W_B), qspec(U_IQ, W_B), qspec(U_MISC, LANES),
                  aspec(U_KB, W_B), aspec(U_VB, W_B), aspec(U_MISC, LANES)],
        out_specs=pl.BlockSpec((qb, W_B), lambda b, i: (b * nq + i, 0)),
        out_shape=jax.ShapeDtypeStruct((nb * t_len, W_B), BF16),
        scratch_shapes=[pltpu.VMEM((nkc, KC, 2 * W_B), BF16), pltpu.VMEM((nkc, W_B, KC), BF16),
                        pltpu.VMEM((nkc, KC, LANES), BF16),
                        pltpu.VMEM((nkc, KC, qb), I32), pltpu.VMEM((nkc, KC, qb), F32),
                        pltpu.VMEM((H_B // 2, 2 * qb, 2 * LANES), BF16)],
        compiler_params=_cp(("arbitrary", "arbitrary")),
        name="dsa_prompt",
    )(p2, p2, p2, p2, p2, p2)


PG_STEP = 32


def _dsa_s_score_body(pt_ref, iq_ref, iwb_ref, iknew_ref, *rest, t_len):
    pages = rest[:PG_STEP]
    o_ref, onew_ref = rest[PG_STEP:PG_STEP + 2]
    iq = (iq_ref[0] * (D_I ** -0.5)).astype(BF16)
    iwb = iwb_ref[0] * (H_I ** -0.5)
    row = _iota((SUBLANES, LANES), 0)
    lane = _iota((SUBLANES, LANES), 1)

    def scores(ikt):
        s = jnp.maximum(_dot(iq, ikt.astype(BF16)), 0.0) * iwb
        per_t = [jnp.sum(s[t * H_I:(t + 1) * H_I], axis=0, keepdims=True) for t in range(t_len)]
        return _rows_to_tile(per_t, LANES) + 0.0

    for j in range(PG_STEP):
        o_ref[0, j] = jnp.where(row < t_len, scores(pages[j][0, 0]), NEG_INF)

    @pl.when(pl.program_id(1) == 0)
    def _():
        onew_ref[0, 0] = jnp.where((row < t_len) & (lane <= row), scores(iknew_ref[0]), NEG_INF)


def _dsa_sample_scores(layer, page_table, cache_kidx_t, iq32, iwb, iknew, t_len):
    nb, n_pages = page_table.shape
    nsteps = n_pages // PG_STEP
    rows = t_len * H_I

    def page_spec(j):
        return pl.BlockSpec((1, 1, D_I, PAGE_SIZE),
                            lambda b, s, pt: (layer, pt[b, s * PG_STEP + j], 0, 0))

    grid_spec = pltpu.PrefetchScalarGridSpec(
        num_scalar_prefetch=1,
        grid=(nb, nsteps),
        in_specs=[pl.BlockSpec((1, rows, D_I), lambda b, s, pt: (b, 0, 0)),
                  pl.BlockSpec((1, rows, LANES), lambda b, s, pt: (b, 0, 0)),
                  pl.BlockSpec((1, D_I, PAGE_SIZE), lambda b, s, pt: (b, 0, 0))]
                 + [page_spec(j) for j in range(PG_STEP)],
        out_specs=[pl.BlockSpec((1, PG_STEP, SUBLANES, LANES), lambda b, s, pt: (b, s, 0, 0)),
                   pl.BlockSpec((1, 1, SUBLANES, LANES), lambda b, s, pt: (b, 0, 0, 0))],
    )
    return pl.pallas_call(
        functools.partial(_dsa_s_score_body, t_len=t_len),
        grid_spec=grid_spec,
        out_shape=[jax.ShapeDtypeStruct((nb, n_pages, SUBLANES, LANES), F32),
                   jax.ShapeDtypeStruct((nb, 1, SUBLANES, LANES), F32)],
        compiler_params=_cp(("arbitrary", "arbitrary")),
        name="dsa_sample_scores",
    )(page_table, iq32, iwb, iknew, *([cache_kidx_t] * PG_STEP))


SEL_B = 8


def _dsa_s_select_body(sc_ref, scn_ref, bias_ref, biasn_ref, *, t_len, n_pages, top):
    row8 = _iota((SUBLANES, LANES), 0)
    lane8 = _iota((SUBLANES, LANES), 1)
    keys_p = [_score_key(sc_ref[b]) for b in range(SEL_B)]
    keys_n = [_score_key(scn_ref[b, 0]) for b in range(SEL_B)]

    def count(pred_past, pred_new):
        c = jnp.sum(jnp.where(pred_past, 1.0, 0.0), axis=0) + jnp.where(pred_new, 1.0, 0.0)
        return jnp.sum(c, axis=1, keepdims=True)

    prefix = [jnp.full((SUBLANES, 1), -2 ** 31, I32)] * SEL_B
    for bit in range(31, -1, -1):
        step = jnp.int32(-2 ** 31) if bit == 31 else jnp.int32(1 << bit)
        nxt = []
        for b in range(SEL_B):
            cand = prefix[b] + step
            cnt = count(keys_p[b] >= cand[None], keys_n[b] >= cand)
            nxt.append(jnp.where(cnt >= float(top), cand, prefix[b]))
        prefix = nxt

    tri = _tri_incl()
    ones = jnp.ones((LANES, LANES), BF16)
    for b in range(SEL_B):
        tau = prefix[b]
        need = float(top) - count(keys_p[b] > tau[None], keys_n[b] > tau)
        eq = jnp.where(keys_p[b] == tau[None], 1.0, 0.0)
        eq2 = eq.reshape(n_pages * SUBLANES, LANES).astype(BF16)
        pre_in = _dot(eq2, tri).reshape(n_pages, SUBLANES, LANES)
        totals = _dot(eq2, ones).reshape(n_pages, SUBLANES, LANES)
        carry = jnp.zeros((SUBLANES, LANES), F32)
        for p in range(n_pages):
            keep = (keys_p[b][p] > tau) | ((eq[p] > 0.0) & (pre_in[p] + carry <= need))
            bias_ref[b, p] = jnp.where(keep & (row8 < t_len), 0.0, NEG_INF)
            carry = carry + totals[p]
        e = jnp.where(keys_n[b] == tau, 1.0, 0.0)
        pre = _dot(e.astype(BF16), tri) + carry
        keep = (keys_n[b] > tau) | ((e > 0.0) & (pre <= need))
        biasn_ref[b, 0] = jnp.where(keep & (row8 < t_len) & (lane8 <= row8), 0.0, NEG_INF)


def _dsa_sample_select(scores, scores_new, t_len):
    nb, n_pages = scores.shape[:2]
    top = min(TOPK_MAX, (n_pages * PAGE_SIZE + t_len) // 4)
    past = pl.BlockSpec((SEL_B, n_pages, SUBLANES, LANES), lambda c: (c, 0, 0, 0))
    new = pl.BlockSpec((SEL_B, 1, SUBLANES, LANES), lambda c: (c, 0, 0, 0))
    return pl.pallas_call(
        functools.partial(_dsa_s_select_body, t_len=t_len, n_pages=n_pages, top=top),
        grid=(nb // SEL_B,),
        in_specs=[past, new],
        out_specs=[past, new],
        out_shape=[jax.ShapeDtypeStruct(scores.shape, F32), jax.ShapeDtypeStruct(scores_new.shape, F32)],
        compiler_params=_cp(("arbitrary",)),
        name="dsa_sample_select",
    )(scores, scores_new)


def _dsa_s_attn_body(pt_ref, bias_ref, biasn_ref, q_ref, knew_ref, vnew_ref, *rest, t_len, n_pages):
    kpages = rest[:PG_STEP]
    vpages = rest[PG_STEP:2 * PG_STEP]
    o_ref = rest[2 * PG_STEP]
    m_scr, l_scr, acc_scr = rest[2 * PG_STEP + 1:]
    s_idx = pl.program_id(1)
    nsteps = n_pages // PG_STEP
    past = n_pages * PAGE_SIZE
    rows = t_len * H_B

    @pl.when(s_idx == 0)
    def _():
        m_scr[...] = jnp.full((rows, LANES), NEG_INF, F32)
        l_scr[...] = jnp.zeros((rows, LANES), F32)
        acc_scr[...] = jnp.zeros((rows, W_B), F32)

    q = q_ref[0]
    col = _iota((SUBLANES, W_B), 1)
    hrow = _iota((SUBLANES, W_B), 0)
    blockmask = (col // D_B) == hrow
    qexp = jnp.concatenate(
        [jnp.where(blockmask, jnp.broadcast_to(q[t:t + 1], (SUBLANES, W_B)), 0.0) for t in range(t_len)],
        axis=0).astype(BF16)
    hvec = _iota((rows, LANES), 0) % H_B
    slope = jnp.exp2(-8.0 * (hvec.astype(F32) + 1.0) / H_B)
    tvec = _iota((rows, LANES), 0) // H_B
    qpos = past + tvec
    lane = _iota((rows, LANES), 1)

    def expand_bias(bp):
        return jnp.concatenate([jnp.broadcast_to(bp[t:t + 1], (SUBLANES, LANES)) for t in range(t_len)],
                               axis=0)

    def attend(k_list, v_list, bias_list, kpos0_list):
        logits = []
        for kp, bp, kpos0 in zip(k_list, bias_list, kpos0_list):
            s = _dot(qexp, kp.astype(BF16)) * (D_B ** -0.5)
            dist = (qpos - (kpos0 + lane)).astype(F32)
            logits.append(s - slope * dist + expand_bias(bp))
        m_old = m_scr[...]
        m_new = m_old
        for s in logits:
            m_new = jnp.maximum(m_new, jnp.max(s, axis=1, keepdims=True))
        m_safe = jnp.where(m_new == NEG_INF, 0.0, m_new)
        scale = jnp.exp(m_old - m_safe)
        l = l_scr[...] * scale
        acc = acc_scr[...] * scale[:, 0:1]
        for s, vp in zip(logits, v_list):
            p = jnp.exp(s - m_safe)
            l = l + jnp.sum(p, axis=1, keepdims=True)
            acc = acc + _dot_nt(p.astype(BF16), vp.astype(BF16))
        m_scr[...] = m_new
        l_scr[...] = l
        acc_scr[...] = acc

    @pl.when(s_idx < nsteps)
    def _():
        base = s_idx * PG_STEP
        attend([kpages[j][0, 0] for j in range(PG_STEP)],
               [vpages[j][0, 0] for j in range(PG_STEP)],
               [bias_ref[0, base + j] for j in range(PG_STEP)],
               [(base + j) * PAGE_SIZE for j in range(PG_STEP)])

    @pl.when(s_idx == nsteps)
    def _():
        attend([knew_ref[0]], [vnew_ref[0]], [biasn_ref[0, 0]], [past])
        o = acc_scr[...] / l_scr[...][:, 0:1]
        outs = [jnp.sum(jnp.where(blockmask, o[t * H_B:(t + 1) * H_B], 0.0), axis=0, keepdims=True)
                for t in range(t_len)]
        o_ref[0] = _rows_to_tile(outs, W_B).astype(BF16)


def _dsa_sample_attend(layer, page_table, bias, bias_new, q3, knew, vnew, cache_k4, cache_v4, t_len):
    nb, n_pages = page_table.shape
    nsteps = n_pages // PG_STEP
    rows = t_len * H_B

    def page_spec(j):
        def imap(b, s, pt):
            return (layer, pt[b, jnp.minimum(s, nsteps - 1) * PG_STEP + j], 0, 0)
        return pl.BlockSpec((1, 1, W_B, PAGE_SIZE), imap)

    per_b = lambda shape: pl.BlockSpec((1,) + shape, lambda b, s, pt: (b,) + (0,) * len(shape))
    grid_spec = pltpu.PrefetchScalarGridSpec(
        num_scalar_prefetch=1,
        grid=(nb, nsteps + 1),
        in_specs=[per_b((n_pages, SUBLANES, LANES)), per_b((1, SUBLANES, LANES)), per_b((t_len, W_B)),
                  per_b((W_B, PAGE_SIZE)), per_b((W_B, PAGE_SIZE))]
                 + [page_spec(j) for j in range(PG_STEP)] + [page_spec(j) for j in range(PG_STEP)],
        out_specs=per_b((SUBLANES, W_B)),
        scratch_shapes=[pltpu.VMEM((rows, LANES), F32), pltpu.VMEM((rows, LANES), F32),
                        pltpu.VMEM((rows, W_B), F32)],
    )
    body = functools.partial(_dsa_s_attn_body, t_len=t_len, n_pages=n_pages)
    return pl.pallas_call(
        body,
        grid_spec=grid_spec,
        out_shape=jax.ShapeDtypeStruct((nb, SUBLANES, W_B), BF16),
        compiler_params=_cp(("arbitrary", "arbitrary")),
        name="dsa_sample_attend",
    )(page_table, bias, bias_new, q3, knew, vnew, *([cache_k4] * PG_STEP), *([cache_v4] * PG_STEP))


def _head_block_ones(n):
    return jnp.where((_iota((n, n), 0) // D_C) == (_iota((n, n), 1) // D_C), 1.0, 0.0).astype(BF16)


def _rwkv_prep_body(*refs, tm, seq_len):
    pc_refs = refs[:PC_BLOCKS]
    (sh_ref, mu_ref, w0_ref, a0_ref, kk_ref, ka_ref, rk_ref, wwa_ref, g2_ref,
     r_ref, w_ref, k_ref, v_ref, na_ref, b_ref, g_ref, bonus_ref, carry) = refs[PC_BLOCKS:]
    pc = jnp.concatenate([r[...] for r in pc_refs], axis=1)
    rolled = pltpu.roll(pc, 1, 0)
    row = _iota(pc.shape, 0)
    if seq_len >= tm:
        tiles_per_seq = seq_len // tm
        first = (pl.program_id(0) % tiles_per_seq) == 0
        head_row = jnp.where(first, sh_ref[0], carry[...])
        prev = jnp.where(row == 0, head_row, rolled)
        carry[...] = pc[tm - 1:tm, :]
    else:
        prev = jnp.where((row % seq_len) == 0, sh_ref[0], rolled)
    pm = pc + (prev - pc) * mu_ref[...]
    r = pm[:, 0:W_C]
    k = pm[:, W_C:2 * W_C]
    v = pm[:, 2 * W_C:3 * W_C]
    lwa = pm[:, 3 * W_C:3 * W_C + LANES]
    lg = pm[:, 3 * W_C + LANES:3 * W_C + 2 * LANES]
    lane = _iota(lwa.shape, 1)
    lwa = jnp.where(lane < LORA_W, jnp.tanh(lwa), lwa)
    wa = _dot(lwa.astype(BF16), wwa_ref[...])
    x = -(w0_ref[...] + wa[:, 0:W_C])
    softplus = jnp.maximum(x, 0.0) + jnp.log1p(jnp.exp(-jnp.abs(x)))
    w_raw = -softplus - 0.5
    a = _sigmoid(a0_ref[...] + wa[:, W_C:2 * W_C])
    g = _dot(_sigmoid(lg).astype(BF16), g2_ref[...])
    ones = _head_block_ones(W_C)
    kk = k * kk_ref[...]
    ss = _dot((kk * kk).astype(BF16), ones)
    kk = kk / jnp.maximum(jnp.sqrt(ss), 1e-12)
    k2 = k * (1.0 + (a - 1.0) * ka_ref[...])
    r_ref[...] = r
    w_ref[...] = jnp.exp(-jnp.exp(w_raw))
    k_ref[...] = k2
    v_ref[...] = v
    na_ref[...] = -kk
    b_ref[...] = kk * a
    g_ref[...] = g
    bonus_ref[...] = _dot((r * k2 * rk_ref[...]).astype(BF16), ones) * v


def _rwkv_prep(p2, shift_rows, prm, tm, seq_len):
    n = p2.shape[0]
    out_rm = pl.BlockSpec((tm, W_C), lambda i: (i, 0))
    shape_rm = jax.ShapeDtypeStruct((n, W_C), F32)
    if seq_len >= tm:
        tiles_per_seq = seq_len // tm
        sh_spec = pl.BlockSpec((1, 1, D_C_IN), lambda i: (i // tiles_per_seq, 0, 0))
        out_scan = pl.BlockSpec((tm, W_C), lambda i: (i % tiles_per_seq, i // tiles_per_seq))
        shape_scan = jax.ShapeDtypeStruct((seq_len, (n // seq_len) * W_C), F32)
    else:
        sh_spec = pl.BlockSpec((1, tm, D_C_IN), lambda i: (i, 0, 0))
        out_scan, shape_scan = out_rm, shape_rm
    row = lambda w: pl.BlockSpec((1, w), lambda i: (0, 0))
    body = functools.partial(_rwkv_prep_body, tm=tm, seq_len=seq_len)
    pc_block0 = U_PC * LANES // PC_BLOCK_W

    def pc_spec(j):
        return pl.BlockSpec((tm, PC_BLOCK_W), lambda i: (i, pc_block0 + j))

    return pl.pallas_call(
        body,
        grid=(n // tm,),
        in_specs=[pc_spec(j) for j in range(PC_BLOCKS)]
                 + [sh_spec, row(D_C_IN), row(W_C), row(W_C), row(W_C), row(W_C), row(W_C),
                    pl.BlockSpec((LANES, 2 * W_C), lambda i: (0, 0)),
                    pl.BlockSpec((LORA_G, W_C), lambda i: (0, 0))],
        out_specs=[out_scan] * 6 + [out_rm] * 2,
        out_shape=[shape_scan] * 6 + [shape_rm] * 2,
        scratch_shapes=[pltpu.VMEM((1, D_C_IN), F32)],
        compiler_params=_cp(("arbitrary",)),
        name="rwkv_prep",
    )(*([p2] * PC_BLOCKS), shift_rows, prm["mu"], prm["w0"], prm["a0"], prm["kk"], prm["ka"], prm["rk"],
      prm["wwa"], prm["g2"])


SCAN_B = 8
SCAN_TB = 128


def _rwkv_scan_body(r_ref, w_ref, k_ref, v_ref, na_ref, b_ref, s0_ref, y_ref, sout_ref, s_scr, ybuf,
                    *, tb, ntb):
    tblk = pl.program_id(1)

    @pl.when(tblk == 0)
    def _():
        s_scr[...] = s0_ref[...]

    ones1 = _head_block_ones(LANES)
    ones2 = _head_block_ones(2 * LANES)
    vi = _iota((D_C, LANES), 0)
    li = _iota((D_C, LANES), 1)
    irep = jnp.where((li % D_C) == vi, 1.0, 0.0)
    ngrp = SCAN_B * (H_C // 2)

    def vec(x_row, g):
        return x_row[:, g * LANES:(g + 1) * LANES]

    def y_row(yb_all):
        return jnp.concatenate(
            [jnp.sum(yb_all[g * D_C:(g + 1) * D_C] * irep, axis=0, keepdims=True) for g in range(ngrp)], axis=1)

    def advance(w_t, k_t, v_t, na_t, b_t, r_p):
        s_old = [s_scr[g] for g in range(ngrp)]
        na_b, r_b = na_t.astype(BF16), r_p.astype(BF16)
        s_b = [s.astype(BF16) for s in s_old]
        lhs1 = jnp.concatenate(
            [jnp.concatenate([s_b[g] * vec(na_b, g), s_b[g] * vec(r_b, g)], axis=1) for g in range(ngrp)],
            axis=0)
        res1 = _dot(lhs1, ones2)
        lhs2 = jnp.concatenate(
            [jnp.concatenate([(irep * vec(v_t, g)).astype(BF16), (irep * vec(v_t, g + 1)).astype(BF16)], axis=1)
             for g in range(0, ngrp, 2)], axis=0)
        res2 = _dot(lhs2, ones2)
        for g in range(ngrp):
            sa = res1[g * D_C:(g + 1) * D_C, 0:LANES]
            pr, half = divmod(g, 2)
            vb = res2[pr * D_C:(pr + 1) * D_C, half * LANES:(half + 1) * LANES]
            s_scr[g] = s_old[g] * vec(w_t, g) + sa * vec(b_t, g) + vb * vec(k_t, g)
        return y_row(res1[:, LANES:2 * LANES])

    def final_y(r_last):
        lhs = jnp.concatenate([(s_scr[g] * vec(r_last, g)).astype(BF16) for g in range(ngrp)], axis=0)
        return y_row(_dot(lhs, ones1))

    refs = (w_ref, k_ref, v_ref, na_ref, b_ref)
    if tb % SUBLANES == 0:
        ybuf[...] = jnp.zeros(ybuf.shape, F32)

        def tile_step(t8, carry):
            base = pl.multiple_of(t8 * SUBLANES, SUBLANES)
            prev = pl.multiple_of(jnp.maximum(t8 - 1, 0) * SUBLANES, SUBLANES)
            tiles = [ref[pl.ds(base, SUBLANES), :] for ref in refs]
            r_tile = r_ref[pl.ds(base, SUBLANES), :]
            r_prev_tile = r_ref[pl.ds(prev, SUBLANES), :]
            for tt in range(SUBLANES):
                r_p = r_prev_tile[SUBLANES - 1:SUBLANES] if tt == 0 else r_tile[tt - 1:tt]
                y_p = advance(*[x[tt:tt + 1] for x in tiles], r_p)
                if tt == 0:
                    ybuf[SUBLANES - 1:SUBLANES, :] = y_p
                    y_ref[pl.ds(prev, SUBLANES), :] = ybuf[...]
                else:
                    ybuf[tt - 1:tt, :] = y_p
            return carry

        lax.fori_loop(0, tb // SUBLANES, tile_step, 0)
        ybuf[SUBLANES - 1:SUBLANES, :] = final_y(r_ref[tb - 1:tb, :])
        y_ref[tb - SUBLANES:tb, :] = ybuf[...]
    else:
        for t in range(tb):
            y_p = advance(*[ref[t:t + 1, :] for ref in refs], r_ref[max(t - 1, 0):max(t - 1, 0) + 1, :])
            if t > 0:
                y_ref[t - 1:t, :] = y_p
        y_ref[tb - 1:tb, :] = final_y(r_ref[tb - 1:tb, :])

    @pl.when(tblk == ntb - 1)
    def _():
        sout_ref[...] = s_scr[...]


def _rwkv_scan(r, w, k, v, na, b, s0, nb, t_len, tb):
    ntb = t_len // tb
    npair = H_C // 2
    blk = pl.BlockSpec((tb, SCAN_B * W_C), lambda c, t: (t, c))
    sblk = pl.BlockSpec((SCAN_B * npair, D_C, LANES), lambda c, t: (c, 0, 0))
    body = functools.partial(_rwkv_scan_body, tb=tb, ntb=ntb)
    return pl.pallas_call(
        body,
        grid=(nb // SCAN_B, ntb),
        in_specs=[blk] * 6 + [sblk],
        out_specs=[blk, sblk],
        out_shape=[jax.ShapeDtypeStruct((t_len, nb * W_C), F32),
                   jax.ShapeDtypeStruct((nb * npair, D_C, LANES), F32)],
        scratch_shapes=[pltpu.VMEM((SCAN_B * npair, D_C, LANES), F32),
                        pltpu.VMEM((SUBLANES, SCAN_B * W_C), F32)],
        compiler_params=_cp(("arbitrary", "arbitrary")),
        name="rwkv_scan",
    )(r, w, k, v, na, b, s0)


def _rwkv_out(y, g, bonus, lnx_g, lnx_b):
    ones = _head_block_ones(W_C)
    mu = _dot(y.astype(BF16), ones) * (1.0 / D_C)
    d = y - mu
    var = _dot((d * d).astype(BF16), ones) * (1.0 / D_C)
    yn = d * lax.rsqrt(var + RWKV_GN_EPS) * lnx_g + lnx_b
    return ((yn + bonus) * g).astype(BF16)


def _layer_norm(x, g, b):
    mu = jnp.mean(x, axis=-1, keepdims=True)
    var = jnp.mean(jnp.square(x - mu), axis=-1, keepdims=True)
    return (x - mu) * lax.rsqrt(var + LN_EPS) * g + b


def _merge_body(oa_ref, ob_ref, y_ref, g_ref, bonus_ref, lxg_ref, lxb_ref, gla_ref, glb_ref, glc_ref,
                x_ref, wb_ref, wo_ref, gt_ref, sc_ref, sh_ref, lg_ref, lb_ref, x1_ref, h_ref):
    oc = _rwkv_out(y_ref[...], g_ref[...], bonus_ref[...], lxg_ref[...], lxb_ref[...])
    mixed = None
    for gidx, (o, gl_ref) in enumerate(((oa_ref[...], gla_ref), (ob_ref[...], glb_ref), (oc, glc_ref))):
        br = _dot(o, wb_ref[0, gidx])
        term = _sigmoid(gl_ref[...]) * br
        mixed = term if mixed is None else mixed + term
    y = _dot(mixed.astype(BF16), wo_ref[0])
    x1 = _layer_norm(ALPHA * x_ref[...] + gt_ref[0] * y, lg_ref[...], lb_ref[...])
    x1_ref[...] = x1
    h_ref[...] = (x1 * (1.0 + sc_ref[0]) + sh_ref[0]).astype(BF16)


def _merge(oa, ob, y, g, bonus, lnx_g, lnx_b, seq_len, p2, x, wb, wo, li, gt, sc, sh, ln_g, ln_b, tm,
           tiles_per_group):
    n = x.shape[0]
    blk = lambda w: pl.BlockSpec((tm, w), lambda i: (i, 0))
    row = pl.BlockSpec((1, D_MODEL), lambda i: (0, 0))
    rowc = pl.BlockSpec((1, W_C), lambda i: (0, 0))
    ms = lambda m: _mod_spec(m, tm, tiles_per_group)
    gl_block0 = U_GL * LANES // D_MODEL
    if seq_len >= tm:
        tiles_per_seq = seq_len // tm
        yblk = pl.BlockSpec((tm, W_C), lambda i: (i % tiles_per_seq, i // tiles_per_seq))
    else:
        yblk = blk(W_C)
    return pl.pallas_call(
        _merge_body,
        grid=(n // tm,),
        in_specs=[blk(BRANCH_W), blk(BRANCH_W), yblk, blk(W_C), blk(W_C), rowc, rowc]
                 + [pl.BlockSpec((tm, D_MODEL), lambda i, g=g: (i, gl_block0 + g)) for g in range(N_BRANCH)]
                 + [blk(D_MODEL),
                  pl.BlockSpec((1, N_BRANCH, BRANCH_W, D_MODEL), lambda i: (li, 0, 0, 0)),
                  pl.BlockSpec((1, D_MODEL, D_MODEL), lambda i: (li, 0, 0)),
                  ms(gt), ms(sc), ms(sh), row, row],
        out_specs=[blk(D_MODEL), blk(D_MODEL)],
        out_shape=[jax.ShapeDtypeStruct((n, D_MODEL), F32), jax.ShapeDtypeStruct((n, D_MODEL), BF16)],
        compiler_params=_cp(("arbitrary",)),
        name="merge",
    )(oa, ob, y, g, bonus, lnx_g, lnx_b, p2, p2, p2, x, wb, wo, gt, sc, sh, ln_g, ln_b)


def _ffn_finish(acc, x_ref, gt_ref, sc_ref, sh_ref, lg_ref, lb_ref, x2_ref, h_ref):
    x2 = _layer_norm(ALPHA * x_ref[...] + gt_ref[0] * acc, lg_ref[...], lb_ref[...])
    x2_ref[...] = x2
    h_ref[...] = (x2 * (1.0 + sc_ref[0]) + sh_ref[0]).astype(BF16)


def _ffn_body(h_ref, x_ref, w1_ref, w3_ref, w2_ref, gt_ref, sc_ref, sh_ref, lg_ref, lb_ref,
              x2_ref, hn_ref, acc, *, nf):
    f = pl.program_id(1)
    h = h_ref[...]
    u = (_silu(_dot(h, w1_ref[0])) * _dot(h, w3_ref[0])).astype(BF16)
    part = _dot(u, w2_ref[0])

    @pl.when(f == 0)
    def _():
        acc[...] = part

    @pl.when(f > 0)
    def _():
        acc[...] = acc[...] + part

    @pl.when(f == nf - 1)
    def _():
        _ffn_finish(acc[...], x_ref, gt_ref, sc_ref, sh_ref, lg_ref, lb_ref, x2_ref, hn_ref)


def _ffn_dense(h, x, w1, w3, w2, li, gt, sc, sh, ln_g, ln_b, tm, tiles_per_group):
    n = x.shape[0]
    tf = D_FF // 2
    nf = D_FF // tf
    blk = pl.BlockSpec((tm, D_MODEL), lambda i, f: (i, 0))
    row = pl.BlockSpec((1, D_MODEL), lambda i, f: (0, 0))
    ms = lambda m: _mod_spec(m, tm, tiles_per_group)
    return pl.pallas_call(
        functools.partial(_ffn_body, nf=nf),
        grid=(n // tm, nf),
        in_specs=[blk, blk,
                  pl.BlockSpec((1, D_MODEL, tf), lambda i, f: (li, 0, f)),
                  pl.BlockSpec((1, D_MODEL, tf), lambda i, f: (li, 0, f)),
                  pl.BlockSpec((1, tf, D_MODEL), lambda i, f: (li, f, 0)),
                  ms(gt), ms(sc), ms(sh), row, row],
        out_specs=[blk, blk],
        out_shape=[jax.ShapeDtypeStruct((n, D_MODEL), F32), jax.ShapeDtypeStruct((n, D_MODEL), BF16)],
        scratch_shapes=[pltpu.VMEM((tm, D_MODEL), F32)],
        compiler_params=_cp(("arbitrary", "arbitrary")),
        name="ffn_dense",
    )(h, x, w1, w3, w2, gt, sc, sh, ln_g, ln_b)


def _moe_body(h_ref, x_ref, rw_ref, rb_ref, w1_ref, w3_ref, w2_ref, gt_ref, sc_ref, sh_ref, lg_ref, lb_ref,
              x2_ref, hn_ref, acc, gate):
    e = pl.program_id(1)
    h = h_ref[...]
    lane = _iota((h.shape[0], LANES), 1)

    @pl.when(e == 0)
    def _():
        logits = _dot(h, rw_ref[...]) + rb_ref[...]
        m1 = jnp.max(logits, axis=1, keepdims=True)
        i1 = jnp.min(jnp.where(logits == m1, lane, LANES), axis=1, keepdims=True)
        rest = jnp.where(lane == i1, NEG_INF, logits)
        m2 = jnp.max(rest, axis=1, keepdims=True)
        i2 = jnp.min(jnp.where(rest == m2, lane, LANES), axis=1, keepdims=True)
        e2 = jnp.exp(m2 - m1)
        p1 = 1.0 / (1.0 + e2)
        p2 = e2 / (1.0 + e2)
        gate[...] = jnp.where(lane == i1, p1, 0.0) + jnp.where(lane == i2, p2, 0.0)
        acc[...] = jnp.zeros(acc.shape, F32)

    ge = jnp.sum(jnp.where(lane == e, gate[...], 0.0), axis=1, keepdims=True)
    u = (_silu(_dot(h, w1_ref[0, 0])) * _dot(h, w3_ref[0, 0])).astype(BF16)
    acc[...] = acc[...] + ge * _dot(u, w2_ref[0, 0])

    @pl.when(e == N_EXPERTS - 1)
    def _():
        _ffn_finish(acc[...], x_ref, gt_ref, sc_ref, sh_ref, lg_ref, lb_ref, x2_ref, hn_ref)


def _ffn_moe(h, x, rw, rb, w1, w3, w2, li, gt, sc, sh, ln_g, ln_b, tm, tiles_per_group):
    n = x.shape[0]
    blk = pl.BlockSpec((tm, D_MODEL), lambda i, e: (i, 0))
    row = pl.BlockSpec((1, D_MODEL), lambda i, e: (0, 0))
    ms = lambda m: _mod_spec(m, tm, tiles_per_group)
    return pl.pallas_call(
        _moe_body,
        grid=(n // tm, N_EXPERTS),
        in_specs=[blk, blk,
                  pl.BlockSpec((D_MODEL, LANES), lambda i, e: (0, 0)),
                  pl.BlockSpec((1, LANES), lambda i, e: (0, 0)),
                  pl.BlockSpec((1, 1, D_MODEL, D_FF_E), lambda i, e: (li, e, 0, 0)),
                  pl.BlockSpec((1, 1, D_MODEL, D_FF_E), lambda i, e: (li, e, 0, 0)),
                  pl.BlockSpec((1, 1, D_FF_E, D_MODEL), lambda i, e: (li, e, 0, 0)),
                  ms(gt), ms(sc), ms(sh), row, row],
        out_specs=[blk, blk],
        out_shape=[jax.ShapeDtypeStruct((n, D_MODEL), F32), jax.ShapeDtypeStruct((n, D_MODEL), BF16)],
        scratch_shapes=[pltpu.VMEM((tm, D_MODEL), F32), pltpu.VMEM((tm, LANES), F32)],
        compiler_params=_cp(("arbitrary", "arbitrary")),
        name="ffn_moe",
    )(h, x, rw, rb, w1, w3, w2, gt, sc, sh, ln_g, ln_b)


def _pack_w_in(w):
    o_iw = 8 * 512
    o_ik = o_iw + H_I
    o_pc = o_ik + D_I
    o_gl = o_pc + D_C_IN
    z = lambda n: jnp.zeros((w.shape[0], n), w.dtype)
    packed = jnp.concatenate(
        [w[:, :o_iw], w[:, o_ik:o_pc], w[:, o_iw:o_ik], z(LANES - D_I - H_I), z(LANES),
         w[:, o_pc:o_gl], w[:, o_gl:]], axis=-1)
    assert packed.shape[-1] == D_PACK
    return packed


def _scan_state_in(s):
    b = s.shape[0]
    s = s.reshape(b, H_C // 2, 2, D_C, D_C).transpose(0, 1, 3, 2, 4)
    return s.reshape(b * (H_C // 2), D_C, 2 * D_C)


def _scan_state_out(s, b):
    s = s.reshape(b, H_C // 2, D_C, 2, D_C).transpose(0, 1, 3, 2, 4)
    return s.reshape(b, H_C, D_C, D_C)


def _mods(ada_l, lo, hi, per_row_t):
    parts = jnp.split(ada_l[lo:hi], 6, axis=-1)
    if per_row_t is None:
        return [p[:, None, :] for p in parts]
    return [jnp.repeat(p, per_row_t, axis=0)[None] for p in parts]


def _layer(l, x, h, mods, mods_next, prm, attend_fn, ret_s0, wkv_s0, shift_rows, nb, t_len):
    n = nb * t_len
    tm_proj = min(n, TM_PROJ)
    tm = min(n, TM_ROW)
    sh1, sc1, gt1, sh2, sc2, gt2 = mods
    p2 = _in_proj(h, prm["w_in"][l], tm_proj)

    rb = CHUNK_A if t_len % CHUNK_A == 0 else t_len
    nchunk = t_len // rb
    oa, ret_s = _retention(p2.reshape(nb * nchunk, rb, D_PACK), ret_s0, prm["lgam"], prm["ret_gn_g"][l],
                           nb, nchunk, rb)
    oa = oa.reshape(n, H_A * DV_A)

    ob = attend_fn(p2)

    rp = {k: prm["rw_" + k][l] for k in ("mu", "w0", "a0", "kk", "ka", "rk", "wwa", "g2")}
    r, w, k2, v, na, b, g, bonus = _rwkv_prep(p2, shift_rows, rp, tm, t_len)
    tb = min(t_len, SCAN_TB)
    if t_len >= tm:
        tmaj = lambda a: a
    else:
        tmaj = lambda a: a.reshape(nb, t_len, W_C).transpose(1, 0, 2).reshape(t_len, nb * W_C)
    y, wkv_s = _rwkv_scan(tmaj(r), tmaj(w), tmaj(k2), tmaj(v), tmaj(na), tmaj(b), wkv_s0, nb, t_len, tb)
    if t_len < tm:
        y = y.reshape(t_len, nb, W_C).transpose(1, 0, 2).reshape(n, W_C)

    tm_m = tm
    tpg_m = max(t_len // tm_m, 1)
    x1, h2 = _merge(oa, ob, y, g, bonus, prm["rw_lnx_g"][l], prm["rw_lnx_b"][l], t_len, p2, x,
                    prm["w_branch"], prm["w_out"], l, gt1, sc2, sh2,
                    prm["ln1_g"][l], prm["ln1_b"][l], tm_m, tpg_m)
    sh_n, sc_n = mods_next
    i = l // 2
    if l % 2 == 0:
        x2, hn = _ffn_dense(h2, x1, prm["ffn_w1"], prm["ffn_w3"], prm["ffn_w2"], i, gt2, sc_n, sh_n,
                            prm["ln2_g"][l], prm["ln2_b"][l], tm_m, tpg_m)
    else:
        x2, hn = _ffn_moe(h2, x1, prm["moe_router"][i], prm["moe_router_b"][i], prm["moe_w1"],
                          prm["moe_w3"], prm["moe_w2"], i, gt2, sc_n, sh_n,
                          prm["ln2_g"][l], prm["ln2_b"][l], tm_m, tpg_m)
    shift_s = p2.reshape(nb, t_len, D_PACK)[:, -1, U_PC * LANES:U_PC * LANES + D_C_IN]
    if t_len >= TM_PROJ:
        kh, vh, ik = _kv_transposed(h, prm["wk_t"][l], prm["wv_t"][l], prm["wi_t"][l], nb, t_len, TM_PROJ)
    else:
        kh = p2[:, U_KB * LANES:U_KB * LANES + W_B].reshape(nb, t_len, H_B, D_B)
        vh = p2[:, U_VB * LANES:U_VB * LANES + W_B].reshape(nb, t_len, H_B, D_B)
        ik = p2[:, U_MISC * LANES:U_MISC * LANES + D_I].reshape(nb, t_len, D_I)
    return x2, hn, (kh, vh, ik, ret_s, _scan_state_out(wkv_s, nb), shift_s)


def kernel(x_prompt, x_sample, c_prompt, c_sample, cache_k, cache_v, cache_kidx, page_table, state_ret, state_wkv, state_shift, ada_w, ada_b, w_in, ret_gn_g, rw_mu, rw_w0, rw_w2, rw_a0, rw_a2, rw_g2, rw_kk, rw_ka, rw_rk, rw_lnx_g, rw_lnx_b, w_branch, w_out, ln1_g, ln1_b, ln2_g, ln2_b, ffn_w1, ffn_w3, ffn_w2, moe_router, moe_router_b, moe_w1, moe_w3, moe_w2):
    bp, tp, _ = x_prompt.shape
    bs, ts, _ = x_sample.shape
    n_pages = page_table.shape[1]
    n_pool = cache_k.shape[1]

    zw = jnp.zeros((DEPTH, LORA_W, W_C), F32)
    wwa = jnp.concatenate([jnp.concatenate([rw_w2, zw], axis=-1),
                           jnp.concatenate([zw, rw_a2], axis=-1)], axis=1).astype(BF16)
    n_moe = moe_router.shape[0]
    router_pad = jnp.concatenate(
        [moe_router, jnp.zeros((n_moe, D_MODEL, LANES - N_EXPERTS), F32)], axis=-1).astype(BF16)
    router_b_pad = jnp.concatenate(
        [moe_router_b, jnp.full((n_moe, LANES - N_EXPERTS), -1e30, F32)], axis=-1)[:, None, :]
    log_gamma = jnp.log1p(-jnp.exp2(-5.0 - jnp.arange(H_A, dtype=F32)))
    row = lambda a: a.reshape(DEPTH, 1, -1)
    o_k, o_v, o_ik = 5 * W_B, 6 * W_B, 8 * W_B + H_I
    w_in_bf = [w_in[l].astype(BF16) for l in range(DEPTH)]
    w_t = lambda o, w: [wl[:, o:o + w] for wl in w_in_bf]
    zpad = jnp.zeros((D_MODEL, LANES - D_I), BF16)
    prm = dict(
        wk_t=w_t(o_k, W_B), wv_t=w_t(o_v, W_B),
        wi_t=[jnp.concatenate([wl[:, o_ik:o_ik + D_I], zpad], axis=1) for wl in w_in_bf],
        w_in=[_pack_w_in(wl) for wl in w_in_bf], lgam=jnp.repeat(log_gamma, DK_A)[None, :], ret_gn_g=row(ret_gn_g),
        rw_mu=row(rw_mu), rw_w0=row(rw_w0), rw_a0=row(rw_a0), rw_kk=row(rw_kk), rw_ka=row(rw_ka),
        rw_rk=row(rw_rk), rw_wwa=wwa, rw_g2=rw_g2.astype(BF16), rw_lnx_g=row(rw_lnx_g), rw_lnx_b=row(rw_lnx_b),
        w_branch=w_branch.astype(BF16), w_out=w_out.astype(BF16),
        ln1_g=row(ln1_g), ln1_b=row(ln1_b), ln2_g=row(ln2_g), ln2_b=row(ln2_b),
        ffn_w1=ffn_w1.astype(BF16), ffn_w3=ffn_w3.astype(BF16), ffn_w2=ffn_w2.astype(BF16),
        moe_router=router_pad, moe_router_b=router_b_pad,
        moe_w1=moe_w1.astype(BF16), moe_w3=moe_w3.astype(BF16), moe_w2=moe_w2.astype(BF16))

    ada = _ada_all(jnp.concatenate([c_prompt, c_sample], axis=0), ada_w, ada_b)
    mods_p = [_mods(ada[l], 0, bp, None) for l in range(DEPTH)]
    mods_s = [_mods(ada[l], bp, bp + bs, ts) for l in range(DEPTH)]

    np_, ns_ = bp * tp, bs * ts
    tm_p = min(tp, TM_ROW)
    xp = x_prompt.reshape(np_, D_MODEL)
    xs = x_sample.reshape(ns_, D_MODEL)
    hp = _modulate(xp, mods_p[0][1], mods_p[0][0], tm_p, tp // tm_p)
    hs = _modulate(xs, mods_s[0][1], mods_s[0][0], ns_, 1)

    cache_k4 = cache_k.transpose(0, 1, 3, 4, 2).reshape(DEPTH, n_pool, W_B, PAGE_SIZE)
    cache_v4 = cache_v.transpose(0, 1, 3, 4, 2).reshape(DEPTH, n_pool, W_B, PAGE_SIZE)
    cache_kidx_t = cache_kidx.transpose(0, 1, 3, 2)
    zeros_ret = jnp.zeros((bp, H_A, DK_A, DV_A), F32)
    zeros_wkv = jnp.zeros((bp * (H_C // 2), D_C, 2 * D_C), F32)
    zeros_shift = jnp.zeros((bp, 1, D_C_IN), F32)

    st_p, st_s = [], []
    for l in range(DEPTH):
        nxt = min(l + 1, DEPTH - 1)
        attend_p = lambda p2: _dsa_prompt(p2, bp, tp)
        xp, hp, sp = _layer(l, xp, hp, mods_p[l], (mods_p[nxt][0], mods_p[nxt][1]), prm, attend_p,
                            zeros_ret, zeros_wkv, zeros_shift, bp, tp)

        def attend_s(p2, l=l):
            pad = lambda a: jnp.pad(a.reshape(bs, ts, -1).transpose(0, 2, 1),
                                    ((0, 0), (0, 0), (0, PAGE_SIZE - ts)))
            q3 = p2[:, U_QB * LANES:U_QB * LANES + W_B].reshape(bs, ts, W_B)
            iq32 = p2[:, U_IQ * LANES:U_IQ * LANES + W_B].reshape(bs, ts * H_I, D_I)
            iw = p2[:, U_MISC * LANES + IW_OFF:U_MISC * LANES + IW_OFF + H_I].reshape(bs, ts * H_I, 1)
            iwb = jnp.broadcast_to(iw, (bs, ts * H_I, LANES))
            knew = pad(p2[:, U_KB * LANES:U_KB * LANES + W_B])
            vnew = pad(p2[:, U_VB * LANES:U_VB * LANES + W_B])
            iknew = pad(p2[:, U_MISC * LANES:U_MISC * LANES + D_I])
            scores, scores_new = _dsa_sample_scores(l, page_table, cache_kidx_t, iq32, iwb, iknew, ts)
            bias, bias_new = _dsa_sample_select(scores, scores_new, ts)
            o = _dsa_sample_attend(l, page_table, bias, bias_new, q3, knew, vnew, cache_k4, cache_v4, ts)
            return o[:, :ts].reshape(ns_, W_B)

        shift_rows = jnp.repeat(state_shift[l], ts, axis=0)[None]
        xs, hs, ss = _layer(l, xs, hs, mods_s[l], (mods_s[nxt][0], mods_s[nxt][1]), prm, attend_s,
                            state_ret[l], _scan_state_in(state_wkv[l]), shift_rows, bs, ts)
        st_p.append(sp)
        st_s.append(ss)

    outs = [xp.reshape(bp, tp, D_MODEL), xs.reshape(bs, ts, D_MODEL)]
    for j in range(6):
        sp = jnp.stack([s[j] for s in st_p])
        if tp >= TM_PROJ and j < 2:
            sp = sp.reshape(DEPTH, bp, H_B, D_B, tp).transpose(0, 1, 4, 2, 3)
        elif tp >= TM_PROJ and j == 2:
            sp = sp.transpose(0, 1, 3, 2)
        outs.append(sp)
        outs.append(jnp.stack([s[j] for s in st_s]))
    return tuple(outs)
```
